```python
import jax, jax.numpy as jnp
from jax import lax
import numpy as np

D_MODEL = 1024
BATCH = 8
SEQ = 4096
DEPTH = 2

N_MIXERS = 2
CONV_WIDTH = 3
ML_HEADS = 8
ML_QK_DIM = D_MODEL // (2 * ML_HEADS)
ML_V_DIM = D_MODEL // ML_HEADS
ML_QK_W = ML_HEADS * ML_QK_DIM
ML_V_W = ML_HEADS * ML_V_DIM
ML_PROJ_W = 2 * ML_QK_W + 2 * ML_V_W + 2 * ML_HEADS
ML_CHUNK = 64
N_EXPERTS = 16
N_GROUPS = 4
EXPERTS_PER_GROUP = N_EXPERTS // N_GROUPS
TOP_K = 2
D_EXPERT = 3 * D_MODEL // 2
DISPATCH_BLOCK = 256
ALPHA = (2 * DEPTH) ** 0.25
BETA = (8 * DEPTH) ** -0.25
LN_EPS = 1e-5
HEAD_NORM_EPS = 1e-6
N_CONV_LAYERS = (DEPTH + 1) // 2
N_MLSTM_LAYERS = DEPTH // 2

kernel_name = "hybrid_conv_mlstm_shared_router_moe_deepnorm"


def layer_norm(x, g, b):
    xf = x.astype(jnp.float32)
    mu = jnp.mean(xf, axis=-1, keepdims=True)
    xc = xf - mu
    var = jnp.mean(xc * xc, axis=-1, keepdims=True)
    return (xc * lax.rsqrt(var + LN_EPS) * g.astype(jnp.float32) + b.astype(jnp.float32)).astype(x.dtype)


def short_conv_mixer(x, w_in, w_conv, w_out):
    d = x.shape[-1]
    proj = x @ w_in
    gate_b, gate_c, h = jnp.split(proj, 3, axis=-1)
    u = gate_c * h
    conv = lax.conv_general_dilated(
        u, w_conv[:, None, :], window_strides=(1,), padding=[(CONV_WIDTH - 1, 0)],
        dimension_numbers=("NWC", "WIO", "NWC"), feature_group_count=d)
    return (gate_b * conv) @ w_out


def mlstm_chunkwise(q, k, v, log_i, log_f):
    bsz, nh, s, dk = q.shape
    dv = v.shape[-1]
    L = ML_CHUNK
    nc = s // L

    def to_chunks(a):
        a = a.reshape(a.shape[:2] + (nc, L) + a.shape[3:])
        return jnp.moveaxis(a, 2, 0)

    causal = jnp.tril(jnp.ones((L, L), dtype=bool))

    def step(carry, inp):
        c_state, n_state, m_state = carry
        qc, kc, vc, ic, fc = inp
        b = jnp.cumsum(fc, axis=-1)
        d_mat = b[..., :, None] - b[..., None, :] + ic[..., None, :]
        d_mat = jnp.where(causal, d_mat, -jnp.inf)
        m_inter = b + m_state[..., None]
        m_t = jnp.maximum(m_inter, jnp.max(d_mat, axis=-1))
        w_inter = jnp.exp(m_inter - m_t)
        p = jnp.exp(d_mat - m_t[..., None])
        a = p * jnp.einsum("bhld,bhrd->bhlr", qc, kc)
        num = jnp.einsum("bhlr,bhrv->bhlv", a, vc) + w_inter[..., None] * jnp.einsum("bhld,bhdv->bhlv", qc, c_state)
        den = jnp.sum(a, axis=-1) + w_inter * jnp.einsum("bhld,bhd->bhl", qc, n_state)
        h = num / jnp.maximum(jnp.abs(den), jnp.exp(-m_t))[..., None]
        m_new = m_t[..., -1]
        g_state = jnp.exp(b[..., -1] + m_state - m_new)
        g_r = jnp.exp(b[..., -1:] - b + ic - m_new[..., None])
        c_new = g_state[..., None, None] * c_state + jnp.einsum("bhr,bhrd,bhrv->bhdv", g_r, kc, vc)
        n_new = g_state[..., None] * n_state + jnp.einsum("bhr,bhrd->bhd", g_r, kc)
        return (c_new, n_new, m_new), h

    init = (jnp.zeros((bsz, nh, dk, dv), jnp.float32),
            jnp.zeros((bsz, nh, dk), jnp.float32),
            jnp.zeros((bsz, nh), jnp.float32))
    _, hs = lax.scan(step, init, tuple(to_chunks(a) for a in (q, k, v, log_i, log_f)))
    hs = jnp.moveaxis(hs, 0, 2)
    return hs.reshape(bsz, nh, s, dv)


def mlstm_mixer(x, w_in, b_gate, norm_g, w_out):
    bsz, s, d = x.shape
    proj = x @ w_in
    q, k, v, o_pre, gates = jnp.split(
        proj, [ML_QK_W, 2 * ML_QK_W, 2 * ML_QK_W + ML_V_W, 2 * ML_QK_W + 2 * ML_V_W], axis=-1)
    f32 = jnp.float32
    heads = lambda a, hd: a.reshape(bsz, s, ML_HEADS, hd).transpose(0, 2, 1, 3).astype(f32)
    q = heads(q, ML_QK_DIM) * (ML_QK_DIM ** -0.5)
    k = heads(k, ML_QK_DIM)
    v = heads(v, ML_V_DIM)
    gates = (gates.astype(f32) + b_gate.astype(f32)).transpose(0, 2, 1)
    log_i = gates[:, :ML_HEADS]
    log_f = jax.nn.log_sigmoid(gates[:, ML_HEADS:])
    h = mlstm_chunkwise(q, k, v, log_i, log_f).transpose(0, 2, 1, 3)
    h = h * lax.rsqrt(jnp.mean(h * h, axis=-1, keepdims=True) + HEAD_NORM_EPS)
    h = h * norm_g.astype(f32).reshape(ML_HEADS, ML_V_DIM)
    o = jax.nn.sigmoid(o_pre.astype(f32)).reshape(bsz, s, ML_HEADS, ML_V_DIM)
    return (o * h).reshape(bsz, s, d).astype(x.dtype) @ w_out


def grouped_moe(x, router_w, router_b, w_gate, w_up, w_down):
    bsz, s, d = x.shape
    t = bsz * s
    xt = x.reshape(t, d)
    probs = jax.nn.softmax((xt @ router_w).astype(jnp.float32), axis=-1)
    sel = (probs + router_b.astype(jnp.float32)).reshape(t, N_GROUPS, EXPERTS_PER_GROUP)
    group_score = jnp.sum(lax.top_k(sel, TOP_K)[0], axis=-1)
    g_best = jnp.argmax(group_score, axis=-1)
    within = jnp.take_along_axis(sel, g_best[:, None, None], axis=1)[:, 0]
    _, local_idx = lax.top_k(within, TOP_K)
    expert_idx = g_best[:, None] * EXPERTS_PER_GROUP + local_idx
    gate = jnp.take_along_axis(probs, expert_idx, axis=1)
    gate = gate / jnp.sum(gate, axis=-1, keepdims=True)

    n_assign = t * TOP_K
    e_flat = expert_idx.reshape(-1).astype(jnp.int32)
    tok_flat = jnp.repeat(jnp.arange(t, dtype=jnp.int32), TOP_K)
    g_flat = gate.reshape(-1)
    order = jnp.argsort(e_flat)
    e_s, tok_s, g_s = e_flat[order], tok_flat[order], g_flat[order]
    counts = jnp.bincount(e_flat, length=N_EXPERTS).astype(jnp.int32)
    starts = jnp.cumsum(counts) - counts
    padded = (counts + DISPATCH_BLOCK - 1) // DISPATCH_BLOCK * DISPATCH_BLOCK
    pends = jnp.cumsum(padded)
    pstarts = pends - padded
    dest = pstarts[e_s] + (jnp.arange(n_assign, dtype=jnp.int32) - starts[e_s])
    n_rows = -(-n_assign // DISPATCH_BLOCK) * DISPATCH_BLOCK + N_EXPERTS * DISPATCH_BLOCK
    n_blocks = n_rows // DISPATCH_BLOCK
    block_start = jnp.arange(n_blocks, dtype=jnp.int32) * DISPATCH_BLOCK
    block_e = jnp.minimum(jnp.searchsorted(pends, block_start, side="right"), N_EXPERTS - 1)
    x_buf = jnp.zeros((n_rows, d), x.dtype).at[dest].set(xt[tok_s])

    def expert_block(args):
        xb, e = args
        hb = jax.nn.silu(xb @ w_gate[e]) * (xb @ w_up[e])
        return hb @ w_down[e]

    y_buf = lax.map(expert_block, (x_buf.reshape(n_blocks, DISPATCH_BLOCK, d), block_e))
    y_buf = y_buf.reshape(n_rows, d)
    y = jnp.zeros((t, d), x.dtype).at[tok_s].add(y_buf[dest] * g_s[:, None].astype(x.dtype))
    return y.reshape(bsz, s, d)


def setup_inputs(seed: int = 0) -> dict:
    key = jax.random.key(seed)
    ks = jax.random.split(key, 20)
    nrm = lambda k, shape, scale: jax.random.normal(k, shape, jnp.float32) * scale
    d = D_MODEL
    f_bias = jnp.linspace(3.0, 6.0, ML_HEADS, dtype=jnp.float32)
    b_gate = jnp.concatenate([
        nrm(ks[5], (N_MLSTM_LAYERS, ML_HEADS), 0.1),
        f_bias[None, :] + nrm(ks[6], (N_MLSTM_LAYERS, ML_HEADS), 0.1)], axis=-1)
    return {
        "x": nrm(ks[0], (BATCH, SEQ, d), 1.0),
        "conv_w_in": nrm(ks[1], (N_CONV_LAYERS, d, 3 * d), d ** -0.5),
        "conv_w": nrm(ks[2], (N_CONV_LAYERS, CONV_WIDTH, d), CONV_WIDTH ** -0.5),
        "conv_w_out": nrm(ks[3], (N_CONV_LAYERS, d, d), BETA * d ** -0.5),
        "ml_w_in": jnp.concatenate([
            nrm(ks[4], (N_MLSTM_LAYERS, d, 2 * ML_QK_W), d ** -0.5),
            nrm(ks[7], (N_MLSTM_LAYERS, d, ML_V_W), BETA * d ** -0.5),
            nrm(ks[8], (N_MLSTM_LAYERS, d, ML_V_W + 2 * ML_HEADS), d ** -0.5)], axis=-1),
        "ml_b_gate": b_gate,
        "ml_norm_g": 1.0 + nrm(ks[9], (N_MLSTM_LAYERS, ML_V_W), 0.02),
        "ml_w_out": nrm(ks[10], (N_MLSTM_LAYERS, d, d), BETA * d ** -0.5),
        "ln_mix_g": 1.0 + nrm(ks[11], (DEPTH, d), 0.02),
        "ln_mix_b": nrm(ks[12], (DEPTH, d), 0.02),
        "ln_ffn_g": 1.0 + nrm(ks[13], (DEPTH, d), 0.02),
        "ln_ffn_b": nrm(ks[14], (DEPTH, d), 0.02),
        "router_w": nrm(ks[15], (d, N_EXPERTS), d ** -0.5),
        "router_b": nrm(ks[16], (N_EXPERTS,), 0.01),
        "exp_w_gate": nrm(ks[17], (DEPTH, N_EXPERTS, d, D_EXPERT), BETA * d ** -0.5),
        "exp_w_up": nrm(ks[18], (DEPTH, N_EXPERTS, d, D_EXPERT), BETA * d ** -0.5),
        "exp_w_down": nrm(ks[19], (DEPTH, N_EXPERTS, D_EXPERT, d), BETA * D_EXPERT ** -0.5),
    }


def reference(x, conv_w_in, conv_w, conv_w_out, ml_w_in, ml_b_gate, ml_norm_g, ml_w_out,
              ln_mix_g, ln_mix_b, ln_ffn_g, ln_ffn_b, router_w, router_b,
              exp_w_gate, exp_w_up, exp_w_down):
    for i in range(DEPTH):
        j = i // N_MIXERS
        if i % N_MIXERS == 0:
            mix = short_conv_mixer(x, conv_w_in[j], conv_w[j], conv_w_out[j])
        else:
            mix = mlstm_mixer(x, ml_w_in[j], ml_b_gate[j], ml_norm_g[j], ml_w_out[j])
        x = layer_norm(ALPHA * x + mix, ln_mix_g[i], ln_mix_b[i])
        ffn = grouped_moe(x, router_w, router_b, exp_w_gate[i], exp_w_up[i], exp_w_down[i])
        x = layer_norm(ALPHA * x + ffn, ln_ffn_g[i], ln_ffn_b[i])
    return x
```

```python
import functools
import itertools

import numpy as np
import jax
import jax.numpy as jnp
from jax import lax
from jax.experimental import pallas as pl
from jax.experimental.pallas import tpu as pltpu

D_MODEL = 1024
DEPTH = 2
CONV_WIDTH = 3
ML_HEADS = 8
ML_QK_DIM = D_MODEL // (2 * ML_HEADS)
ML_V_DIM = D_MODEL // ML_HEADS
ML_QK_W = ML_HEADS * ML_QK_DIM
ML_V_W = ML_HEADS * ML_V_DIM
N_EXPERTS = 16
N_GROUPS = 4
EXPERTS_PER_GROUP = N_EXPERTS // N_GROUPS
D_EXPERT = 3 * D_MODEL // 2
ALPHA = (2 * DEPTH) ** 0.25
LN_EPS = 1e-5
HEAD_NORM_EPS = 1e-6

LANES = 128
SUBLANES = 8
VMEM_LIMIT_BYTES = 56 * 1024 * 1024

PAIRS = tuple(itertools.combinations(range(EXPERTS_PER_GROUP), 2))
N_PAIRS = len(PAIRS)
N_CLASSES = N_GROUPS * N_PAIRS
CLASS_E_LO = np.array([g * EXPERTS_PER_GROUP + a for g in range(N_GROUPS) for a, _ in PAIRS], np.int32)
CLASS_E_HI = np.array([g * EXPERTS_PER_GROUP + b for g in range(N_GROUPS) for _, b in PAIRS], np.int32)

META_G_LO, META_G_HI, META_CLASS, META_RANK = 0, 1, 2, 3
ROW_W = D_MODEL + LANES

MIX_TILE = 512
EXPERT_BLOCK = 256
ML_CHUNK = 128


def _layer_norm(z, g, b):
    mu = jnp.mean(z, axis=-1, keepdims=True)
    zc = z - mu
    var = jnp.mean(zc * zc, axis=-1, keepdims=True)
    return zc * lax.rsqrt(var + LN_EPS) * g + b


def _split_bf16(a):
    hi = a.astype(jnp.bfloat16)
    lo = (a - hi.astype(jnp.float32)).astype(jnp.bfloat16)
    return hi, lo


def _group_member(v, lane, s):
    pos = lane & (EXPERTS_PER_GROUP - 1)
    ahead = pltpu.roll(v, LANES - s, axis=1)
    behind = pltpu.roll(v, EXPERTS_PER_GROUP - s, axis=1)
    return jnp.where(pos + s < EXPERTS_PER_GROUP, ahead, behind)


def _route_and_rank(x1, rw_hi, rw_lo, rb, count_ref):
    ts = x1.shape[0]
    f32 = jnp.float32
    lane = lax.broadcasted_iota(jnp.int32, (ts, LANES), 1)
    is_expert = lane < N_EXPERTS

    x_hi, x_lo = _split_bf16(x1)
    logits = (jnp.dot(x_hi, rw_hi, preferred_element_type=f32)
              + jnp.dot(x_hi, rw_lo, preferred_element_type=f32)
              + jnp.dot(x_lo, rw_hi, preferred_element_type=f32))
    logits = jnp.where(is_expert, logits, -1e30)
    mx = jnp.max(logits, axis=-1, keepdims=True)
    ex = jnp.where(is_expert, jnp.exp(logits - mx), 0.0)
    probs = ex / jnp.sum(ex, axis=-1, keepdims=True)
    sel = probs + rb

    pos = lane & (EXPERTS_PER_GROUP - 1)
    beaten_by = jnp.zeros((ts, LANES), jnp.int32)
    for s in range(1, EXPERTS_PER_GROUP):
        other = _group_member(sel, lane, s)
        other_is_lower = pos + s >= EXPERTS_PER_GROUP
        wins = (other > sel) | ((other == sel) & other_is_lower)
        beaten_by = beaten_by + wins.astype(jnp.int32)
    in_top2 = beaten_by < 2
    top2_val = jnp.where(in_top2, sel, 0.0)
    group_score = top2_val
    for s in range(1, EXPERTS_PER_GROUP):
        group_score = group_score + _group_member(top2_val, lane, s)
    group_score = jnp.where(is_expert, group_score, -1e30)
    best = jnp.max(group_score, axis=-1, keepdims=True)
    group_of_lane = lane >> 2
    g_best = jnp.min(jnp.where((group_score == best) & is_expert, group_of_lane, N_GROUPS),
                     axis=-1, keepdims=True)
    chosen = (group_of_lane == g_best) & in_top2 & is_expert
    e_lo = jnp.min(jnp.where(chosen, lane, LANES), axis=-1, keepdims=True)
    e_hi = jnp.max(jnp.where(chosen, lane, -1), axis=-1, keepdims=True)
    p_lo = jnp.sum(jnp.where(lane == e_lo, probs, 0.0), axis=-1, keepdims=True)
    p_hi = jnp.sum(jnp.where(lane == e_hi, probs, 0.0), axis=-1, keepdims=True)
    g_lo = p_lo / (p_lo + p_hi)
    g_hi = p_hi / (p_lo + p_hi)
    a = e_lo & (EXPERTS_PER_GROUP - 1)
    b = e_hi & (EXPERTS_PER_GROUP - 1)
    pair = ((a * (2 * EXPERTS_PER_GROUP - 1 - a)) >> 1) + (b - a - 1)
    cls = g_best * N_PAIRS + pair

    onehot = jnp.where(lane == cls, 1.0, 0.0)
    row = lax.broadcasted_iota(jnp.int32, (ts, ts), 0)
    col = lax.broadcasted_iota(jnp.int32, (ts, ts), 1)
    earlier = jnp.where(col < row, 1.0, 0.0).astype(jnp.bfloat16)
    before = jnp.dot(earlier, onehot.astype(jnp.bfloat16), preferred_element_type=f32)
    running = count_ref[0:1, :]
    rank = jnp.sum(onehot * (before + running), axis=-1, keepdims=True)
    count_ref[...] = jnp.broadcast_to(running + jnp.sum(onehot, axis=0, keepdims=True), count_ref.shape)

    meta = jnp.where(lane == META_G_LO, g_lo,
           jnp.where(lane == META_G_HI, g_hi,
           jnp.where(lane == META_CLASS, cls.astype(f32),
           jnp.where(lane == META_RANK, rank, 0.0))))
    return meta


def _conv_mixer_kernel(x_ref, w_in_ref, w_conv_ref, w_out_ref, ln_g_ref, ln_b_ref,
                       rw_hi_ref, rw_lo_ref, rb_ref, out_ref, count_ref, carry_ref, *, tiles_per_seq):
    i = pl.program_id(0)

    @pl.when(i == 0)
    def _():
        count_ref[...] = jnp.zeros_like(count_ref)

    @pl.when(i % tiles_per_seq == 0)
    def _():
        carry_ref[...] = jnp.zeros_like(carry_ref)

    f32 = jnp.float32
    d = D_MODEL
    x = x_ref[...]
    ts = x.shape[0]
    proj = jnp.dot(x.astype(jnp.bfloat16), w_in_ref[...], preferred_element_type=f32)
    gate_b, gate_c, h = proj[:, :d], proj[:, d:2 * d], proj[:, 2 * d:]
    u = gate_c * h
    row = lax.broadcasted_iota(jnp.int32, (ts, d), 0)
    prev = carry_ref[...]
    u1 = jnp.where(row == 0, prev[SUBLANES - 1:SUBLANES, :], pltpu.roll(u, 1, axis=0))
    u2 = jnp.where(row == 0, prev[SUBLANES - 2:SUBLANES - 1, :],
                   jnp.where(row == 1, prev[SUBLANES - 1:SUBLANES, :], pltpu.roll(u, 2, axis=0)))
    carry_ref[...] = u[ts - SUBLANES:, :]
    wc = w_conv_ref[...]
    conv = wc[0:1, :] * u2 + wc[1:2, :] * u1 + wc[2:3, :] * u
    mix = jnp.dot((gate_b * conv).astype(jnp.bfloat16), w_out_ref[...], preferred_element_type=f32)
    x1 = _layer_norm(ALPHA * x + mix, ln_g_ref[...], ln_b_ref[...])
    out_ref[:, :d] = x1
    out_ref[:, d:] = _route_and_rank(x1, rw_hi_ref[...], rw_lo_ref[...], rb_ref[...], count_ref)


def _const_spec(shape):
    return pl.BlockSpec(shape, lambda i: (0,) * len(shape))


def _pad_rows(a, rows=SUBLANES):
    return jnp.pad(a, ((0, rows - a.shape[0]), (0, 0)))


def _prep_router(router_w, router_b):
    rw = jnp.pad(router_w.astype(jnp.float32), ((0, 0), (0, LANES - N_EXPERTS)))
    rb = jnp.pad(router_b.astype(jnp.float32), (0, LANES - N_EXPERTS))[None, :]
    return _split_bf16(rw), rb


def _conv_mixer_layer(x2d, seq, w_in, w_conv, w_out, ln_g, ln_b, rw_hi, rw_lo, rb):
    t, d = x2d.shape
    ts = MIX_TILE
    grid = (t // ts,)
    return pl.pallas_call(
        functools.partial(_conv_mixer_kernel, tiles_per_seq=seq // ts),
        out_shape=(jax.ShapeDtypeStruct((t, ROW_W), jnp.float32),
                   jax.ShapeDtypeStruct((SUBLANES, LANES), jnp.float32)),
        grid=grid,
        in_specs=[pl.BlockSpec((ts, d), lambda i: (i, 0)),
                  _const_spec((d, 3 * d)), _const_spec((SUBLANES, d)), _const_spec((d, d)),
                  _const_spec((1, d)), _const_spec((1, d)),
                  _const_spec((d, LANES)), _const_spec((d, LANES)), _const_spec((1, LANES))],
        out_specs=(pl.BlockSpec((ts, ROW_W), lambda i: (i, 0)),
                   _const_spec((SUBLANES, LANES))),
        scratch_shapes=[pltpu.VMEM((SUBLANES, d), jnp.float32)],
        compiler_params=pltpu.CompilerParams(dimension_semantics=("arbitrary",),
                                             vmem_limit_bytes=VMEM_LIMIT_BYTES),
        name="conv_mixer_route",
    )(x2d, w_in, w_conv, w_out, ln_g, ln_b, rw_hi, rw_lo, rb)


def _log_sigmoid(z):
    return jnp.minimum(z, 0.0) - jnp.log1p(jnp.exp(-jnp.abs(z)))


def _mlstm_mixer_kernel(x_ref, w_qkvo_ref, wg_hi_ref, wg_lo_ref, wgt_hi_ref, wgt_lo_ref,
                        bg_col_ref, bg_row_ref, norm_g_ref, w_out_ref, ln_g_ref, ln_b_ref,
                        rw_hi_ref, rw_lo_ref, rb_ref, out_ref, count_ref,
                        proj_ref, gcol_ref, grow_ref, h_ref, c_ref, m_ref, *, tiles_per_seq):
    i = pl.program_id(0)
    f32, bf16 = jnp.float32, jnp.bfloat16
    nh, dk, dv, d = ML_HEADS, ML_QK_DIM, ML_V_DIM, D_MODEL
    L = ML_CHUNK
    ts = x_ref.shape[0]
    n_chunks = ts // L

    @pl.when(i == 0)
    def _():
        count_ref[...] = jnp.zeros_like(count_ref)

    @pl.when(i % tiles_per_seq == 0)
    def _():
        c_ref[...] = jnp.zeros_like(c_ref)
        m_ref[...] = jnp.zeros_like(m_ref)

    x = x_ref[...]
    x_hi, x_lo = _split_bf16(x)
    proj_ref[...] = jnp.dot(x_hi, w_qkvo_ref[...], preferred_element_type=f32)

    g_col = (jnp.dot(x_hi, wg_hi_ref[...], preferred_element_type=f32)
             + jnp.dot(x_hi, wg_lo_ref[...], preferred_element_type=f32)
             + jnp.dot(x_lo, wg_hi_ref[...], preferred_element_type=f32)) + bg_col_ref[...]
    nt = (((1,), (1,)), ((), ()))
    g_row = (lax.dot_general(wgt_hi_ref[...], x_hi, nt, preferred_element_type=f32)
             + lax.dot_general(wgt_lo_ref[...], x_hi, nt, preferred_element_type=f32)
             + lax.dot_general(wgt_hi_ref[...], x_lo, nt, preferred_element_type=f32)) + bg_row_ref[...]

    r_i = lax.broadcasted_iota(jnp.int32, (ts, ts), 0)
    c_i = lax.broadcasted_iota(jnp.int32, (ts, ts), 1)
    same_chunk = (r_i // L) == (c_i // L)
    tri_col = jnp.where(same_chunk & (c_i <= r_i), 1.0, 0.0).astype(bf16)
    tri_row = jnp.where(same_chunk & (r_i <= c_i), 1.0, 0.0).astype(bf16)
    lf_col_hi, lf_col_lo = _split_bf16(_log_sigmoid(g_col))
    cum_col = (jnp.dot(tri_col, lf_col_hi, preferred_element_type=f32)
               + jnp.dot(tri_col, lf_col_lo, preferred_element_type=f32))
    lf_row_hi, lf_row_lo = _split_bf16(_log_sigmoid(g_row))
    cum_row = (jnp.dot(lf_row_hi, tri_row, preferred_element_type=f32)
               + jnp.dot(lf_row_lo, tri_row, preferred_element_type=f32))
    lane = lax.broadcasted_iota(jnp.int32, (ts, LANES), 1)
    gcol_ref[...] = jnp.where(lane < nh, g_col, cum_col)
    sub = lax.broadcasted_iota(jnp.int32, (2 * nh, ts), 0)
    g_row_all = jnp.where(sub < nh, g_row, cum_row)
    for c in range(n_chunks):
        grow_ref[c] = g_row_all[:, c * L:(c + 1) * L]

    causal = lax.broadcasted_iota(jnp.int32, (L, L), 0) >= lax.broadcasted_iota(jnp.int32, (L, L), 1)
    ones_col = jnp.where(lax.broadcasted_iota(jnp.int32, (L, dv), 1) == 0, 1.0, 0.0).astype(bf16)
    tn = (((0,), (0,)), ((), ()))

    def chunk_body(c, carry):
        r0 = pl.multiple_of(c * L, L)
        gc = gcol_ref[pl.ds(r0, L), :]
        gr = grow_ref[c]
        for h in range(nh):
            i_col, b_col = gc[:, h:h + 1], gc[:, nh + h:nh + h + 1]
            i_row, b_row = gr[h:h + 1, :], gr[nh + h:nh + h + 1, :]
            m_prev = m_ref[h][0:1, 0:1]
            q = (proj_ref[pl.ds(r0, L), h * dk:(h + 1) * dk] * (dk ** -0.5)).astype(bf16)
            k32 = proj_ref[pl.ds(r0, L), ML_QK_W + h * dk:ML_QK_W + (h + 1) * dk]
            v = proj_ref[pl.ds(r0, L), 2 * ML_QK_W + h * dv:2 * ML_QK_W + (h + 1) * dv].astype(bf16)
            o_pre = proj_ref[pl.ds(r0, L), 2 * ML_QK_W + ML_V_W + h * dv:2 * ML_QK_W + ML_V_W + (h + 1) * dv]
            v_ext = jnp.concatenate([v, ones_col], axis=1)

            dmat = jnp.where(causal, b_col - b_row + i_row, -jnp.inf)
            m_inter = b_col + m_prev
            m_t = jnp.maximum(m_inter, jnp.max(dmat, axis=-1, keepdims=True))
            w_inter = jnp.exp(m_inter - m_t)
            p = jnp.exp(dmat - m_t)
            s = lax.dot_general(q, k32.astype(bf16), nt, preferred_element_type=f32)
            a = (p * s).astype(bf16)
            c_prev = c_ref[h]
            tot = (jnp.dot(a, v_ext, preferred_element_type=f32)
                   + w_inter * jnp.dot(q, c_prev.astype(bf16), preferred_element_type=f32))
            den = tot[:, dv:dv + 1]
            hh = tot[:, :dv] / jnp.maximum(jnp.abs(den), jnp.exp(-m_t))

            m_new = m_t[L - 1:L, :]
            b_last = b_col[L - 1:L, :]
            g_state = jnp.exp(b_last + m_prev - m_new)
            g_r = jnp.exp(b_last - b_col + i_col - m_new)
            kg = (k32 * g_r).astype(bf16)
            c_ref[h] = g_state * c_prev + lax.dot_general(kg, v_ext, tn, preferred_element_type=f32)
            m_ref[h] = jnp.broadcast_to(m_new, m_ref.shape[1:])

            hn = hh * lax.rsqrt(jnp.mean(hh * hh, axis=-1, keepdims=True) + HEAD_NORM_EPS)
            hn = hn * norm_g_ref[:, h * dv:(h + 1) * dv]
            h_ref[pl.ds(r0, L), h * dv:(h + 1) * dv] = jax.nn.sigmoid(o_pre) * hn
        return carry

    lax.fori_loop(0, n_chunks, chunk_body, 0)

    mix = jnp.dot(h_ref[...].astype(bf16), w_out_ref[...], preferred_element_type=f32)
    x1 = _layer_norm(ALPHA * x + mix, ln_g_ref[...], ln_b_ref[...])
    out_ref[:, :d] = x1
    out_ref[:, d:] = _route_and_rank(x1, rw_hi_ref[...], rw_lo_ref[...], rb_ref[...], count_ref)


def _mlstm_mixer_layer(x2d, seq, w_in, b_gate, norm_g, w_out, ln_g, ln_b, rw_hi, rw_lo, rb):
    t, d = x2d.shape
    ts = MIX_TILE
    nh = ML_HEADS
    n_qkvo = 2 * ML_QK_W + 2 * ML_V_W
    w_qkvo = w_in[:, :n_qkvo].astype(jnp.bfloat16)
    w_g = w_in[:, n_qkvo:].astype(jnp.float32)
    wg_hi, wg_lo = _split_bf16(jnp.pad(w_g, ((0, 0), (0, LANES - 2 * nh))))
    wgt_hi, wgt_lo = _split_bf16(w_g.T)
    bg = b_gate.astype(jnp.float32)
    bg_col = jnp.pad(bg, (0, LANES - 2 * nh))[None, :]
    bg_row = bg[:, None]
    return pl.pallas_call(
        functools.partial(_mlstm_mixer_kernel, tiles_per_seq=seq // ts),
        out_shape=(jax.ShapeDtypeStruct((t, ROW_W), jnp.float32),
                   jax.ShapeDtypeStruct((SUBLANES, LANES), jnp.float32)),
        grid=(t // ts,),
        in_specs=[pl.BlockSpec((ts, d), lambda i: (i, 0)),
                  _const_spec((d, n_qkvo)),
                  _const_spec((d, LANES)), _const_spec((d, LANES)),
                  _const_spec((2 * nh, d)), _const_spec((2 * nh, d)),
                  _const_spec((1, LANES)), _const_spec((2 * nh, 1)),
                  _const_spec((1, d)), _const_spec((d, d)),
                  _const_spec((1, d)), _const_spec((1, d)),
                  _const_spec((d, LANES)), _const_spec((d, LANES)), _const_spec((1, LANES))],
        out_specs=(pl.BlockSpec((ts, ROW_W), lambda i: (i, 0)),
                   _const_spec((SUBLANES, LANES))),
        scratch_shapes=[pltpu.VMEM((ts, n_qkvo), jnp.float32),
                        pltpu.VMEM((ts, LANES), jnp.float32),
                        pltpu.VMEM((ts // ML_CHUNK, 2 * nh, ML_CHUNK), jnp.float32),
                        pltpu.VMEM((ts, d), jnp.float32),
                        pltpu.VMEM((nh, ML_QK_DIM, 2 * ML_V_DIM), jnp.float32),
                        pltpu.VMEM((nh, SUBLANES, LANES), jnp.float32)],
        compiler_params=pltpu.CompilerParams(dimension_semantics=("arbitrary",),
                                             vmem_limit_bytes=VMEM_LIMIT_BYTES),
        name="mlstm_mixer_route",
    )(x2d, w_qkvo, wg_hi, wg_lo, wgt_hi, wgt_lo, bg_col, bg_row, norm_g[None, :], w_out.astype(jnp.bfloat16),
      ln_g, ln_b, rw_hi, rw_lo, rb)


def _expert_kernel(tok_ref, elo_ref, ehi_ref, nvalid_ref,
                   x_hbm, wg_lo_ref, wu_lo_ref, wd_lo_ref, wg_hi_ref, wu_hi_ref, wd_hi_ref,
                   ln_g_ref, ln_b_ref, out_hbm, xbuf, obuf, gather_sem, scatter_sem):
    i = pl.program_id(0)
    nb = pl.num_programs(0)
    bm = xbuf.shape[1]
    d = D_MODEL
    slot = i % 2

    def gather_copy(j, s, r):
        tok = tok_ref[j * bm + r]
        return pltpu.make_async_copy(x_hbm.at[pl.ds(tok, 1)], xbuf.at[s, pl.ds(r, 1)], gather_sem.at[s])

    def scatter_copy(j, s, r):
        tok = tok_ref[j * bm + r]
        return pltpu.make_async_copy(obuf.at[s, pl.ds(r, 1)], out_hbm.at[pl.ds(tok, 1)], scatter_sem.at[s])

    def for_rows(j, fn):
        def body(r, c):
            fn(r)
            return c
        lax.fori_loop(0, nvalid_ref[j], body, 0)

    @pl.when(i == 0)
    def _():
        xbuf[...] = jnp.zeros_like(xbuf)
        for_rows(0, lambda r: gather_copy(0, 0, r).start())

    @pl.when(i + 1 < nb)
    def _():
        for_rows(i + 1, lambda r: gather_copy(i + 1, 1 - slot, r).start())

    for_rows(i, lambda r: gather_copy(i, slot, r).wait())

    @pl.when(nvalid_ref[i] > 0)
    def _():
        xb = xbuf[slot]
        x = xb[:, :d]
        g_lo = xb[:, d + META_G_LO:d + META_G_LO + 1]
        g_hi = xb[:, d + META_G_HI:d + META_G_HI + 1]
        x16 = x.astype(jnp.bfloat16)

        def ffn(wg_ref, wu_ref, wd_ref):
            g = jnp.dot(x16, wg_ref[0], preferred_element_type=jnp.float32)
            u = jnp.dot(x16, wu_ref[0], preferred_element_type=jnp.float32)
            h = (g * jax.nn.sigmoid(g)) * u
            return jnp.dot(h.astype(jnp.bfloat16), wd_ref[0], preferred_element_type=jnp.float32)

        y = g_lo * ffn(wg_lo_ref, wu_lo_ref, wd_lo_ref) + g_hi * ffn(wg_hi_ref, wu_hi_ref, wd_hi_ref)
        obuf[slot] = _layer_norm(ALPHA * x + y, ln_g_ref[...], ln_b_ref[...])
        for_rows(i, lambda r: scatter_copy(i, slot, r).start())

    @pl.when(i > 0)
    def _():
        for_rows(i - 1, lambda r: scatter_copy(i - 1, 1 - slot, r).wait())

    @pl.when(i == nb - 1)
    def _():
        for_rows(i, lambda r: scatter_copy(i, slot, r).wait())


def _moe_layer(xext, counts, w_gate, w_up, w_down, ln_g, ln_b):
    t = xext.shape[0]
    d, f, bm = D_MODEL, D_EXPERT, EXPERT_BLOCK
    n_blocks = t // bm + N_CLASSES
    n_rows = n_blocks * bm

    cls = xext[:, d + META_CLASS].astype(jnp.int32)
    rank = xext[:, d + META_RANK].astype(jnp.int32)
    cnt = counts[0, :N_CLASSES].astype(jnp.int32)
    cls_blocks = (cnt + bm - 1) // bm
    blk_end = jnp.cumsum(cls_blocks)
    blk_start = blk_end - cls_blocks
    total_blocks = blk_end[-1]
    dest = blk_start[cls] * bm + rank
    tok_of_row = jnp.zeros((n_rows,), jnp.int32).at[dest].set(
        jnp.arange(t, dtype=jnp.int32), unique_indices=True)
    blk = jnp.arange(n_blocks, dtype=jnp.int32)
    blk_cls = jnp.searchsorted(blk_end, jnp.minimum(blk, total_blocks - 1), side="right").astype(jnp.int32)
    blk_cls = jnp.minimum(blk_cls, N_CLASSES - 1)
    nvalid = jnp.clip(cnt[blk_cls] - (blk - blk_start[blk_cls]) * bm, 0, bm)
    nvalid = jnp.where(blk < total_blocks, nvalid, 0).astype(jnp.int32)
    e_lo = jnp.asarray(CLASS_E_LO)[blk_cls]
    e_hi = jnp.asarray(CLASS_E_HI)[blk_cls]

    w_lo = lambda shape: pl.BlockSpec((1,) + shape, lambda i, tok, elo, ehi, nv: (elo[i], 0, 0))
    w_hi = lambda shape: pl.BlockSpec((1,) + shape, lambda i, tok, elo, ehi, nv: (ehi[i], 0, 0))
    vec = pl.BlockSpec((1, d), lambda i, tok, elo, ehi, nv: (0, 0))
    grid_spec = pltpu.PrefetchScalarGridSpec(
        num_scalar_prefetch=4,
        grid=(n_blocks,),
        in_specs=[pl.BlockSpec(memory_space=pl.ANY),
                  w_lo((d, f)), w_lo((d, f)), w_lo((f, d)),
                  w_hi((d, f)), w_hi((d, f)), w_hi((f, d)),
                  vec, vec],
        out_specs=pl.BlockSpec(memory_space=pl.ANY),
        scratch_shapes=[pltpu.VMEM((2, bm, ROW_W), jnp.float32),
                        pltpu.VMEM((2, bm, d), jnp.float32),
                        pltpu.SemaphoreType.DMA((2,)),
                        pltpu.SemaphoreType.DMA((2,))],
    )
    return pl.pallas_call(
        _expert_kernel,
        out_shape=jax.ShapeDtypeStruct((t, d), jnp.float32),
        grid_spec=grid_spec,
        compiler_params=pltpu.CompilerParams(dimension_semantics=("arbitrary",),
                                             vmem_limit_bytes=VMEM_LIMIT_BYTES),
        name="expert_pair_ffn",
    )(tok_of_row, e_lo, e_hi, nvalid, xext, w_gate, w_up, w_down, w_gate, w_up, w_down, ln_g, ln_b)


def kernel(x, conv_w_in, conv_w, conv_w_out, ml_w_in, ml_b_gate, ml_norm_g, ml_w_out, ln_mix_g, ln_mix_b,
           ln_ffn_g, ln_ffn_b, router_w, router_b, exp_w_gate, exp_w_up, exp_w_down):
    bsz, seq, d = x.shape
    assert d == D_MODEL and seq % MIX_TILE == 0 and (bsz * seq) % EXPERT_BLOCK == 0
    bf16 = jnp.bfloat16
    vec = lambda a: a.astype(jnp.float32)[None, :]
    (rw_hi, rw_lo), rb = _prep_router(router_w, router_b)
    x2d = x.reshape(bsz * seq, d).astype(jnp.float32)
    for i in range(DEPTH):
        j = i // 2
        if i % 2 == 0:
            xext, counts = _conv_mixer_layer(
                x2d, seq, conv_w_in[j].astype(bf16), _pad_rows(conv_w[j].astype(jnp.float32)),
                conv_w_out[j].astype(bf16), vec(ln_mix_g[i]), vec(ln_mix_b[i]), rw_hi, rw_lo, rb)
        else:
            xext, counts = _mlstm_mixer_layer(
                x2d, seq, ml_w_in[j], ml_b_gate[j], ml_norm_g[j].astype(jnp.float32), ml_w_out[j],
                vec(ln_mix_g[i]), vec(ln_mix_b[i]), rw_hi, rw_lo, rb)
        x2d = _moe_layer(xext, counts, exp_w_gate[i].astype(bf16), exp_w_up[i].astype(bf16),
                         exp_w_down[i].astype(bf16), vec(ln_ffn_g[i]), vec(ln_ffn_b[i]))
    return x2d.reshape(bsz, seq, d).astype(x.dtype)
```

```python
import functools
import itertools

import numpy as np
import jax
import jax.numpy as jnp
from jax import lax
from jax.experimental import pallas as pl
from jax.experimental.pallas import tpu as pltpu

D_MODEL = 1024
DEPTH = 2
CONV_WIDTH = 3
ML_HEADS = 8
ML_QK_DIM = D_MODEL // (2 * ML_HEADS)
ML_V_DIM = D_MODEL // ML_HEADS
ML_QK_W = ML_HEADS * ML_QK_DIM
ML_V_W = ML_HEADS * ML_V_DIM
N_EXPERTS = 16
N_GROUPS = 4
EXPERTS_PER_GROUP = N_EXPERTS // N_GROUPS
D_EXPERT = 3 * D_MODEL // 2
ALPHA = (2 * DEPTH) ** 0.25
LN_EPS = 1e-5
HEAD_NORM_EPS = 1e-6

LANES = 128
SUBLANES = 8
VMEM_LIMIT_BYTES = 56 * 1024 * 1024

PAIRS = tuple(itertools.combinations(range(EXPERTS_PER_GROUP), 2))
N_PAIRS = len(PAIRS)
N_CLASSES = N_GROUPS * N_PAIRS
CLASS_E_LO = np.array([g * EXPERTS_PER_GROUP + a for g in range(N_GROUPS) for a, _ in PAIRS], np.int32)
CLASS_E_HI = np.array([g * EXPERTS_PER_GROUP + b for g in range(N_GROUPS) for _, b in PAIRS], np.int32)

META_G_LO, META_G_HI, META_CLASS, META_RANK = 0, 1, 2, 3
ROW_W = D_MODEL + LANES

MIX_TILE = 512
EXPERT_BLOCK = 256
ML_CHUNK = 128


def _layer_norm(z, g, b):
    mu = jnp.mean(z, axis=-1, keepdims=True)
    zc = z - mu
    var = jnp.mean(zc * zc, axis=-1, keepdims=True)
    return zc * lax.rsqrt(var + LN_EPS) * g + b


def _split_bf16(a):
    hi = a.astype(jnp.bfloat16)
    lo = (a - hi.astype(jnp.float32)).astype(jnp.bfloat16)
    return hi, lo


def _group_member(v, lane, s):
    pos = lane & (EXPERTS_PER_GROUP - 1)
    ahead = pltpu.roll(v, LANES - s, axis=1)
    behind = pltpu.roll(v, EXPERTS_PER_GROUP - s, axis=1)
    return jnp.where(pos + s < EXPERTS_PER_GROUP, ahead, behind)


def _route_and_rank(x1, rw_hi, rw_lo, rb, count_ref):
    ts = x1.shape[0]
    f32 = jnp.float32
    lane = lax.broadcasted_iota(jnp.int32, (ts, LANES), 1)
    is_expert = lane < N_EXPERTS

    x_hi, x_lo = _split_bf16(x1)
    logits = (jnp.dot(x_hi, rw_hi, preferred_element_type=f32)
              + jnp.dot(x_hi, rw_lo, preferred_element_type=f32)
              + jnp.dot(x_lo, rw_hi, preferred_element_type=f32))
    logits = jnp.where(is_expert, logits, -1e30)
    mx = jnp.max(logits, axis=-1, keepdims=True)
    ex = jnp.where(is_expert, jnp.exp(logits - mx), 0.0)
    probs = ex / jnp.sum(ex, axis=-1, keepdims=True)
    sel = probs + rb

    pos = lane & (EXPERTS_PER_GROUP - 1)
    beaten_by = jnp.zeros((ts, LANES), jnp.int32)
    for s in range(1, EXPERTS_PER_GROUP):
        other = _group_member(sel, lane, s)
        other_is_lower = pos + s >= EXPERTS_PER_GROUP
        wins = (other > sel) | ((other == sel) & other_is_lower)
        beaten_by = beaten_by + wins.astype(jnp.int32)
    in_top2 = beaten_by < 2
    top2_val = jnp.where(in_top2, sel, 0.0)
    group_score = top2_val
    for s in range(1, EXPERTS_PER_GROUP):
        group_score = group_score + _group_member(top2_val, lane, s)
    group_score = jnp.where(is_expert, group_score, -1e30)
    best = jnp.max(group_score, axis=-1, keepdims=True)
    group_of_lane = lane >> 2
    g_best = jnp.min(jnp.where((group_score == best) & is_expert, group_of_lane, N_GROUPS),
                     axis=-1, keepdims=True)
    chosen = (group_of_lane == g_best) & in_top2 & is_expert
    e_lo = jnp.min(jnp.where(chosen, lane, LANES), axis=-1, keepdims=True)
    e_hi = jnp.max(jnp.where(chosen, lane, -1), axis=-1, keepdims=True)
    p_lo = jnp.sum(jnp.where(lane == e_lo, probs, 0.0), axis=-1, keepdims=True)
    p_hi = jnp.sum(jnp.where(lane == e_hi, probs, 0.0), axis=-1, keepdims=True)
    g_lo = p_lo / (p_lo + p_hi)
    g_hi = p_hi / (p_lo + p_hi)
    a = e_lo & (EXPERTS_PER_GROUP - 1)
    b = e_hi & (EXPERTS_PER_GROUP - 1)
    pair = ((a * (2 * EXPERTS_PER_GROUP - 1 - a)) >> 1) + (b - a - 1)
    cls = g_best * N_PAIRS + pair

    onehot = jnp.where(lane == cls, 1.0, 0.0)
    row = lax.broadcasted_iota(jnp.int32, (ts, ts), 0)
    col = lax.broadcasted_iota(jnp.int32, (ts, ts), 1)
    earlier = jnp.where(col < row, 1.0, 0.0).astype(jnp.bfloat16)
    before = jnp.dot(earlier, onehot.astype(jnp.bfloat16), preferred_element_type=f32)
    running = count_ref[0:1, :]
    rank = jnp.sum(onehot * (before + running), axis=-1, keepdims=True)
    count_ref[...] = jnp.broadcast_to(running + jnp.sum(onehot, axis=0, keepdims=True), count_ref.shape)

    meta = jnp.where(lane == META_G_LO, g_lo,
           jnp.where(lane == META_G_HI, g_hi,
           jnp.where(lane == META_CLASS, cls.astype(f32),
           jnp.where(lane == META_RANK, rank, 0.0))))
    return meta


def _conv_mixer_kernel(x_ref, w_in_ref, w_conv_ref, w_out_ref, ln_g_ref, ln_b_ref,
                       rw_hi_ref, rw_lo_ref, rb_ref, out_ref, meta_t_ref, count_ref, carry_ref, *, tiles_per_seq):
    i = pl.program_id(0)

    @pl.when(i == 0)
    def _():
        count_ref[...] = jnp.zeros_like(count_ref)

    @pl.when(i % tiles_per_seq == 0)
    def _():
        carry_ref[...] = jnp.zeros_like(carry_ref)

    f32 = jnp.float32
    d = D_MODEL
    x = x_ref[...]
    ts = x.shape[0]
    proj = jnp.dot(x.astype(jnp.bfloat16), w_in_ref[...], preferred_element_type=f32)
    gate_b, gate_c, h = proj[:, :d], proj[:, d:2 * d], proj[:, 2 * d:]
    u = gate_c * h
    row = lax.broadcasted_iota(jnp.int32, (ts, d), 0)
    prev = carry_ref[...]
    u1 = jnp.where(row == 0, prev[SUBLANES - 1:SUBLANES, :], pltpu.roll(u, 1, axis=0))
    u2 = jnp.where(row == 0, prev[SUBLANES - 2:SUBLANES - 1, :],
                   jnp.where(row == 1, prev[SUBLANES - 1:SUBLANES, :], pltpu.roll(u, 2, axis=0)))
    carry_ref[...] = u[ts - SUBLANES:, :]
    wc = w_conv_ref[...]
    conv = wc[0:1, :] * u2 + wc[1:2, :] * u1 + wc[2:3, :] * u
    mix = jnp.dot((gate_b * conv).astype(jnp.bfloat16), w_out_ref[...], preferred_element_type=f32)
    x1 = _layer_norm(ALPHA * x + mix, ln_g_ref[...], ln_b_ref[...])
    meta = _route_and_rank(x1, rw_hi_ref[...], rw_lo_ref[...], rb_ref[...], count_ref)
    out_ref[:, :d] = x1
    out_ref[:, d:] = meta
    meta_t_ref[...] = meta.T[:SUBLANES, :]


def _const_spec(shape):
    return pl.BlockSpec(shape, lambda i: (0,) * len(shape))


def _pad_rows(a, rows=SUBLANES):
    return jnp.pad(a, ((0, rows - a.shape[0]), (0, 0)))


def _prep_router(router_w, router_b):
    rw = jnp.pad(router_w.astype(jnp.float32), ((0, 0), (0, LANES - N_EXPERTS)))
    rb = jnp.pad(router_b.astype(jnp.float32), (0, LANES - N_EXPERTS))[None, :]
    return _split_bf16(rw), rb


def _conv_mixer_layer(x2d, seq, w_in, w_conv, w_out, ln_g, ln_b, rw_hi, rw_lo, rb):
    t, d = x2d.shape
    ts = MIX_TILE
    grid = (t // ts,)
    return pl.pallas_call(
        functools.partial(_conv_mixer_kernel, tiles_per_seq=seq // ts),
        out_shape=(jax.ShapeDtypeStruct((t, ROW_W), jnp.float32),
                   jax.ShapeDtypeStruct((SUBLANES, t), jnp.float32),
                   jax.ShapeDtypeStruct((SUBLANES, LANES), jnp.float32)),
        grid=grid,
        in_specs=[pl.BlockSpec((ts, d), lambda i: (i, 0)),
                  _const_spec((d, 3 * d)), _const_spec((SUBLANES, d)), _const_spec((d, d)),
                  _const_spec((1, d)), _const_spec((1, d)),
                  _const_spec((d, LANES)), _const_spec((d, LANES)), _const_spec((1, LANES))],
        out_specs=(pl.BlockSpec((ts, ROW_W), lambda i: (i, 0)),
                   pl.BlockSpec((SUBLANES, ts), lambda i: (0, i)),
                   _const_spec((SUBLANES, LANES))),
        scratch_shapes=[pltpu.VMEM((SUBLANES, d), jnp.float32)],
        compiler_params=pltpu.CompilerParams(dimension_semantics=("arbitrary",),
                                             vmem_limit_bytes=VMEM_LIMIT_BYTES),
        name="conv_mixer_route",
    )(x2d, w_in, w_conv, w_out, ln_g, ln_b, rw_hi, rw_lo, rb)


def _log_sigmoid(z):
    return jnp.minimum(z, 0.0) - jnp.log1p(jnp.exp(-jnp.abs(z)))


def _mlstm_mixer_kernel(x_ref, w_qkvo_ref, wg_hi_ref, wg_lo_ref, wgt_hi_ref, wgt_lo_ref,
                        bg_col_ref, bg_row_ref, norm_g_ref, w_out_ref, ln_g_ref, ln_b_ref,
                        rw_hi_ref, rw_lo_ref, rb_ref, out_ref, meta_t_ref, count_ref,
                        proj_ref, gcol_ref, grow_ref, h_ref, c_ref, m_ref, *, tiles_per_seq):
    i = pl.program_id(0)
    f32, bf16 = jnp.float32, jnp.bfloat16
    nh, dk, dv, d = ML_HEADS, ML_QK_DIM, ML_V_DIM, D_MODEL
    L = ML_CHUNK
    ts = x_ref.shape[0]
    n_chunks = ts // L

    @pl.when(i == 0)
    def _():
        count_ref[...] = jnp.zeros_like(count_ref)

    @pl.when(i % tiles_per_seq == 0)
    def _():
        c_ref[...] = jnp.zeros_like(c_ref)
        m_ref[...] = jnp.zeros_like(m_ref)

    x = x_ref[...]
    x_hi, x_lo = _split_bf16(x)
    proj_ref[...] = jnp.dot(x_hi, w_qkvo_ref[...], preferred_element_type=f32)

    g_col = (jnp.dot(x_hi, wg_hi_ref[...], preferred_element_type=f32)
             + jnp.dot(x_hi, wg_lo_ref[...], preferred_element_type=f32)
             + jnp.dot(x_lo, wg_hi_ref[...], preferred_element_type=f32)) + bg_col_ref[...]
    nt = (((1,), (1,)), ((), ()))
    g_row = (lax.dot_general(wgt_hi_ref[...], x_hi, nt, preferred_element_type=f32)
             + lax.dot_general(wgt_lo_ref[...], x_hi, nt, preferred_element_type=f32)
             + lax.dot_general(wgt_hi_ref[...], x_lo, nt, preferred_element_type=f32)) + bg_row_ref[...]

    r_i = lax.broadcasted_iota(jnp.int32, (ts, ts), 0)
    c_i = lax.broadcasted_iota(jnp.int32, (ts, ts), 1)
    same_chunk = (r_i // L) == (c_i // L)
    tri_col = jnp.where(same_chunk & (c_i <= r_i), 1.0, 0.0).astype(bf16)
    tri_row = jnp.where(same_chunk & (r_i <= c_i), 1.0, 0.0).astype(bf16)
    lf_col_hi, lf_col_lo = _split_bf16(_log_sigmoid(g_col))
    cum_col = (jnp.dot(tri_col, lf_col_hi, preferred_element_type=f32)
               + jnp.dot(tri_col, lf_col_lo, preferred_element_type=f32))
    lf_row_hi, lf_row_lo = _split_bf16(_log_sigmoid(g_row))
    cum_row = (jnp.dot(lf_row_hi, tri_row, preferred_element_type=f32)
               + jnp.dot(lf_row_lo, tri_row, preferred_element_type=f32))
    lane = lax.broadcasted_iota(jnp.int32, (ts, LANES), 1)
    gcol_ref[...] = jnp.where(lane < nh, g_col, cum_col)
    sub = lax.broadcasted_iota(jnp.int32, (2 * nh, ts), 0)
    g_row_all = jnp.where(sub < nh, g_row, cum_row)
    for c in range(n_chunks):
        grow_ref[c] = g_row_all[:, c * L:(c + 1) * L]

    causal = lax.broadcasted_iota(jnp.int32, (L, L), 0) >= lax.broadcasted_iota(jnp.int32, (L, L), 1)
    ones_col = jnp.where(lax.broadcasted_iota(jnp.int32, (L, dv), 1) == 0, 1.0, 0.0).astype(bf16)
    tn = (((0,), (0,)), ((), ()))

    def chunk_body(c, carry):
        r0 = pl.multiple_of(c * L, L)
        gc = gcol_ref[pl.ds(r0, L), :]
        gr = grow_ref[c]
        for h in range(nh):
            i_col, b_col = gc[:, h:h + 1], gc[:, nh + h:nh + h + 1]
            i_row, b_row = gr[h:h + 1, :], gr[nh + h:nh + h + 1, :]
            m_prev = m_ref[h][0:1, 0:1]
            q = (proj_ref[pl.ds(r0, L), h * dk:(h + 1) * dk] * (dk ** -0.5)).astype(bf16)
            k32 = proj_ref[pl.ds(r0, L), ML_QK_W + h * dk:ML_QK_W + (h + 1) * dk]
            v = proj_ref[pl.ds(r0, L), 2 * ML_QK_W + h * dv:2 * ML_QK_W + (h + 1) * dv].astype(bf16)
            o_pre = proj_ref[pl.ds(r0, L), 2 * ML_QK_W + ML_V_W + h * dv:2 * ML_QK_W + ML_V_W + (h + 1) * dv]
            v_ext = jnp.concatenate([v, ones_col], axis=1)

            dmat = jnp.where(causal, b_col - b_row + i_row, -jnp.inf)
            m_inter = b_col + m_prev
            m_t = jnp.maximum(m_inter, jnp.max(dmat, axis=-1, keepdims=True))
            w_inter = jnp.exp(m_inter - m_t)
            p = jnp.exp(dmat - m_t)
            s = lax.dot_general(q, k32.astype(bf16), nt, preferred_element_type=f32)
            a = (p * s).astype(bf16)
            c_prev = c_ref[h]
            tot = (jnp.dot(a, v_ext, preferred_element_type=f32)
                   + w_inter * jnp.dot(q, c_prev.astype(bf16), preferred_element_type=f32))
            den = tot[:, dv:dv + 1]
            hh = tot[:, :dv] / jnp.maximum(jnp.abs(den), jnp.exp(-m_t))

            m_new = m_t[L - 1:L, :]
            b_last = b_col[L - 1:L, :]
            g_state = jnp.exp(b_last + m_prev - m_new)
            g_r = jnp.exp(b_last - b_col + i_col - m_new)
            kg = (k32 * g_r).astype(bf16)
            c_ref[h] = g_state * c_prev + lax.dot_general(kg, v_ext, tn, preferred_element_type=f32)
            m_ref[h] = jnp.broadcast_to(m_new, m_ref.shape[1:])

            hn = hh * lax.rsqrt(jnp.mean(hh * hh, axis=-1, keepdims=True) + HEAD_NORM_EPS)
            hn = hn * norm_g_ref[:, h * dv:(h + 1) * dv]
            h_ref[pl.ds(r0, L), h * dv:(h + 1) * dv] = jax.nn.sigmoid(o_pre) * hn
        return carry

    lax.fori_loop(0, n_chunks, chunk_body, 0)

    mix = jnp.dot(h_ref[...].astype(bf16), w_out_ref[...], preferred_element_type=f32)
    x1 = _layer_norm(ALPHA * x + mix, ln_g_ref[...], ln_b_ref[...])
    meta = _route_and_rank(x1, rw_hi_ref[...], rw_lo_ref[...], rb_ref[...], count_ref)
    out_ref[:, :d] = x1
    out_ref[:, d:] = meta
    meta_t_ref[...] = meta.T[:SUBLANES, :]


def _mlstm_mixer_layer(x2d, seq, w_in, b_gate, norm_g, w_out, ln_g, ln_b, rw_hi, rw_lo, rb):
    t, d = x2d.shape
    ts = MIX_TILE
    nh = ML_HEADS
    n_qkvo = 2 * ML_QK_W + 2 * ML_V_W
    w_qkvo = w_in[:, :n_qkvo].astype(jnp.bfloat16)
    w_g = w_in[:, n_qkvo:].astype(jnp.float32)
    wg_hi, wg_lo = _split_bf16(jnp.pad(w_g, ((0, 0), (0, LANES - 2 * nh))))
    wgt_hi, wgt_lo = _split_bf16(w_g.T)
    bg = b_gate.astype(jnp.float32)
    bg_col = jnp.pad(bg, (0, LANES - 2 * nh))[None, :]
    bg_row = bg[:, None]
    return pl.pallas_call(
        functools.partial(_mlstm_mixer_kernel, tiles_per_seq=seq // ts),
        out_shape=(jax.ShapeDtypeStruct((t, ROW_W), jnp.float32),
                   jax.ShapeDtypeStruct((SUBLANES, t), jnp.float32),
                   jax.ShapeDtypeStruct((SUBLANES, LANES), jnp.float32)),
        grid=(t // ts,),
        in_specs=[pl.BlockSpec((ts, d), lambda i: (i, 0)),
                  _const_spec((d, n_qkvo)),
                  _const_spec((d, LANES)), _const_spec((d, LANES)),
                  _const_spec((2 * nh, d)), _const_spec((2 * nh, d)),
                  _const_spec((1, LANES)), _const_spec((2 * nh, 1)),
                  _const_spec((1, d)), _const_spec((d, d)),
                  _const_spec((1, d)), _const_spec((1, d)),
                  _const_spec((d, LANES)), _const_spec((d, LANES)), _const_spec((1, LANES))],
        out_specs=(pl.BlockSpec((ts, ROW_W), lambda i: (i, 0)),
                   pl.BlockSpec((SUBLANES, ts), lambda i: (0, i)),
                   _const_spec((SUBLANES, LANES))),
        scratch_shapes=[pltpu.VMEM((ts, n_qkvo), jnp.float32),
                        pltpu.VMEM((ts, LANES), jnp.float32),
                        pltpu.VMEM((ts // ML_CHUNK, 2 * nh, ML_CHUNK), jnp.float32),
                        pltpu.VMEM((ts, d), jnp.float32),
                        pltpu.VMEM((nh, ML_QK_DIM, 2 * ML_V_DIM), jnp.float32),
                        pltpu.VMEM((nh, SUBLANES, LANES), jnp.float32)],
        compiler_params=pltpu.CompilerParams(dimension_semantics=("arbitrary",),
                                             vmem_limit_bytes=VMEM_LIMIT_BYTES),
        name="mlstm_mixer_route",
    )(x2d, w_qkvo, wg_hi, wg_lo, wgt_hi, wgt_lo, bg_col, bg_row, norm_g[None, :], w_out.astype(jnp.bfloat16),
      ln_g, ln_b, rw_hi, rw_lo, rb)


def _expert_kernel(tok_ref, elo_ref, ehi_ref, nvalid_ref,
                   x_hbm, wg_lo_ref, wu_lo_ref, wd_lo_ref, wg_hi_ref, wu_hi_ref, wd_hi_ref,
                   ln_g_ref, ln_b_ref, out_hbm, xbuf0, xbuf1, obuf0, obuf1, gather_sem, scatter_sem):
    i = pl.program_id(0)
    nb = pl.num_programs(0)
    bm = xbuf0.shape[0]
    d = D_MODEL
    xbufs, obufs = (xbuf0, xbuf1), (obuf0, obuf1)
    nv = nvalid_ref[i]
    nv_prev = nvalid_ref[jnp.maximum(i - 1, 0)]

    def start_gather(j, s):
        for r in range(bm):
            tok = tok_ref[j * bm + r]
            pltpu.make_async_copy(x_hbm.at[pl.ds(tok, 1)], xbufs[s].at[pl.ds(r, 1)], gather_sem.at[s]).start()

    def wait_gather(s):
        pltpu.make_async_copy(xbufs[s], xbufs[s], gather_sem.at[s]).wait()

    def scatter_copy(j, s, r):
        tok = tok_ref[j * bm + r]
        return pltpu.make_async_copy(obufs[s].at[pl.ds(r, 1)], out_hbm.at[pl.ds(tok, 1)], scatter_sem.at[s])

    def wait_scatter(s, n):
        @pl.when(n == bm)
        def _():
            pltpu.make_async_copy(obufs[s], obufs[s], scatter_sem.at[s]).wait()

        @pl.when(n < bm)
        def _():
            def body(r, c):
                pltpu.make_async_copy(obufs[s].at[pl.ds(0, 1)], obufs[s].at[pl.ds(0, 1)], scatter_sem.at[s]).wait()
                return c
            lax.fori_loop(0, n, body, 0)

    @pl.when(i == 0)
    def _():
        start_gather(0, 0)

    def step(s):
        @pl.when((i == 0) | (nv_prev > 0))
        def _():
            wait_gather(s)

        @pl.when(nv > 0)
        def _():
            start_gather(i + 1, 1 - s)
            xb = xbufs[s][...]
            x = xb[:, :d]
            g_lo = xb[:, d + META_G_LO:d + META_G_LO + 1]
            g_hi = xb[:, d + META_G_HI:d + META_G_HI + 1]
            x16 = x.astype(jnp.bfloat16)

            def ffn(wg_ref, wu_ref, wd_ref):
                g = jnp.dot(x16, wg_ref[0], preferred_element_type=jnp.float32)
                u = jnp.dot(x16, wu_ref[0], preferred_element_type=jnp.float32)
                h = (g * jax.nn.sigmoid(g)) * u
                return jnp.dot(h.astype(jnp.bfloat16), wd_ref[0], preferred_element_type=jnp.float32)

            y = g_lo * ffn(wg_lo_ref, wu_lo_ref, wd_lo_ref) + g_hi * ffn(wg_hi_ref, wu_hi_ref, wd_hi_ref)
            obufs[s][...] = _layer_norm(ALPHA * x + y, ln_g_ref[...], ln_b_ref[...])

            @pl.when(nv == bm)
            def _():
                for r in range(bm):
                    scatter_copy(i, s, r).start()

            @pl.when(nv < bm)
            def _():
                def body(r, c):
                    scatter_copy(i, s, r).start()
                    return c
                lax.fori_loop(0, nv, body, 0)

        @pl.when(i > 0)
        def _():
            wait_scatter(1 - s, nv_prev)

        @pl.when(i == nb - 1)
        def _():
            wait_scatter(s, nv)

            @pl.when(nv > 0)
            def _():
                wait_gather(1 - s)

    for s in range(2):
        pl.when(i % 2 == s)(functools.partial(step, s))


def _moe_layer(xext, meta_t, counts, w_gate, w_up, w_down, ln_g, ln_b):
    t = xext.shape[0]
    d, f, bm = D_MODEL, D_EXPERT, EXPERT_BLOCK
    n_blocks = t // bm + N_CLASSES
    n_rows = n_blocks * bm

    cls = meta_t[META_CLASS].astype(jnp.int32)
    rank = meta_t[META_RANK].astype(jnp.int32)
    cnt = counts[0, :N_CLASSES].astype(jnp.int32)
    cls_blocks = (cnt + bm - 1) // bm
    blk_end = jnp.cumsum(cls_blocks)
    blk_start = blk_end - cls_blocks
    total_blocks = blk_end[-1]
    row_start = jnp.sum(jnp.where(cls[None, :] == jnp.arange(N_CLASSES, dtype=jnp.int32)[:, None],
                                  blk_start[:, None] * bm, 0), axis=0)
    dest = row_start + rank
    tok_of_row = jnp.zeros((n_rows + bm,), jnp.int32).at[dest].set(
        jnp.arange(t, dtype=jnp.int32), unique_indices=True)
    blk = jnp.arange(n_blocks, dtype=jnp.int32)
    blk_cls = jnp.sum(jnp.minimum(blk, total_blocks - 1)[:, None] >= blk_end[None, :], axis=1).astype(jnp.int32)
    blk_cls = jnp.minimum(blk_cls, N_CLASSES - 1)
    nvalid = jnp.clip(cnt[blk_cls] - (blk - blk_start[blk_cls]) * bm, 0, bm)
    nvalid = jnp.where(blk < total_blocks, nvalid, 0).astype(jnp.int32)
    e_lo = jnp.asarray(CLASS_E_LO)[blk_cls]
    e_hi = jnp.asarray(CLASS_E_HI)[blk_cls]

    w_lo = lambda shape: pl.BlockSpec((1,) + shape, lambda i, tok, elo, ehi, nv: (elo[i], 0, 0))
    w_hi = lambda shape: pl.BlockSpec((1,) + shape, lambda i, tok, elo, ehi, nv: (ehi[i], 0, 0))
    vec = pl.BlockSpec((1, d), lambda i, tok, elo, ehi, nv: (0, 0))
    grid_spec = pltpu.PrefetchScalarGridSpec(
        num_scalar_prefetch=4,
        grid=(n_blocks,),
        in_specs=[pl.BlockSpec(memory_space=pl.ANY),
                  w_lo((d, f)), w_lo((d, f)), w_lo((f, d)),
                  w_hi((d, f)), w_hi((d, f)), w_hi((f, d)),
                  vec, vec],
        out_specs=pl.BlockSpec(memory_space=pl.ANY),
        scratch_shapes=[pltpu.VMEM((bm, ROW_W), jnp.float32), pltpu.VMEM((bm, ROW_W), jnp.float32),
                        pltpu.VMEM((bm, d), jnp.float32), pltpu.VMEM((bm, d), jnp.float32),
                        pltpu.SemaphoreType.DMA((2,)),
                        pltpu.SemaphoreType.DMA((2,))],
    )
    return pl.pallas_call(
        _expert_kernel,
        out_shape=jax.ShapeDtypeStruct((t, d), jnp.float32),
        grid_spec=grid_spec,
        compiler_params=pltpu.CompilerParams(dimension_semantics=("arbitrary",),
                                             vmem_limit_bytes=VMEM_LIMIT_BYTES),
        name="expert_pair_ffn",
    )(tok_of_row, e_lo, e_hi, nvalid, xext, w_gate, w_up, w_down, w_gate, w_up, w_down, ln_g, ln_b)


def kernel(x, conv_w_in, conv_w, conv_w_out, ml_w_in, ml_b_gate, ml_norm_g, ml_w_out, ln_mix_g, ln_mix_b,
           ln_ffn_g, ln_ffn_b, router_w, router_b, exp_w_gate, exp_w_up, exp_w_down):
    bsz, seq, d = x.shape
    assert d == D_MODEL and seq % MIX_TILE == 0 and (bsz * seq) % EXPERT_BLOCK == 0
    bf16 = jnp.bfloat16
    vec = lambda a: a.astype(jnp.float32)[None, :]
    (rw_hi, rw_lo), rb = _prep_router(router_w, router_b)
    x2d = x.reshape(bsz * seq, d).astype(jnp.float32)
    for i in range(DEPTH):
        j = i // 2
        if i % 2 == 0:
            xext, meta_t, counts = _conv_mixer_layer(
                x2d, seq, conv_w_in[j].astype(bf16), _pad_rows(conv_w[j].astype(jnp.float32)),
                conv_w_out[j].astype(bf16), vec(ln_mix_g[i]), vec(ln_mix_b[i]), rw_hi, rw_lo, rb)
        else:
            xext, meta_t, counts = _mlstm_mixer_layer(
                x2d, seq, ml_w_in[j], ml_b_gate[j], ml_norm_g[j].astype(jnp.float32), ml_w_out[j],
                vec(ln_mix_g[i]), vec(ln_mix_b[i]), rw_hi, rw_lo, rb)
        x2d = _moe_layer(xext, meta_t, counts, exp_w_gate[i].astype(bf16), exp_w_up[i].astype(bf16),
                         exp_w_down[i].astype(bf16), vec(ln_ffn_g[i]), vec(ln_ffn_b[i]))
    return x2d.reshape(bsz, seq, d).astype(x.dtype)
```

```python
import functools
import itertools

import numpy as np
import jax
import jax.numpy as jnp
from jax import lax
from jax.experimental import pallas as pl
from jax.experimental.pallas import tpu as pltpu

D_MODEL = 1024
DEPTH = 2
CONV_WIDTH = 3
ML_HEADS = 8
ML_QK_DIM = D_MODEL // (2 * ML_HEADS)
ML_V_DIM = D_MODEL // ML_HEADS
ML_QK_W = ML_HEADS * ML_QK_DIM
ML_V_W = ML_HEADS * ML_V_DIM
N_EXPERTS = 16
N_GROUPS = 4
EXPERTS_PER_GROUP = N_EXPERTS // N_GROUPS
D_EXPERT = 3 * D_MODEL // 2
ALPHA = (2 * DEPTH) ** 0.25
LN_EPS = 1e-5
HEAD_NORM_EPS = 1e-6

LANES = 128
SUBLANES = 8
VMEM_LIMIT_BYTES = 56 * 1024 * 1024

PAIRS = tuple(itertools.combinations(range(EXPERTS_PER_GROUP), 2))
N_PAIRS = len(PAIRS)
N_CLASSES = N_GROUPS * N_PAIRS
CLASS_E_LO = np.array([g * EXPERTS_PER_GROUP + a for g in range(N_GROUPS) for a, _ in PAIRS], np.int32)
CLASS_E_HI = np.array([g * EXPERTS_PER_GROUP + b for g in range(N_GROUPS) for _, b in PAIRS], np.int32)

META_G_LO, META_G_HI, META_CLASS, META_RANK = 0, 1, 2, 3
ROW_W = D_MODEL + LANES

MIX_TILE = 512
EXPERT_BLOCK = 256
ML_CHUNK = 128


def _layer_norm(z, g, b):
    mu = jnp.mean(z, axis=-1, keepdims=True)
    zc = z - mu
    var = jnp.mean(zc * zc, axis=-1, keepdims=True)
    return zc * lax.rsqrt(var + LN_EPS) * g + b


def _split_bf16(a):
    hi = a.astype(jnp.bfloat16)
    lo = (a - hi.astype(jnp.float32)).astype(jnp.bfloat16)
    return hi, lo


def _group_member(v, lane, s):
    pos = lane & (EXPERTS_PER_GROUP - 1)
    ahead = pltpu.roll(v, LANES - s, axis=1)
    behind = pltpu.roll(v, EXPERTS_PER_GROUP - s, axis=1)
    return jnp.where(pos + s < EXPERTS_PER_GROUP, ahead, behind)


def _route_and_rank(x1, rw_hi, rw_lo, rb, count_ref):
    ts = x1.shape[0]
    f32 = jnp.float32
    lane = lax.broadcasted_iota(jnp.int32, (ts, LANES), 1)
    is_expert = lane < N_EXPERTS

    x_hi, x_lo = _split_bf16(x1)
    logits = (jnp.dot(x_hi, rw_hi, preferred_element_type=f32)
              + jnp.dot(x_hi, rw_lo, preferred_element_type=f32)
              + jnp.dot(x_lo, rw_hi, preferred_element_type=f32))
    logits = jnp.where(is_expert, logits, -1e30)
    mx = jnp.max(logits, axis=-1, keepdims=True)
    ex = jnp.where(is_expert, jnp.exp(logits - mx), 0.0)
    probs = ex / jnp.sum(ex, axis=-1, keepdims=True)
    sel = probs + rb

    pos = lane & (EXPERTS_PER_GROUP - 1)
    beaten_by = jnp.zeros((ts, LANES), jnp.int32)
    for s in range(1, EXPERTS_PER_GROUP):
        other = _group_member(sel, lane, s)
        other_is_lower = pos + s >= EXPERTS_PER_GROUP
        wins = (other > sel) | ((other == sel) & other_is_lower)
        beaten_by = beaten_by + wins.astype(jnp.int32)
    in_top2 = beaten_by < 2
    top2_val = jnp.where(in_top2, sel, 0.0)
    group_score = top2_val
    for s in range(1, EXPERTS_PER_GROUP):
        group_score = group_score + _group_member(top2_val, lane, s)
    group_score = jnp.where(is_expert, group_score, -1e30)
    best = jnp.max(group_score, axis=-1, keepdims=True)
    group_of_lane = lane >> 2
    g_best = jnp.min(jnp.where((group_score == best) & is_expert, group_of_lane, N_GROUPS),
                     axis=-1, keepdims=True)
    chosen = (group_of_lane == g_best) & in_top2 & is_expert
    e_lo = jnp.min(jnp.where(chosen, lane, LANES), axis=-1, keepdims=True)
    e_hi = jnp.max(jnp.where(chosen, lane, -1), axis=-1, keepdims=True)
    p_lo = jnp.sum(jnp.where(lane == e_lo, probs, 0.0), axis=-1, keepdims=True)
    p_hi = jnp.sum(jnp.where(lane == e_hi, probs, 0.0), axis=-1, keepdims=True)
    g_lo = p_lo / (p_lo + p_hi)
    g_hi = p_hi / (p_lo + p_hi)
    a = e_lo & (EXPERTS_PER_GROUP - 1)
    b = e_hi & (EXPERTS_PER_GROUP - 1)
    pair = ((a * (2 * EXPERTS_PER_GROUP - 1 - a)) >> 1) + (b - a - 1)
    cls = g_best * N_PAIRS + pair

    onehot = jnp.where(lane == cls, 1.0, 0.0)
    row = lax.broadcasted_iota(jnp.int32, (ts, ts), 0)
    col = lax.broadcasted_iota(jnp.int32, (ts, ts), 1)
    earlier = jnp.where(col < row, 1.0, 0.0).astype(jnp.bfloat16)
    before = jnp.dot(earlier, onehot.astype(jnp.bfloat16), preferred_element_type=f32)
    running = count_ref[0:1, :]
    rank = jnp.sum(onehot * (before + running), axis=-1, keepdims=True)
    count_ref[...] = jnp.broadcast_to(running + jnp.sum(onehot, axis=0, keepdims=True), count_ref.shape)

    meta = jnp.where(lane == META_G_LO, g_lo,
           jnp.where(lane == META_G_HI, g_hi,
           jnp.where(lane == META_CLASS, cls.astype(f32),
           jnp.where(lane == META_RANK, rank, 0.0))))
    return meta


def _conv_mixer_kernel(x_ref, w_in_ref, w_conv_ref, w_out_ref, ln_g_ref, ln_b_ref,
                       rw_hi_ref, rw_lo_ref, rb_ref, out_ref, meta_t_ref, count_ref, carry_ref, *, tiles_per_seq):
    i = pl.program_id(0)

    @pl.when(i == 0)
    def _():
        count_ref[...] = jnp.zeros_like(count_ref)

    @pl.when(i % tiles_per_seq == 0)
    def _():
        carry_ref[...] = jnp.zeros_like(carry_ref)

    f32 = jnp.float32
    d = D_MODEL
    x = x_ref[...]
    ts = x.shape[0]
    proj = jnp.dot(x.astype(jnp.bfloat16), w_in_ref[...], preferred_element_type=f32)
    gate_b, gate_c, h = proj[:, :d], proj[:, d:2 * d], proj[:, 2 * d:]
    u = gate_c * h
    row = lax.broadcasted_iota(jnp.int32, (ts, d), 0)
    prev = carry_ref[...]
    u1 = jnp.where(row == 0, prev[SUBLANES - 1:SUBLANES, :], pltpu.roll(u, 1, axis=0))
    u2 = jnp.where(row == 0, prev[SUBLANES - 2:SUBLANES - 1, :],
                   jnp.where(row == 1, prev[SUBLANES - 1:SUBLANES, :], pltpu.roll(u, 2, axis=0)))
    carry_ref[...] = u[ts - SUBLANES:, :]
    wc = w_conv_ref[...]
    conv = wc[0:1, :] * u2 + wc[1:2, :] * u1 + wc[2:3, :] * u
    mix = jnp.dot((gate_b * conv).astype(jnp.bfloat16), w_out_ref[...], preferred_element_type=f32)
    x1 = _layer_norm(ALPHA * x + mix, ln_g_ref[...], ln_b_ref[...])
    meta = _route_and_rank(x1, rw_hi_ref[...], rw_lo_ref[...], rb_ref[...], count_ref)
    out_ref[:, :d] = x1
    out_ref[:, d:] = meta
    meta_t_ref[...] = meta.T[:SUBLANES, :]


def _const_spec(shape):
    return pl.BlockSpec(shape, lambda i: (0,) * len(shape))


def _pad_rows(a, rows=SUBLANES):
    return jnp.pad(a, ((0, rows - a.shape[0]), (0, 0)))


def _prep_router(router_w, router_b):
    rw = jnp.pad(router_w.astype(jnp.float32), ((0, 0), (0, LANES - N_EXPERTS)))
    rb = jnp.pad(router_b.astype(jnp.float32), (0, LANES - N_EXPERTS))[None, :]
    return _split_bf16(rw), rb


def _conv_mixer_layer(x2d, seq, w_in, w_conv, w_out, ln_g, ln_b, rw_hi, rw_lo, rb):
    t, d = x2d.shape
    ts = MIX_TILE
    grid = (t // ts,)
    return pl.pallas_call(
        functools.partial(_conv_mixer_kernel, tiles_per_seq=seq // ts),
        out_shape=(jax.ShapeDtypeStruct((t, ROW_W), jnp.float32),
                   jax.ShapeDtypeStruct((SUBLANES, t), jnp.float32),
                   jax.ShapeDtypeStruct((SUBLANES, LANES), jnp.float32)),
        grid=grid,
        in_specs=[pl.BlockSpec((ts, d), lambda i: (i, 0)),
                  _const_spec((d, 3 * d)), _const_spec((SUBLANES, d)), _const_spec((d, d)),
                  _const_spec((1, d)), _const_spec((1, d)),
                  _const_spec((d, LANES)), _const_spec((d, LANES)), _const_spec((1, LANES))],
        out_specs=(pl.BlockSpec((ts, ROW_W), lambda i: (i, 0)),
                   pl.BlockSpec((SUBLANES, ts), lambda i: (0, i)),
                   _const_spec((SUBLANES, LANES))),
        scratch_shapes=[pltpu.VMEM((SUBLANES, d), jnp.float32)],
        compiler_params=pltpu.CompilerParams(dimension_semantics=("arbitrary",),
                                             vmem_limit_bytes=VMEM_LIMIT_BYTES),
        name="conv_mixer_route",
    )(x2d, w_in, w_conv, w_out, ln_g, ln_b, rw_hi, rw_lo, rb)


def _log_sigmoid(z):
    return jnp.minimum(z, 0.0) - jnp.log1p(jnp.exp(-jnp.abs(z)))


def _split3_f32(a):
    f32, bf16 = jnp.float32, jnp.bfloat16
    hi = a.astype(bf16).astype(f32)
    r1 = a - hi
    mid = r1.astype(bf16).astype(f32)
    lo = (r1 - mid).astype(bf16).astype(f32)
    return hi, mid, lo


_COL_U, _COL_ONE, _COL_W, _COL_EMT, _COL_END = 0, 24, 48, 64, 80


def _mlstm_mixer_kernel(x_ref, w_qvo_ref, wkt_ref, wgt_hi_ref, wgt_lo_ref, bg_row_ref, norm_g_ref, w_out_ref,
                        ln_g_ref, ln_b_ref, rw_hi_ref, rw_lo_ref, rb_ref, out_ref, meta_t_ref, count_ref,
                        proj_ref, kt_ref, grow_ref, h_ref, c_ref, m_ref, *, tiles_per_seq):
    i = pl.program_id(0)
    f32, bf16 = jnp.float32, jnp.bfloat16
    nh, dk, dv, d = ML_HEADS, ML_QK_DIM, ML_V_DIM, D_MODEL
    L = ML_CHUNK
    ts = x_ref.shape[0]
    n_chunks = ts // L
    o_base = ML_QK_W + ML_V_W

    @pl.when(i == 0)
    def _():
        count_ref[...] = jnp.zeros_like(count_ref)

    @pl.when(i % tiles_per_seq == 0)
    def _():
        c_ref[...] = jnp.zeros_like(c_ref)
        m_ref[...] = jnp.zeros_like(m_ref)

    x = x_ref[...]
    x_hi, x_lo = _split_bf16(x)
    nt = (((1,), (1,)), ((), ()))
    proj_ref[...] = jnp.dot(x_hi, w_qvo_ref[...], preferred_element_type=f32)
    k_t = lax.dot_general(wkt_ref[...], x_hi, nt, preferred_element_type=f32)
    g_row = (lax.dot_general(wgt_hi_ref[...], x_hi, nt, preferred_element_type=f32)
             + lax.dot_general(wgt_lo_ref[...], x_hi, nt, preferred_element_type=f32)
             + lax.dot_general(wgt_hi_ref[...], x_lo, nt, preferred_element_type=f32)) + bg_row_ref[...]
    r_i = lax.broadcasted_iota(jnp.int32, (ts, ts), 0)
    c_i = lax.broadcasted_iota(jnp.int32, (ts, ts), 1)
    tri_row = jnp.where(((r_i // L) == (c_i // L)) & (r_i <= c_i), 1.0, 0.0).astype(bf16)
    lf_hi, lf_lo = _split_bf16(_log_sigmoid(g_row))
    cum_row = (jnp.dot(lf_hi, tri_row, preferred_element_type=f32)
               + jnp.dot(lf_lo, tri_row, preferred_element_type=f32))
    sub = lax.broadcasted_iota(jnp.int32, (2 * nh, ts), 0)
    g_row_all = jnp.where(sub < nh, g_row, cum_row)
    for c in range(n_chunks):
        grow_ref[c] = g_row_all[:, c * L:(c + 1) * L]
        kt_ref[c] = k_t[:, c * L:(c + 1) * L]

    causal = lax.broadcasted_iota(jnp.int32, (L, L), 0) >= lax.broadcasted_iota(jnp.int32, (L, L), 1)
    ones_col = jnp.where(lax.broadcasted_iota(jnp.int32, (L, dv), 1) == 0, 1.0, 0.0).astype(bf16)
    lane_hl = lax.broadcasted_iota(jnp.int32, (nh, L), 1)
    lane_ll = lax.broadcasted_iota(jnp.int32, (L, LANES), 1)
    sel_r = lax.broadcasted_iota(jnp.int32, (LANES, 3 * LANES), 0)
    sel_c = lax.broadcasted_iota(jnp.int32, (LANES, 3 * LANES), 1)
    is_w_row = (sel_r >= _COL_W) & (sel_r < _COL_EMT)
    is_e_row = (sel_r >= _COL_EMT) & (sel_r < _COL_END)
    b_sel = jnp.where((is_w_row & (sel_c < 2 * dv)) | (is_e_row & (sel_c >= 2 * dv)), 1.0, 0.0).astype(bf16)
    sel_den = jnp.where(lax.broadcasted_iota(jnp.int32, (2 * dv, dv), 0) == dv, 1.0, 0.0).astype(bf16)
    mean_mat = jnp.full((dv, dv), 1.0 / dv, bf16)
    neg_rows = jnp.full((_COL_ONE - _COL_U, L), -1.0, f32)
    one_rows = jnp.ones((_COL_W - _COL_ONE, L), f32)
    zero_rows_a = jnp.zeros((LANES - _COL_END, L), f32)
    zero_rows_b = jnp.zeros((LANES - _COL_W, L), f32)

    def chunk_body(c, carry):
        r0 = pl.multiple_of(c * L, L)
        gr = grow_ref[c]
        i_row, b_row = gr[:nh, :], gr[nh:, :]
        m_prev = m_ref[...]

        g = i_row - b_row
        cm = g
        shift = 1
        while shift < L:
            cm = jnp.maximum(cm, jnp.where(lane_hl >= shift, pltpu.roll(cm, shift, axis=1), -jnp.inf))
            shift *= 2
        u = jnp.maximum(m_prev, cm)
        m_t = b_row + u
        m_new = jnp.broadcast_to(m_t[:, L - 1:L], (nh, L))
        b_last = jnp.broadcast_to(b_row[:, L - 1:L], (nh, L))
        w_inter = jnp.exp(m_prev - u)
        emt = jnp.exp(-m_t)
        g_r = jnp.exp(b_last - b_row + i_row - m_new)
        g_state = jnp.exp(b_last + m_prev - m_new)
        m_ref[...] = m_new

        w_hi = w_inter.astype(bf16).astype(f32)
        e_hi = emt.astype(bf16).astype(f32)
        col_rows = jnp.concatenate([*_split3_f32(u), one_rows, w_hi, w_inter - w_hi, e_hi, emt - e_hi,
                                    zero_rows_a], axis=0)
        colmat = col_rows.T.astype(bf16)
        b_exp = jnp.concatenate([neg_rows, *_split3_f32(g), zero_rows_b], axis=0).astype(bf16)

        for p in range(nh // 2):
            c_pair = c_ref[p]
            c_pair16 = c_pair.astype(bf16)
            q2 = proj_ref[pl.ds(r0, L), p * LANES:(p + 1) * LANES] * (dk ** -0.5)
            kt_pair = kt_ref[c, p * LANES:(p + 1) * LANES, :]
            kt_pair16 = kt_pair.astype(bf16)
            for hh in range(2):
                h = 2 * p + hh
                lhs = jnp.where((lane_ll & (nh - 1)) == h, colmat, 0)
                e = jnp.dot(lhs, b_exp, preferred_element_type=f32)
                wb = jnp.dot(lhs, b_sel, preferred_element_type=f32)
                pmat = jnp.exp(jnp.where(causal, e, -jnp.inf))
                q_m = jnp.where((lane_ll >= hh * dk) & (lane_ll < (hh + 1) * dk), q2, 0.0).astype(bf16)
                a = (pmat * jnp.dot(q_m, kt_pair16, preferred_element_type=f32)).astype(bf16)
                v = proj_ref[pl.ds(r0, L), ML_QK_W + h * dv:ML_QK_W + (h + 1) * dv].astype(bf16)
                v_ext = jnp.concatenate([v, ones_col], axis=1)
                tot = (jnp.dot(a, v_ext, preferred_element_type=f32)
                       + wb[:, :2 * dv] * jnp.dot(q_m, c_pair16, preferred_element_type=f32))
                num = tot[:, :dv]
                den_b = jnp.dot(tot.astype(bf16), sel_den, preferred_element_type=f32)
                msq_b = jnp.dot((num * num).astype(bf16), mean_mat, preferred_element_type=f32)
                inv = 1.0 / jnp.maximum(jnp.abs(den_b), wb[:, 2 * dv:])
                hn = (num * inv) * lax.rsqrt(msq_b * inv * inv + HEAD_NORM_EPS)
                o_pre = proj_ref[pl.ds(r0, L), o_base + h * dv:o_base + (h + 1) * dv]
                h_ref[pl.ds(r0, L), h * dv:(h + 1) * dv] = (
                    jax.nn.sigmoid(o_pre) * hn * norm_g_ref[:, h * dv:(h + 1) * dv])

                kg = (kt_pair[hh * dk:(hh + 1) * dk, :] * g_r[h:h + 1, :]).astype(bf16)
                gs = g_state[h:h + 1, :]
                c_ref[p, hh * dk:(hh + 1) * dk, :] = (
                    jnp.concatenate([gs, gs], axis=1) * c_pair[hh * dk:(hh + 1) * dk, :]
                    + jnp.dot(kg, v_ext, preferred_element_type=f32))
        return carry

    lax.fori_loop(0, n_chunks, chunk_body, 0)

    mix = jnp.dot(h_ref[...].astype(bf16), w_out_ref[...], preferred_element_type=f32)
    x1 = _layer_norm(ALPHA * x + mix, ln_g_ref[...], ln_b_ref[...])
    meta = _route_and_rank(x1, rw_hi_ref[...], rw_lo_ref[...], rb_ref[...], count_ref)
    out_ref[:, :d] = x1
    out_ref[:, d:] = meta
    meta_t_ref[...] = meta.T[:SUBLANES, :]


def _mlstm_mixer_layer(x2d, seq, w_in, b_gate, norm_g, w_out, ln_g, ln_b, rw_hi, rw_lo, rb):
    t, d = x2d.shape
    ts = MIX_TILE
    nh = ML_HEADS
    assert ML_CHUNK == LANES and ts % ML_CHUNK == 0 and 2 * ML_QK_DIM == LANES and ML_V_DIM == LANES
    n_qkvo = 2 * ML_QK_W + 2 * ML_V_W
    n_qvo = ML_QK_W + 2 * ML_V_W
    w_qvo = jnp.concatenate([w_in[:, :ML_QK_W], w_in[:, 2 * ML_QK_W:n_qkvo]], axis=1).astype(jnp.bfloat16)
    wkt = w_in[:, ML_QK_W:2 * ML_QK_W].T.astype(jnp.bfloat16)
    w_g = w_in[:, n_qkvo:].astype(jnp.float32)
    wgt_hi, wgt_lo = _split_bf16(w_g.T)
    bg_row = b_gate.astype(jnp.float32)[:, None]
    return pl.pallas_call(
        functools.partial(_mlstm_mixer_kernel, tiles_per_seq=seq // ts),
        out_shape=(jax.ShapeDtypeStruct((t, ROW_W), jnp.float32),
                   jax.ShapeDtypeStruct((SUBLANES, t), jnp.float32),
                   jax.ShapeDtypeStruct((SUBLANES, LANES), jnp.float32)),
        grid=(t // ts,),
        in_specs=[pl.BlockSpec((ts, d), lambda i: (i, 0)),
                  _const_spec((d, n_qvo)), _const_spec((ML_QK_W, d)),
                  _const_spec((2 * nh, d)), _const_spec((2 * nh, d)),
                  _const_spec((2 * nh, 1)),
                  _const_spec((1, d)), _const_spec((d, d)),
                  _const_spec((1, d)), _const_spec((1, d)),
                  _const_spec((d, LANES)), _const_spec((d, LANES)), _const_spec((1, LANES))],
        out_specs=(pl.BlockSpec((ts, ROW_W), lambda i: (i, 0)),
                   pl.BlockSpec((SUBLANES, ts), lambda i: (0, i)),
                   _const_spec((SUBLANES, LANES))),
        scratch_shapes=[pltpu.VMEM((ts, n_qvo), jnp.float32),
                        pltpu.VMEM((ts // ML_CHUNK, ML_QK_W, ML_CHUNK), jnp.float32),
                        pltpu.VMEM((ts // ML_CHUNK, 2 * nh, ML_CHUNK), jnp.float32),
                        pltpu.VMEM((ts, d), jnp.float32),
                        pltpu.VMEM((nh // 2, 2 * ML_QK_DIM, 2 * ML_V_DIM), jnp.float32),
                        pltpu.VMEM((nh, ML_CHUNK), jnp.float32)],
        compiler_params=pltpu.CompilerParams(dimension_semantics=("arbitrary",),
                                             vmem_limit_bytes=VMEM_LIMIT_BYTES),
        name="mlstm_mixer_route",
    )(x2d, w_qvo, wkt, wgt_hi, wgt_lo, bg_row, norm_g[None, :], w_out.astype(jnp.bfloat16),
      ln_g, ln_b, rw_hi, rw_lo, rb)


def _expert_kernel(tok_ref, elo_ref, ehi_ref, nvalid_ref,
                   x_hbm, wg_lo_ref, wu_lo_ref, wd_lo_ref, wg_hi_ref, wu_hi_ref, wd_hi_ref,
                   ln_g_ref, ln_b_ref, out_hbm, xbuf0, xbuf1, obuf0, obuf1, gather_sem, scatter_sem):
    i = pl.program_id(0)
    nb = pl.num_programs(0)
    bm = xbuf0.shape[0]
    d = D_MODEL
    xbufs, obufs = (xbuf0, xbuf1), (obuf0, obuf1)
    nv = nvalid_ref[i]
    nv_prev = nvalid_ref[jnp.maximum(i - 1, 0)]

    def start_gather(j, s):
        for r in range(bm):
            tok = tok_ref[j * bm + r]
            pltpu.make_async_copy(x_hbm.at[pl.ds(tok, 1)], xbufs[s].at[pl.ds(r, 1)], gather_sem.at[s]).start()

    def wait_gather(s):
        pltpu.make_async_copy(xbufs[s], xbufs[s], gather_sem.at[s]).wait()

    def scatter_copy(j, s, r):
        tok = tok_ref[j * bm + r]
        return pltpu.make_async_copy(obufs[s].at[pl.ds(r, 1)], out_hbm.at[pl.ds(tok, 1)], scatter_sem.at[s])

    def wait_scatter(s, n):
        @pl.when(n == bm)
        def _():
            pltpu.make_async_copy(obufs[s], obufs[s], scatter_sem.at[s]).wait()

        @pl.when(n < bm)
        def _():
            def body(r, c):
                pltpu.make_async_copy(obufs[s].at[pl.ds(0, 1)], obufs[s].at[pl.ds(0, 1)], scatter_sem.at[s]).wait()
                return c
            lax.fori_loop(0, n, body, 0)

    @pl.when(i == 0)
    def _():
        start_gather(0, 0)

    def step(s):
        @pl.when((i == 0) | (nv_prev > 0))
        def _():
            wait_gather(s)

        @pl.when(nv > 0)
        def _():
            start_gather(i + 1, 1 - s)
            xb = xbufs[s][...]
            x = xb[:, :d]
            g_lo = xb[:, d + META_G_LO:d + META_G_LO + 1]
            g_hi = xb[:, d + META_G_HI:d + META_G_HI + 1]
            x16 = x.astype(jnp.bfloat16)

            def ffn(wg_ref, wu_ref, wd_ref):
                g = jnp.dot(x16, wg_ref[0, 0], preferred_element_type=jnp.float32)
                u = jnp.dot(x16, wu_ref[0, 0], preferred_element_type=jnp.float32)
                h = (g * jax.nn.sigmoid(g)) * u
                return jnp.dot(h.astype(jnp.bfloat16), wd_ref[0, 0], preferred_element_type=jnp.float32)

            y = g_lo * ffn(wg_lo_ref, wu_lo_ref, wd_lo_ref) + g_hi * ffn(wg_hi_ref, wu_hi_ref, wd_hi_ref)
            obufs[s][...] = _layer_norm(ALPHA * x + y, ln_g_ref[...], ln_b_ref[...])

            @pl.when(nv == bm)
            def _():
                for r in range(bm):
                    scatter_copy(i, s, r).start()

            @pl.when(nv < bm)
            def _():
                def body(r, c):
                    scatter_copy(i, s, r).start()
                    return c
                lax.fori_loop(0, nv, body, 0)

        @pl.when(i > 0)
        def _():
            wait_scatter(1 - s, nv_prev)

        @pl.when(i == nb - 1)
        def _():
            wait_scatter(s, nv)

            @pl.when(nv > 0)
            def _():
                wait_gather(1 - s)

    for s in range(2):
        pl.when(i % 2 == s)(functools.partial(step, s))


def _moe_layer(xext, meta_t, counts, layer, w_gate, w_up, w_down, ln_g, ln_b):
    t = xext.shape[0]
    d, f, bm = D_MODEL, D_EXPERT, EXPERT_BLOCK
    n_blocks = t // bm + N_CLASSES
    n_rows = n_blocks * bm

    cls = meta_t[META_CLASS].astype(jnp.int32)
    rank = meta_t[META_RANK].astype(jnp.int32)
    cnt = counts[0, :N_CLASSES].astype(jnp.int32)
    cls_blocks = (cnt + bm - 1) // bm
    blk_end = jnp.cumsum(cls_blocks)
    blk_start = blk_end - cls_blocks
    total_blocks = blk_end[-1]
    row_start = jnp.sum(jnp.where(cls[None, :] == jnp.arange(N_CLASSES, dtype=jnp.int32)[:, None],
                                  blk_start[:, None] * bm, 0), axis=0)
    dest = row_start + rank
    tok_of_row = jnp.zeros((n_rows + bm,), jnp.int32).at[dest].set(
        jnp.arange(t, dtype=jnp.int32), unique_indices=True)
    blk = jnp.arange(n_blocks, dtype=jnp.int32)
    blk_cls = jnp.sum(jnp.minimum(blk, total_blocks - 1)[:, None] >= blk_end[None, :], axis=1).astype(jnp.int32)
    blk_cls = jnp.minimum(blk_cls, N_CLASSES - 1)
    nvalid = jnp.clip(cnt[blk_cls] - (blk - blk_start[blk_cls]) * bm, 0, bm)
    nvalid = jnp.where(blk < total_blocks, nvalid, 0).astype(jnp.int32)
    e_lo = jnp.asarray(CLASS_E_LO)[blk_cls]
    e_hi = jnp.asarray(CLASS_E_HI)[blk_cls]

    w_lo = lambda shape: pl.BlockSpec((1, 1) + shape, lambda i, tok, elo, ehi, nv: (layer, elo[i], 0, 0))
    w_hi = lambda shape: pl.BlockSpec((1, 1) + shape, lambda i, tok, elo, ehi, nv: (layer, ehi[i], 0, 0))
    vec = pl.BlockSpec((1, d), lambda i, tok, elo, ehi, nv: (0, 0))
    grid_spec = pltpu.PrefetchScalarGridSpec(
        num_scalar_prefetch=4,
        grid=(n_blocks,),
        in_specs=[pl.BlockSpec(memory_space=pl.ANY),
                  w_lo((d, f)), w_lo((d, f)), w_lo((f, d)),
                  w_hi((d, f)), w_hi((d, f)), w_hi((f, d)),
                  vec, vec],
        out_specs=pl.BlockSpec(memory_space=pl.ANY),
        scratch_shapes=[pltpu.VMEM((bm, ROW_W), jnp.float32), pltpu.VMEM((bm, ROW_W), jnp.float32),
                        pltpu.VMEM((bm, d), jnp.float32), pltpu.VMEM((bm, d), jnp.float32),
                        pltpu.SemaphoreType.DMA((2,)),
                        pltpu.SemaphoreType.DMA((2,))],
    )
    return pl.pallas_call(
        _expert_kernel,
        out_shape=jax.ShapeDtypeStruct((t, d), jnp.float32),
        grid_spec=grid_spec,
        compiler_params=pltpu.CompilerParams(dimension_semantics=("arbitrary",),
                                             vmem_limit_bytes=VMEM_LIMIT_BYTES),
        name="expert_pair_ffn",
    )(tok_of_row, e_lo, e_hi, nvalid, xext, w_gate, w_up, w_down, w_gate, w_up, w_down, ln_g, ln_b)


def kernel(x, conv_w_in, conv_w, conv_w_out, ml_w_in, ml_b_gate, ml_norm_g, ml_w_out, ln_mix_g, ln_mix_b,
           ln_ffn_g, ln_ffn_b, router_w, router_b, exp_w_gate, exp_w_up, exp_w_down):
    bsz, seq, d = x.shape
    assert d == D_MODEL and seq % MIX_TILE == 0 and (bsz * seq) % EXPERT_BLOCK == 0
    bf16 = jnp.bfloat16
    vec = lambda a: a.astype(jnp.float32)[None, :]
    (rw_hi, rw_lo), rb = _prep_router(router_w, router_b)
    x2d = x.reshape(bsz * seq, d).astype(jnp.float32)
    w_gate16, w_up16, w_down16 = exp_w_gate.astype(bf16), exp_w_up.astype(bf16), exp_w_down.astype(bf16)
    for i in range(DEPTH):
        j = i // 2
        if i % 2 == 0:
            xext, meta_t, counts = _conv_mixer_layer(
                x2d, seq, conv_w_in[j].astype(bf16), _pad_rows(conv_w[j].astype(jnp.float32)),
                conv_w_out[j].astype(bf16), vec(ln_mix_g[i]), vec(ln_mix_b[i]), rw_hi, rw_lo, rb)
        else:
            xext, meta_t, counts = _mlstm_mixer_layer(
                x2d, seq, ml_w_in[j], ml_b_gate[j], ml_norm_g[j].astype(jnp.float32), ml_w_out[j],
                vec(ln_mix_g[i]), vec(ln_mix_b[i]), rw_hi, rw_lo, rb)
        x2d = _moe_layer(xext, meta_t, counts, i, w_gate16, w_up16, w_down16,
                         vec(ln_ffn_g[i]), vec(ln_ffn_b[i]))
    return x2d.reshape(bsz, seq, d).astype(x.dtype)
```

```python
import functools
import itertools

import numpy as np
import jax
import jax.numpy as jnp
from jax import lax
from jax.experimental import pallas as pl
from jax.experimental.pallas import tpu as pltpu

D_MODEL = 1024
DEPTH = 2
CONV_WIDTH = 3
ML_HEADS = 8
ML_QK_DIM = D_MODEL // (2 * ML_HEADS)
ML_V_DIM = D_MODEL // ML_HEADS
ML_QK_W = ML_HEADS * ML_QK_DIM
ML_V_W = ML_HEADS * ML_V_DIM
N_EXPERTS = 16
N_GROUPS = 4
EXPERTS_PER_GROUP = N_EXPERTS // N_GROUPS
D_EXPERT = 3 * D_MODEL // 2
ALPHA = (2 * DEPTH) ** 0.25
LN_EPS = 1e-5
HEAD_NORM_EPS = 1e-6

LANES = 128
SUBLANES = 8
VMEM_LIMIT_BYTES = 56 * 1024 * 1024

PAIRS = tuple(itertools.combinations(range(EXPERTS_PER_GROUP), 2))
N_PAIRS = len(PAIRS)
N_CLASSES = N_GROUPS * N_PAIRS
CLASS_E_LO = np.array([g * EXPERTS_PER_GROUP + a for g in range(N_GROUPS) for a, _ in PAIRS], np.int32)
CLASS_E_HI = np.array([g * EXPERTS_PER_GROUP + b for g in range(N_GROUPS) for _, b in PAIRS], np.int32)

META_G_LO, META_G_HI, META_CLASS, META_RANK = 0, 1, 2, 3
ROW_W = D_MODEL + LANES

MIX_TILE = 512
EXPERT_BLOCK = 256
ML_CHUNK = 128


def _layer_norm(z, g, b):
    mu = jnp.mean(z, axis=-1, keepdims=True)
    zc = z - mu
    var = jnp.mean(zc * zc, axis=-1, keepdims=True)
    return zc * lax.rsqrt(var + LN_EPS) * g + b


def _split_bf16(a):
    hi = a.astype(jnp.bfloat16)
    lo = (a - hi.astype(jnp.float32)).astype(jnp.bfloat16)
    return hi, lo


def _group_member(v, lane, s):
    pos = lane & (EXPERTS_PER_GROUP - 1)
    ahead = pltpu.roll(v, LANES - s, axis=1)
    behind = pltpu.roll(v, EXPERTS_PER_GROUP - s, axis=1)
    return jnp.where(pos + s < EXPERTS_PER_GROUP, ahead, behind)


def _route_and_rank(x1, rw_hi, rw_lo, rb, count_ref):
    ts = x1.shape[0]
    f32 = jnp.float32
    lane = lax.broadcasted_iota(jnp.int32, (ts, LANES), 1)
    is_expert = lane < N_EXPERTS

    x_hi, x_lo = _split_bf16(x1)
    logits = (jnp.dot(x_hi, rw_hi, preferred_element_type=f32)
              + jnp.dot(x_hi, rw_lo, preferred_element_type=f32)
              + jnp.dot(x_lo, rw_hi, preferred_element_type=f32))
    logits = jnp.where(is_expert, logits, -1e30)
    mx = jnp.max(logits, axis=-1, keepdims=True)
    ex = jnp.where(is_expert, jnp.exp(logits - mx), 0.0)
    probs = ex / jnp.sum(ex, axis=-1, keepdims=True)
    sel = probs + rb

    pos = lane & (EXPERTS_PER_GROUP - 1)
    beaten_by = jnp.zeros((ts, LANES), jnp.int32)
    for s in range(1, EXPERTS_PER_GROUP):
        other = _group_member(sel, lane, s)
        other_is_lower = pos + s >= EXPERTS_PER_GROUP
        wins = (other > sel) | ((other == sel) & other_is_lower)
        beaten_by = beaten_by + wins.astype(jnp.int32)
    in_top2 = beaten_by < 2
    top2_val = jnp.where(in_top2, sel, 0.0)
    group_score = top2_val
    for s in range(1, EXPERTS_PER_GROUP):
        group_score = group_score + _group_member(top2_val, lane, s)
    group_score = jnp.where(is_expert, group_score, -1e30)
    best = jnp.max(group_score, axis=-1, keepdims=True)
    group_of_lane = lane >> 2
    g_best = jnp.min(jnp.where((group_score == best) & is_expert, group_of_lane, N_GROUPS),
                     axis=-1, keepdims=True)
    chosen = (group_of_lane == g_best) & in_top2 & is_expert
    e_lo = jnp.min(jnp.where(chosen, lane, LANES), axis=-1, keepdims=True)
    e_hi = jnp.max(jnp.where(chosen, lane, -1), axis=-1, keepdims=True)
    p_lo = jnp.sum(jnp.where(lane == e_lo, probs, 0.0), axis=-1, keepdims=True)
    p_hi = jnp.sum(jnp.where(lane == e_hi, probs, 0.0), axis=-1, keepdims=True)
    g_lo = p_lo / (p_lo + p_hi)
    g_hi = p_hi / (p_lo + p_hi)
    a = e_lo & (EXPERTS_PER_GROUP - 1)
    b = e_hi & (EXPERTS_PER_GROUP - 1)
    pair = ((a * (2 * EXPERTS_PER_GROUP - 1 - a)) >> 1) + (b - a - 1)
    cls = g_best * N_PAIRS + pair

    onehot = jnp.where(lane == cls, 1.0, 0.0)
    row = lax.broadcasted_iota(jnp.int32, (ts, ts), 0)
    col = lax.broadcasted_iota(jnp.int32, (ts, ts), 1)
    earlier = jnp.where(col < row, 1.0, 0.0).astype(jnp.bfloat16)
    before = jnp.dot(earlier, onehot.astype(jnp.bfloat16), preferred_element_type=f32)
    running = count_ref[0:1, :]
    rank = jnp.sum(onehot * (before + running), axis=-1, keepdims=True)
    count_ref[...] = jnp.broadcast_to(running + jnp.sum(onehot, axis=0, keepdims=True), count_ref.shape)

    meta = jnp.where(lane == META_G_LO, g_lo,
           jnp.where(lane == META_G_HI, g_hi,
           jnp.where(lane == META_CLASS, cls.astype(f32),
           jnp.where(lane == META_RANK, rank, 0.0))))
    return meta


def _cast_slabs(w32_refs, w16_refs):
    for w32_ref, w16_ref in zip(w32_refs, w16_refs):
        w16_ref[...] = w32_ref[0].astype(w16_ref.dtype)


def _cast_slab_specs(layer, n_steps, weights):
    args, in_specs, out_shapes, out_specs = [], [], [], []
    for w in weights:
        n_layers, n_exp, rows, cols = w.shape
        slab, rem = divmod(n_exp * rows, n_steps)
        assert rem == 0 and slab % (2 * SUBLANES) == 0
        args.append(w.reshape(n_layers, n_exp * rows, cols))
        in_specs.append(pl.BlockSpec((1, slab, cols), lambda i: (layer, i, 0)))
        out_shapes.append(jax.ShapeDtypeStruct((n_exp * rows, cols), jnp.bfloat16))
        out_specs.append(pl.BlockSpec((slab, cols), lambda i: (i, 0)))
    return args, in_specs, out_shapes, out_specs


def _conv_mixer_kernel(x_ref, w_in_ref, w_conv_ref, w_out_ref, ln_g_ref, ln_b_ref,
                       rw_hi_ref, rw_lo_ref, rb_ref, wg32_ref, wu32_ref, wd32_ref,
                       out_ref, meta_t_ref, count_ref, wg16_ref, wu16_ref, wd16_ref, carry_ref, *, tiles_per_seq):
    i = pl.program_id(0)

    @pl.when(i == 0)
    def _():
        count_ref[...] = jnp.zeros_like(count_ref)

    @pl.when(i % tiles_per_seq == 0)
    def _():
        carry_ref[...] = jnp.zeros_like(carry_ref)

    f32 = jnp.float32
    d = D_MODEL
    x = x_ref[...]
    ts = x.shape[0]
    proj = jnp.dot(x.astype(jnp.bfloat16), w_in_ref[...], preferred_element_type=f32)
    gate_b, gate_c, h = proj[:, :d], proj[:, d:2 * d], proj[:, 2 * d:]
    u = gate_c * h
    row = lax.broadcasted_iota(jnp.int32, (ts, d), 0)
    prev = carry_ref[...]
    u1 = jnp.where(row == 0, prev[SUBLANES - 1:SUBLANES, :], pltpu.roll(u, 1, axis=0))
    u2 = jnp.where(row == 0, prev[SUBLANES - 2:SUBLANES - 1, :],
                   jnp.where(row == 1, prev[SUBLANES - 1:SUBLANES, :], pltpu.roll(u, 2, axis=0)))
    carry_ref[...] = u[ts - SUBLANES:, :]
    wc = w_conv_ref[...]
    conv = wc[0:1, :] * u2 + wc[1:2, :] * u1 + wc[2:3, :] * u
    mix = jnp.dot((gate_b * conv).astype(jnp.bfloat16), w_out_ref[...], preferred_element_type=f32)
    x1 = _layer_norm(ALPHA * x + mix, ln_g_ref[...], ln_b_ref[...])
    meta = _route_and_rank(x1, rw_hi_ref[...], rw_lo_ref[...], rb_ref[...], count_ref)
    out_ref[:, :d] = x1
    out_ref[:, d:] = meta
    meta_t_ref[...] = meta.T[:SUBLANES, :]
    _cast_slabs((wg32_ref, wu32_ref, wd32_ref), (wg16_ref, wu16_ref, wd16_ref))


def _const_spec(shape):
    return pl.BlockSpec(shape, lambda i: (0,) * len(shape))


def _pad_rows(a, rows=SUBLANES):
    return jnp.pad(a, ((0, rows - a.shape[0]), (0, 0)))


def _prep_router(router_w, router_b):
    rw = jnp.pad(router_w.astype(jnp.float32), ((0, 0), (0, LANES - N_EXPERTS)))
    rb = jnp.pad(router_b.astype(jnp.float32), (0, LANES - N_EXPERTS))[None, :]
    return _split_bf16(rw), rb


def _conv_mixer_layer(x2d, seq, w_in, w_conv, w_out, ln_g, ln_b, rw_hi, rw_lo, rb, layer, expert_weights):
    t, d = x2d.shape
    ts = MIX_TILE
    grid = (t // ts,)
    cast_args, cast_in, cast_shapes, cast_out = _cast_slab_specs(layer, grid[0], expert_weights)
    return pl.pallas_call(
        functools.partial(_conv_mixer_kernel, tiles_per_seq=seq // ts),
        out_shape=(jax.ShapeDtypeStruct((t, ROW_W), jnp.float32),
                   jax.ShapeDtypeStruct((SUBLANES, t), jnp.float32),
                   jax.ShapeDtypeStruct((SUBLANES, LANES), jnp.float32), *cast_shapes),
        grid=grid,
        in_specs=[pl.BlockSpec((ts, d), lambda i: (i, 0)),
                  _const_spec((d, 3 * d)), _const_spec((SUBLANES, d)), _const_spec((d, d)),
                  _const_spec((1, d)), _const_spec((1, d)),
                  _const_spec((d, LANES)), _const_spec((d, LANES)), _const_spec((1, LANES)), *cast_in],
        out_specs=(pl.BlockSpec((ts, ROW_W), lambda i: (i, 0)),
                   pl.BlockSpec((SUBLANES, ts), lambda i: (0, i)),
                   _const_spec((SUBLANES, LANES)), *cast_out),
        scratch_shapes=[pltpu.VMEM((SUBLANES, d), jnp.float32)],
        compiler_params=pltpu.CompilerParams(dimension_semantics=("arbitrary",),
                                             vmem_limit_bytes=VMEM_LIMIT_BYTES),
        name="conv_mixer_route",
    )(x2d, w_in, w_conv, w_out, ln_g, ln_b, rw_hi, rw_lo, rb, *cast_args)


def _log_sigmoid(z):
    return jnp.minimum(z, 0.0) - jnp.log1p(jnp.exp(-jnp.abs(z)))


def _split3_f32(a):
    f32, bf16 = jnp.float32, jnp.bfloat16
    hi = a.astype(bf16).astype(f32)
    r1 = a - hi
    mid = r1.astype(bf16).astype(f32)
    lo = (r1 - mid).astype(bf16).astype(f32)
    return hi, mid, lo


_COL_U, _COL_ONE, _COL_W, _COL_EMT, _COL_END = 0, 24, 48, 64, 80


def _mlstm_mixer_kernel(x_ref, w_qvo_ref, wkt_ref, wgt_hi_ref, wgt_lo_ref, bg_row_ref, norm_g_ref, w_out_ref,
                        ln_g_ref, ln_b_ref, rw_hi_ref, rw_lo_ref, rb_ref, wg32_ref, wu32_ref, wd32_ref,
                        out_ref, meta_t_ref, count_ref, wg16_ref, wu16_ref, wd16_ref,
                        proj_ref, kt_ref, grow_ref, colmat_ref, bexp_ref, h_ref, c_ref, m_ref, *, tiles_per_seq):
    i = pl.program_id(0)
    f32, bf16 = jnp.float32, jnp.bfloat16
    nh, dk, dv, d = ML_HEADS, ML_QK_DIM, ML_V_DIM, D_MODEL
    L = ML_CHUNK
    ts = x_ref.shape[0]
    n_chunks = ts // L
    o_base = ML_QK_W + ML_V_W

    @pl.when(i == 0)
    def _():
        count_ref[...] = jnp.zeros_like(count_ref)

    @pl.when(i % tiles_per_seq == 0)
    def _():
        c_ref[...] = jnp.zeros_like(c_ref)
        m_ref[...] = jnp.zeros_like(m_ref)

    x = x_ref[...]
    x_hi, x_lo = _split_bf16(x)
    nt = (((1,), (1,)), ((), ()))
    proj_ref[...] = jnp.dot(x_hi, w_qvo_ref[...], preferred_element_type=f32)
    k_t = lax.dot_general(wkt_ref[...], x_hi, nt, preferred_element_type=f32)
    g_row = (lax.dot_general(wgt_hi_ref[...], x_hi, nt, preferred_element_type=f32)
             + lax.dot_general(wgt_lo_ref[...], x_hi, nt, preferred_element_type=f32)
             + lax.dot_general(wgt_hi_ref[...], x_lo, nt, preferred_element_type=f32)) + bg_row_ref[...]
    r_i = lax.broadcasted_iota(jnp.int32, (ts, ts), 0)
    c_i = lax.broadcasted_iota(jnp.int32, (ts, ts), 1)
    tri_row = jnp.where(((r_i // L) == (c_i // L)) & (r_i <= c_i), 1.0, 0.0).astype(bf16)
    lf_hi, lf_lo = _split_bf16(_log_sigmoid(g_row))
    cum_row = (jnp.dot(lf_hi, tri_row, preferred_element_type=f32)
               + jnp.dot(lf_lo, tri_row, preferred_element_type=f32))
    i_row, b_row = g_row[:nh, :], cum_row[nh:, :]

    g = i_row - b_row
    lane_in_chunk = lax.broadcasted_iota(jnp.int32, (nh, ts), 1) & (L - 1)
    cm = g
    shift = 1
    while shift < L:
        cm = jnp.maximum(cm, jnp.where(lane_in_chunk >= shift, pltpu.roll(cm, shift, axis=1), -jnp.inf))
        shift *= 2
    m_prev = m_ref[:, 0:1]
    u_parts, w_parts, gr_parts, gs_parts = [], [], [], []
    for c in range(n_chunks):
        sl = slice(c * L, (c + 1) * L)
        u_c = jnp.maximum(m_prev, cm[:, sl])
        b_last = b_row[:, (c + 1) * L - 1:(c + 1) * L]
        m_new = b_last + u_c[:, L - 1:L]
        u_parts.append(u_c)
        w_parts.append(jnp.exp(m_prev - u_c))
        gr_parts.append(jnp.exp(b_last - b_row[:, sl] + i_row[:, sl] - m_new))
        gs_parts.append(jnp.broadcast_to(jnp.exp(b_last + m_prev - m_new), (nh, L)))
        m_prev = m_new
    m_ref[...] = jnp.broadcast_to(m_prev, m_ref.shape)
    u = jnp.concatenate(u_parts, axis=1)
    w_inter = jnp.concatenate(w_parts, axis=1)
    emt = jnp.exp(-(b_row + u))
    w_hi = w_inter.astype(bf16).astype(f32)
    e_hi = emt.astype(bf16).astype(f32)
    col_rows = jnp.concatenate(
        [*_split3_f32(u), jnp.ones((_COL_W - _COL_ONE, ts), f32), w_hi, w_inter - w_hi, e_hi, emt - e_hi,
         jnp.zeros((LANES - _COL_END, ts), f32)], axis=0)
    colmat_ref[...] = col_rows.T.astype(bf16)
    b_exp = jnp.concatenate([jnp.full((_COL_ONE - _COL_U, ts), -1.0, f32), *_split3_f32(g),
                             jnp.zeros((LANES - _COL_W, ts), f32)], axis=0).astype(bf16)
    for c in range(n_chunks):
        sl = slice(c * L, (c + 1) * L)
        bexp_ref[c] = b_exp[:, sl]
        grow_ref[c] = jnp.concatenate([gr_parts[c], gs_parts[c]], axis=0)
        kt_ref[c] = k_t[:, sl]

    causal = lax.broadcasted_iota(jnp.int32, (L, L), 0) >= lax.broadcasted_iota(jnp.int32, (L, L), 1)
    ones_col = jnp.where(lax.broadcasted_iota(jnp.int32, (L, dv), 1) == 0, 1.0, 0.0).astype(bf16)
    lane_ll = lax.broadcasted_iota(jnp.int32, (L, LANES), 1)
    sel_r = lax.broadcasted_iota(jnp.int32, (LANES, 2 * LANES), 0)
    sel_c = lax.broadcasted_iota(jnp.int32, (LANES, 2 * LANES), 1)
    is_w_row = (sel_r >= _COL_W) & (sel_r < _COL_EMT)
    is_e_row = (sel_r >= _COL_EMT) & (sel_r < _COL_END)
    b_sel = jnp.where((is_w_row & (sel_c < LANES)) | (is_e_row & (sel_c >= LANES)), 1.0, 0.0).astype(bf16)
    r2_r = lax.broadcasted_iota(jnp.int32, (2 * dv, 2 * dv), 0)
    r2_c = lax.broadcasted_iota(jnp.int32, (2 * dv, 2 * dv), 1)
    rhs2 = jnp.where((r2_r < dv) & (r2_c < dv), 1.0 / dv,
                     jnp.where((r2_r == dv) & (r2_c >= dv), 1.0, 0.0)).astype(bf16)

    def chunk_body(c, carry):
        r0 = pl.multiple_of(c * L, L)
        gr = grow_ref[c]
        colmat = colmat_ref[pl.ds(r0, L), :]
        rhs1 = jnp.concatenate([bexp_ref[c], b_sel], axis=1)
        heads = range(nh)
        c_pairs = [c_ref[p] for p in range(nh // 2)]
        kt_pairs = [kt_ref[c, p * LANES:(p + 1) * LANES, :] for p in range(nh // 2)]

        ew, q_m, s_mat, v_ext = [], [], [], []
        for h in heads:
            p, hh = divmod(h, 2)
            lhs = jnp.where((lane_ll & (nh - 1)) == h, colmat, 0)
            ew.append(jnp.dot(lhs, rhs1, preferred_element_type=f32))
            q2 = proj_ref[pl.ds(r0, L), p * LANES:(p + 1) * LANES] * (dk ** -0.5)
            q_m.append(jnp.where((lane_ll >= hh * dk) & (lane_ll < (hh + 1) * dk), q2, 0.0))
            s_mat.append(jnp.dot(q_m[h].astype(bf16), kt_pairs[p].astype(bf16), preferred_element_type=f32))
            v = proj_ref[pl.ds(r0, L), ML_QK_W + h * dv:ML_QK_W + (h + 1) * dv].astype(bf16)
            v_ext.append(jnp.concatenate([v, ones_col], axis=1))
        tot = []
        for h in heads:
            p, hh = divmod(h, 2)
            a = jnp.exp(jnp.where(causal, ew[h][:, :L], -jnp.inf)) * s_mat[h]
            qw = q_m[h] * ew[h][:, L:L + LANES]
            lhs = jnp.concatenate([a.astype(bf16), qw.astype(bf16)], axis=1)
            rhs = jnp.concatenate([v_ext[h], c_pairs[p].astype(bf16)], axis=0)
            tot.append(jnp.dot(lhs, rhs, preferred_element_type=f32))
        for h in heads:
            num = tot[h][:, :dv]
            lhs2 = jnp.concatenate([(num * num).astype(bf16), tot[h][:, dv:].astype(bf16)], axis=1)
            r2 = jnp.dot(lhs2, rhs2, preferred_element_type=f32)
            inv = 1.0 / jnp.maximum(jnp.abs(r2[:, dv:]), ew[h][:, L + LANES:])
            hn = (num * inv) * lax.rsqrt(r2[:, :dv] * inv * inv + HEAD_NORM_EPS)
            o_pre = proj_ref[pl.ds(r0, L), o_base + h * dv:o_base + (h + 1) * dv]
            h_ref[pl.ds(r0, L), h * dv:(h + 1) * dv] = (
                jax.nn.sigmoid(o_pre) * hn * norm_g_ref[:, h * dv:(h + 1) * dv])
        for h in heads:
            p, hh = divmod(h, 2)
            rows = slice(hh * dk, (hh + 1) * dk)
            kg = (kt_pairs[p][rows, :] * gr[h:h + 1, :]).astype(bf16)
            gs = gr[nh + h:nh + h + 1, :]
            c_ref[p, rows, :] = (jnp.concatenate([gs, gs], axis=1) * c_pairs[p][rows, :]
                                 + jnp.dot(kg, v_ext[h], preferred_element_type=f32))
        return carry

    lax.fori_loop(0, n_chunks, chunk_body, 0)

    mix = jnp.dot(h_ref[...].astype(bf16), w_out_ref[...], preferred_element_type=f32)
    x1 = _layer_norm(ALPHA * x + mix, ln_g_ref[...], ln_b_ref[...])
    meta = _route_and_rank(x1, rw_hi_ref[...], rw_lo_ref[...], rb_ref[...], count_ref)
    out_ref[:, :d] = x1
    out_ref[:, d:] = meta
    meta_t_ref[...] = meta.T[:SUBLANES, :]
    _cast_slabs((wg32_ref, wu32_ref, wd32_ref), (wg16_ref, wu16_ref, wd16_ref))


def _mlstm_mixer_layer(x2d, seq, w_in, b_gate, norm_g, w_out, ln_g, ln_b, rw_hi, rw_lo, rb, layer, expert_weights):
    t, d = x2d.shape
    ts = MIX_TILE
    nh = ML_HEADS
    cast_args, cast_in, cast_shapes, cast_out = _cast_slab_specs(layer, t // ts, expert_weights)
    assert ML_CHUNK == LANES and ts % ML_CHUNK == 0 and 2 * ML_QK_DIM == LANES and ML_V_DIM == LANES
    n_qkvo = 2 * ML_QK_W + 2 * ML_V_W
    n_qvo = ML_QK_W + 2 * ML_V_W
    w_qvo = jnp.concatenate([w_in[:, :ML_QK_W], w_in[:, 2 * ML_QK_W:n_qkvo]], axis=1).astype(jnp.bfloat16)
    wkt = w_in[:, ML_QK_W:2 * ML_QK_W].T.astype(jnp.bfloat16)
    w_g = w_in[:, n_qkvo:].astype(jnp.float32)
    wgt_hi, wgt_lo = _split_bf16(w_g.T)
    bg_row = b_gate.astype(jnp.float32)[:, None]
    return pl.pallas_call(
        functools.partial(_mlstm_mixer_kernel, tiles_per_seq=seq // ts),
        out_shape=(jax.ShapeDtypeStruct((t, ROW_W), jnp.float32),
                   jax.ShapeDtypeStruct((SUBLANES, t), jnp.float32),
                   jax.ShapeDtypeStruct((SUBLANES, LANES), jnp.float32), *cast_shapes),
        grid=(t // ts,),
        in_specs=[pl.BlockSpec((ts, d), lambda i: (i, 0)),
                  _const_spec((d, n_qvo)), _const_spec((ML_QK_W, d)),
                  _const_spec((2 * nh, d)), _const_spec((2 * nh, d)),
                  _const_spec((2 * nh, 1)),
                  _const_spec((1, d)), _const_spec((d, d)),
                  _const_spec((1, d)), _const_spec((1, d)),
                  _const_spec((d, LANES)), _const_spec((d, LANES)), _const_spec((1, LANES)), *cast_in],
        out_specs=(pl.BlockSpec((ts, ROW_W), lambda i: (i, 0)),
                   pl.BlockSpec((SUBLANES, ts), lambda i: (0, i)),
                   _const_spec((SUBLANES, LANES)), *cast_out),
        scratch_shapes=[pltpu.VMEM((ts, n_qvo), jnp.float32),
                        pltpu.VMEM((ts // ML_CHUNK, ML_QK_W, ML_CHUNK), jnp.float32),
                        pltpu.VMEM((ts // ML_CHUNK, 2 * nh, ML_CHUNK), jnp.float32),
                        pltpu.VMEM((ts, LANES), jnp.bfloat16),
                        pltpu.VMEM((ts // ML_CHUNK, LANES, ML_CHUNK), jnp.bfloat16),
                        pltpu.VMEM((ts, d), jnp.float32),
                        pltpu.VMEM((nh // 2, 2 * ML_QK_DIM, 2 * ML_V_DIM), jnp.float32),
                        pltpu.VMEM((nh, ML_CHUNK), jnp.float32)],
        compiler_params=pltpu.CompilerParams(dimension_semantics=("arbitrary",),
                                             vmem_limit_bytes=VMEM_LIMIT_BYTES),
        name="mlstm_mixer_route",
    )(x2d, w_qvo, wkt, wgt_hi, wgt_lo, bg_row, norm_g[None, :], w_out.astype(jnp.bfloat16),
      ln_g, ln_b, rw_hi, rw_lo, rb, *cast_args)


def _expert_kernel(tok_ref, elo_ref, ehi_ref, nvalid_ref,
                   x_hbm, wg_lo_ref, wu_lo_ref, wd_lo_ref, wg_hi_ref, wu_hi_ref, wd_hi_ref,
                   ln_g_ref, ln_b_ref, out_hbm, xbuf0, xbuf1, obuf0, obuf1, gather_sem, scatter_sem):
    i = pl.program_id(0)
    nb = pl.num_programs(0)
    bm = xbuf0.shape[0]
    d = D_MODEL
    xbufs, obufs = (xbuf0, xbuf1), (obuf0, obuf1)
    nv = nvalid_ref[i]
    nv_prev = nvalid_ref[jnp.maximum(i - 1, 0)]

    def start_gather(j, s):
        for r in range(bm):
            tok = tok_ref[j * bm + r]
            pltpu.make_async_copy(x_hbm.at[pl.ds(tok, 1)], xbufs[s].at[pl.ds(r, 1)], gather_sem.at[s]).start()

    def wait_gather(s):
        pltpu.make_async_copy(xbufs[s], xbufs[s], gather_sem.at[s]).wait()

    def scatter_copy(j, s, r):
        tok = tok_ref[j * bm + r]
        return pltpu.make_async_copy(obufs[s].at[pl.ds(r, 1)], out_hbm.at[pl.ds(tok, 1)], scatter_sem.at[s])

    def wait_scatter(s, n):
        @pl.when(n == bm)
        def _():
            pltpu.make_async_copy(obufs[s], obufs[s], scatter_sem.at[s]).wait()

        @pl.when(n < bm)
        def _():
            def body(r, c):
                pltpu.make_async_copy(obufs[s].at[pl.ds(0, 1)], obufs[s].at[pl.ds(0, 1)], scatter_sem.at[s]).wait()
                return c
            lax.fori_loop(0, n, body, 0)

    @pl.when(i == 0)
    def _():
        start_gather(0, 0)

    def step(s):
        @pl.when((i == 0) | (nv_prev > 0))
        def _():
            wait_gather(s)

        @pl.when(nv > 0)
        def _():
            start_gather(i + 1, 1 - s)
            xb = xbufs[s][...]
            x = xb[:, :d]
            g_lo = xb[:, d + META_G_LO:d + META_G_LO + 1]
            g_hi = xb[:, d + META_G_HI:d + META_G_HI + 1]
            x16 = x.astype(jnp.bfloat16)

            def ffn(wg_ref, wu_ref, wd_ref):
                g = jnp.dot(x16, wg_ref[0], preferred_element_type=jnp.float32)
                u = jnp.dot(x16, wu_ref[0], preferred_element_type=jnp.float32)
                h = (g * jax.nn.sigmoid(g)) * u
                return jnp.dot(h.astype(jnp.bfloat16), wd_ref[0], preferred_element_type=jnp.float32)

            y = g_lo * ffn(wg_lo_ref, wu_lo_ref, wd_lo_ref) + g_hi * ffn(wg_hi_ref, wu_hi_ref, wd_hi_ref)
            obufs[s][...] = _layer_norm(ALPHA * x + y, ln_g_ref[...], ln_b_ref[...])

            @pl.when(nv == bm)
            def _():
                for r in range(bm):
                    scatter_copy(i, s, r).start()

            @pl.when(nv < bm)
            def _():
                def body(r, c):
                    scatter_copy(i, s, r).start()
                    return c
                lax.fori_loop(0, nv, body, 0)

        @pl.when(i > 0)
        def _():
            wait_scatter(1 - s, nv_prev)

        @pl.when(i == nb - 1)
        def _():
            wait_scatter(s, nv)

            @pl.when(nv > 0)
            def _():
                wait_gather(1 - s)

    for s in range(2):
        pl.when(i % 2 == s)(functools.partial(step, s))


def _moe_layer(xext, meta_t, counts, w_gate, w_up, w_down, ln_g, ln_b):
    t = xext.shape[0]
    d, f, bm = D_MODEL, D_EXPERT, EXPERT_BLOCK
    n_blocks = t // bm + N_CLASSES
    n_rows = n_blocks * bm

    cls = meta_t[META_CLASS].astype(jnp.int32)
    rank = meta_t[META_RANK].astype(jnp.int32)
    cnt = counts[0, :N_CLASSES].astype(jnp.int32)
    cls_blocks = (cnt + bm - 1) // bm
    blk_end = jnp.cumsum(cls_blocks)
    blk_start = blk_end - cls_blocks
    total_blocks = blk_end[-1]
    row_start = jnp.sum(jnp.where(cls[None, :] == jnp.arange(N_CLASSES, dtype=jnp.int32)[:, None],
                                  blk_start[:, None] * bm, 0), axis=0)
    dest = row_start + rank
    tok_of_row = jnp.zeros((n_rows + bm,), jnp.int32).at[dest].set(
        jnp.arange(t, dtype=jnp.int32), unique_indices=True)
    blk = jnp.arange(n_blocks, dtype=jnp.int32)
    blk_cls = jnp.sum(jnp.minimum(blk, total_blocks - 1)[:, None] >= blk_end[None, :], axis=1).astype(jnp.int32)
    blk_cls = jnp.minimum(blk_cls, N_CLASSES - 1)
    nvalid = jnp.clip(cnt[blk_cls] - (blk - blk_start[blk_cls]) * bm, 0, bm)
    nvalid = jnp.where(blk < total_blocks, nvalid, 0).astype(jnp.int32)
    e_lo = jnp.asarray(CLASS_E_LO)[blk_cls]
    e_hi = jnp.asarray(CLASS_E_HI)[blk_cls]

    w_gate, w_up, w_down = (w_gate.reshape(N_EXPERTS, d, f), w_up.reshape(N_EXPERTS, d, f),
                            w_down.reshape(N_EXPERTS, f, d))
    w_lo = lambda shape: pl.BlockSpec((1,) + shape, lambda i, tok, elo, ehi, nv: (elo[i], 0, 0))
    w_hi = lambda shape: pl.BlockSpec((1,) + shape, lambda i, tok, elo, ehi, nv: (ehi[i], 0, 0))
    vec = pl.BlockSpec((1, d), lambda i, tok, elo, ehi, nv: (0, 0))
    grid_spec = pltpu.PrefetchScalarGridSpec(
        num_scalar_prefetch=4,
        grid=(n_blocks,),
        in_specs=[pl.BlockSpec(memory_space=pl.ANY),
                  w_lo((d, f)), w_lo((d, f)), w_lo((f, d)),
                  w_hi((d, f)), w_hi((d, f)), w_hi((f, d)),
                  vec, vec],
        out_specs=pl.BlockSpec(memory_space=pl.ANY),
        scratch_shapes=[pltpu.VMEM((bm, ROW_W), jnp.float32), pltpu.VMEM((bm, ROW_W), jnp.float32),
                        pltpu.VMEM((bm, d), jnp.float32), pltpu.VMEM((bm, d), jnp.float32),
                        pltpu.SemaphoreType.DMA((2,)),
                        pltpu.SemaphoreType.DMA((2,))],
    )
    return pl.pallas_call(
        _expert_kernel,
        out_shape=jax.ShapeDtypeStruct((t, d), jnp.float32),
        grid_spec=grid_spec,
        compiler_params=pltpu.CompilerParams(dimension_semantics=("arbitrary",),
                                             vmem_limit_bytes=VMEM_LIMIT_BYTES),
        name="expert_pair_ffn",
    )(tok_of_row, e_lo, e_hi, nvalid, xext, w_gate, w_up, w_down, w_gate, w_up, w_down, ln_g, ln_b)


def kernel(x, conv_w_in, conv_w, conv_w_out, ml_w_in, ml_b_gate, ml_norm_g, ml_w_out, ln_mix_g, ln_mix_b,
           ln_ffn_g, ln_ffn_b, router_w, router_b, exp_w_gate, exp_w_up, exp_w_down):
    bsz, seq, d = x.shape
    assert d == D_MODEL and seq % MIX_TILE == 0 and (bsz * seq) % EXPERT_BLOCK == 0
    bf16 = jnp.bfloat16
    vec = lambda a: a.astype(jnp.float32)[None, :]
    (rw_hi, rw_lo), rb = _prep_router(router_w, router_b)
    x2d = x.reshape(bsz * seq, d).astype(jnp.float32)
    expert_weights = (exp_w_gate.astype(jnp.float32), exp_w_up.astype(jnp.float32),
                      exp_w_down.astype(jnp.float32))
    for i in range(DEPTH):
        j = i // 2
        if i % 2 == 0:
            xext, meta_t, counts, *w16 = _conv_mixer_layer(
                x2d, seq, conv_w_in[j].astype(bf16), _pad_rows(conv_w[j].astype(jnp.float32)),
                conv_w_out[j].astype(bf16), vec(ln_mix_g[i]), vec(ln_mix_b[i]), rw_hi, rw_lo, rb,
                i, expert_weights)
        else:
            xext, meta_t, counts, *w16 = _mlstm_mixer_layer(
                x2d, seq, ml_w_in[j], ml_b_gate[j], ml_norm_g[j].astype(jnp.float32), ml_w_out[j],
                vec(ln_mix_g[i]), vec(ln_mix_b[i]), rw_hi, rw_lo, rb, i, expert_weights)
        x2d = _moe_layer(xext, meta_t, counts, *w16, vec(ln_ffn_g[i]), vec(ln_ffn_b[i]))
    return x2d.reshape(bsz, seq, d).astype(x.dtype)
```

```python
import functools
import itertools

import numpy as np
import jax
import jax.numpy as jnp
from jax import lax
from jax.experimental import pallas as pl
from jax.experimental.pallas import tpu as pltpu

D_MODEL = 1024
DEPTH = 2
CONV_WIDTH = 3
ML_HEADS = 8
ML_QK_DIM = D_MODEL // (2 * ML_HEADS)
ML_V_DIM = D_MODEL // ML_HEADS
ML_QK_W = ML_HEADS * ML_QK_DIM
ML_V_W = ML_HEADS * ML_V_DIM
N_EXPERTS = 16
N_GROUPS = 4
EXPERTS_PER_GROUP = N_EXPERTS // N_GROUPS
D_EXPERT = 3 * D_MODEL // 2
ALPHA = (2 * DEPTH) ** 0.25
LN_EPS = 1e-5
HEAD_NORM_EPS = 1e-6

LANES = 128
SUBLANES = 8
VMEM_LIMIT_BYTES = 56 * 1024 * 1024

PAIRS = tuple(itertools.combinations(range(EXPERTS_PER_GROUP), 2))
N_PAIRS = len(PAIRS)
N_CLASSES = N_GROUPS * N_PAIRS
CLASS_ROWS = -(-N_CLASSES // 16) * 16
CLASS_E_LO = np.array([g * EXPERTS_PER_GROUP + a for g in range(N_GROUPS) for a, _ in PAIRS], np.int32)
CLASS_E_HI = np.array([g * EXPERTS_PER_GROUP + b for g in range(N_GROUPS) for _, b in PAIRS], np.int32)

META_G_LO, META_G_HI, META_CLASS, META_RANK = 0, 1, 2, 3
ROW_W = D_MODEL + LANES

MIX_TILE = 512
EXPERT_BLOCK = 256
ML_CHUNK = 128


def _layer_norm(z, g, b):
    mu = jnp.mean(z, axis=-1, keepdims=True)
    zc = z - mu
    var = jnp.mean(zc * zc, axis=-1, keepdims=True)
    return zc * lax.rsqrt(var + LN_EPS) * g + b


def _split_bf16(a):
    hi = a.astype(jnp.bfloat16)
    lo = (a - hi.astype(jnp.float32)).astype(jnp.bfloat16)
    return hi, lo


def _route_and_rank(x1, rw_hi, rw_lo, rb_col, count_ref):
    ts = x1.shape[0]
    f32, bf16 = jnp.float32, jnp.bfloat16
    n_g, n_m = N_GROUPS, EXPERTS_PER_GROUP

    x_hi, x_lo = _split_bf16(x1)
    logits = (jnp.dot(x_hi, rw_hi, preferred_element_type=f32)
              + jnp.dot(x_hi, rw_lo, preferred_element_type=f32)
              + jnp.dot(x_lo, rw_hi, preferred_element_type=f32))
    lt = logits.T[:N_EXPERTS, :]
    ex = jnp.exp(lt - jnp.max(lt, axis=0, keepdims=True))
    probs = ex / jnp.sum(ex, axis=0, keepdims=True)
    sel = probs + rb_col
    member = lambda a, j: a[j * n_g:(j + 1) * n_g, :]

    in_top2 = []
    for j in range(n_m):
        beaten_by = jnp.zeros((n_g, ts), jnp.int32)
        for k in range(n_m):
            if k != j:
                wins = member(sel, k) > member(sel, j)
                if k < j:
                    wins = wins | (member(sel, k) == member(sel, j))
                beaten_by = beaten_by + wins.astype(jnp.int32)
        in_top2.append(beaten_by < 2)
    score = sum(jnp.where(in_top2[j], member(sel, j), 0.0) for j in range(n_m))
    group = lax.broadcasted_iota(jnp.int32, (n_g, ts), 0)
    best = jnp.max(score, axis=0, keepdims=True)
    g_best = jnp.min(jnp.where(score == best, group, n_g), axis=0, keepdims=True)
    in_best = group == g_best
    picked = [jnp.max(jnp.where(in_top2[j] & in_best, 1, 0), axis=0, keepdims=True) for j in range(n_m)]
    p_pick = [jnp.sum(jnp.where(in_top2[j] & in_best, member(probs, j), 0.0), axis=0, keepdims=True)
              for j in range(n_m)]
    a = functools.reduce(jnp.minimum, [jnp.where(picked[j] > 0, j, n_m) for j in range(n_m)])
    b = functools.reduce(jnp.maximum, [jnp.where(picked[j] > 0, j, -1) for j in range(n_m)])
    p_lo = sum(jnp.where(a == j, p_pick[j], 0.0) for j in range(n_m))
    p_hi = sum(jnp.where(b == j, p_pick[j], 0.0) for j in range(n_m))
    g_lo = p_lo / (p_lo + p_hi)
    g_hi = p_hi / (p_lo + p_hi)
    pair = ((a * (2 * n_m - 1 - a)) >> 1) + (b - a - 1)
    cls = g_best * N_PAIRS + pair

    n_rows = count_ref.shape[0]
    onehot = jnp.where(lax.broadcasted_iota(jnp.int32, (n_rows, ts), 0) == cls, 1.0, 0.0)
    earlier = jnp.where(lax.broadcasted_iota(jnp.int32, (ts, ts), 0)
                        < lax.broadcasted_iota(jnp.int32, (ts, ts), 1), 1.0, 0.0).astype(bf16)
    before = jnp.dot(onehot.astype(bf16), earlier, preferred_element_type=f32)
    running = count_ref[...]
    running_ts = jnp.concatenate([running] * (ts // LANES), axis=1)
    rank = jnp.sum(onehot * (before + running_ts), axis=0, keepdims=True)
    count_ref[...] = running + jnp.dot(onehot.astype(bf16), jnp.ones((ts, LANES), bf16),
                                       preferred_element_type=f32)
    return jnp.concatenate([g_lo, g_hi, cls.astype(f32), rank, jnp.zeros((SUBLANES - 4, ts), f32)], axis=0)


def _cast_slabs(w32_refs, w16_refs):
    for w32_ref, w16_ref in zip(w32_refs, w16_refs):
        w16_ref[...] = w32_ref[0].astype(w16_ref.dtype)


def _cast_slab_specs(layer, n_steps, weights):
    args, in_specs, out_shapes, out_specs = [], [], [], []
    for w in weights:
        n_layers, n_exp, rows, cols = w.shape
        slab, rem = divmod(n_exp * rows, n_steps)
        assert rem == 0 and slab % (2 * SUBLANES) == 0
        args.append(w.reshape(n_layers, n_exp * rows, cols))
        in_specs.append(pl.BlockSpec((1, slab, cols), lambda i: (layer, i, 0)))
        out_shapes.append(jax.ShapeDtypeStruct((n_exp * rows, cols), jnp.bfloat16))
        out_specs.append(pl.BlockSpec((slab, cols), lambda i: (i, 0)))
    return args, in_specs, out_shapes, out_specs


def _conv_mixer_kernel(x_ref, w_in_ref, w_conv_ref, w_out_ref, ln_g_ref, ln_b_ref,
                       rw_hi_ref, rw_lo_ref, rb_ref, wg32_ref, wu32_ref, wd32_ref,
                       out_ref, meta_t_ref, count_ref, wg16_ref, wu16_ref, wd16_ref, carry_ref, *, tiles_per_seq):
    i = pl.program_id(0)

    @pl.when(i == 0)
    def _():
        count_ref[...] = jnp.zeros_like(count_ref)

    @pl.when(i % tiles_per_seq == 0)
    def _():
        carry_ref[...] = jnp.zeros_like(carry_ref)

    f32 = jnp.float32
    d = D_MODEL
    x = x_ref[...]
    ts = x.shape[0]
    proj = jnp.dot(x.astype(jnp.bfloat16), w_in_ref[...], preferred_element_type=f32)
    gate_b, gate_c, h = proj[:, :d], proj[:, d:2 * d], proj[:, 2 * d:]
    u = gate_c * h
    row = lax.broadcasted_iota(jnp.int32, (ts, d), 0)
    prev = carry_ref[...]
    u1 = jnp.where(row == 0, prev[SUBLANES - 1:SUBLANES, :], pltpu.roll(u, 1, axis=0))
    u2 = jnp.where(row == 0, prev[SUBLANES - 2:SUBLANES - 1, :],
                   jnp.where(row == 1, prev[SUBLANES - 1:SUBLANES, :], pltpu.roll(u, 2, axis=0)))
    carry_ref[...] = u[ts - SUBLANES:, :]
    wc = w_conv_ref[...]
    conv = wc[0:1, :] * u2 + wc[1:2, :] * u1 + wc[2:3, :] * u
    mix = jnp.dot((gate_b * conv).astype(jnp.bfloat16), w_out_ref[...], preferred_element_type=f32)
    x1 = _layer_norm(ALPHA * x + mix, ln_g_ref[...], ln_b_ref[...])
    meta = _route_and_rank(x1, rw_hi_ref[...], rw_lo_ref[...], rb_ref[...], count_ref)
    out_ref[:, :d] = x1
    out_ref[:, d:] = jnp.concatenate([meta, jnp.zeros((LANES - SUBLANES, meta.shape[1]), meta.dtype)], axis=0).T
    meta_t_ref[...] = meta
    _cast_slabs((wg32_ref, wu32_ref, wd32_ref), (wg16_ref, wu16_ref, wd16_ref))


def _const_spec(shape):
    return pl.BlockSpec(shape, lambda i: (0,) * len(shape))


def _pad_rows(a, rows=SUBLANES):
    return jnp.pad(a, ((0, rows - a.shape[0]), (0, 0)))


def _prep_router(router_w, router_b):
    order = np.array([g * EXPERTS_PER_GROUP + j for j in range(EXPERTS_PER_GROUP) for g in range(N_GROUPS)])
    rw = jnp.pad(router_w.astype(jnp.float32)[:, order], ((0, 0), (0, LANES - N_EXPERTS)))
    rb = router_b.astype(jnp.float32)[order][:, None]
    return _split_bf16(rw), rb


def _conv_mixer_layer(x2d, seq, w_in, w_conv, w_out, ln_g, ln_b, rw_hi, rw_lo, rb, layer, expert_weights):
    t, d = x2d.shape
    ts = MIX_TILE
    grid = (t // ts,)
    cast_args, cast_in, cast_shapes, cast_out = _cast_slab_specs(layer, grid[0], expert_weights)
    return pl.pallas_call(
        functools.partial(_conv_mixer_kernel, tiles_per_seq=seq // ts),
        out_shape=(jax.ShapeDtypeStruct((t, ROW_W), jnp.float32),
                   jax.ShapeDtypeStruct((SUBLANES, t), jnp.float32),
                   jax.ShapeDtypeStruct((CLASS_ROWS, LANES), jnp.float32), *cast_shapes),
        grid=grid,
        in_specs=[pl.BlockSpec((ts, d), lambda i: (i, 0)),
                  _const_spec((d, 3 * d)), _const_spec((SUBLANES, d)), _const_spec((d, d)),
                  _const_spec((1, d)), _const_spec((1, d)),
                  _const_spec((d, LANES)), _const_spec((d, LANES)), _const_spec((N_EXPERTS, 1)), *cast_in],
        out_specs=(pl.BlockSpec((ts, ROW_W), lambda i: (i, 0)),
                   pl.BlockSpec((SUBLANES, ts), lambda i: (0, i)),
                   _const_spec((CLASS_ROWS, LANES)), *cast_out),
        scratch_shapes=[pltpu.VMEM((SUBLANES, d), jnp.float32)],
        compiler_params=pltpu.CompilerParams(dimension_semantics=("arbitrary",),
                                             vmem_limit_bytes=VMEM_LIMIT_BYTES),
        name="conv_mixer_route",
    )(x2d, w_in, w_conv, w_out, ln_g, ln_b, rw_hi, rw_lo, rb, *cast_args)


def _log_sigmoid(z):
    return jnp.minimum(z, 0.0) - jnp.log1p(jnp.exp(-jnp.abs(z)))


def _split3_f32(a):
    f32, bf16 = jnp.float32, jnp.bfloat16
    hi = a.astype(bf16).astype(f32)
    r1 = a - hi
    mid = r1.astype(bf16).astype(f32)
    lo = (r1 - mid).astype(bf16).astype(f32)
    return hi, mid, lo


_COL_U, _COL_ONE, _COL_W, _COL_EMT, _COL_END = 0, 24, 48, 64, 80


def _mlstm_mixer_kernel(x_ref, w_qvo_ref, wkt_ref, wgt_hi_ref, wgt_lo_ref, bg_row_ref, norm_g_ref, w_out_ref,
                        ln_g_ref, ln_b_ref, rw_hi_ref, rw_lo_ref, rb_ref, wg32_ref, wu32_ref, wd32_ref,
                        out_ref, meta_t_ref, count_ref, wg16_ref, wu16_ref, wd16_ref,
                        proj_ref, kt_ref, grow_ref, colmat_ref, bexp_ref, h_ref, c_ref, m_ref, *, tiles_per_seq):
    i = pl.program_id(0)
    f32, bf16 = jnp.float32, jnp.bfloat16
    nh, dk, dv, d = ML_HEADS, ML_QK_DIM, ML_V_DIM, D_MODEL
    L = ML_CHUNK
    ts = x_ref.shape[0]
    n_chunks = ts // L
    o_base = ML_QK_W + ML_V_W

    @pl.when(i == 0)
    def _():
        count_ref[...] = jnp.zeros_like(count_ref)

    @pl.when(i % tiles_per_seq == 0)
    def _():
        c_ref[...] = jnp.zeros_like(c_ref)
        m_ref[...] = jnp.zeros_like(m_ref)

    x = x_ref[...]
    x_hi, x_lo = _split_bf16(x)
    nt = (((1,), (1,)), ((), ()))
    proj_ref[...] = jnp.dot(x_hi, w_qvo_ref[...], preferred_element_type=f32)
    k_t = lax.dot_general(wkt_ref[...], x_hi, nt, preferred_element_type=f32)
    g_row = (lax.dot_general(wgt_hi_ref[...], x_hi, nt, preferred_element_type=f32)
             + lax.dot_general(wgt_lo_ref[...], x_hi, nt, preferred_element_type=f32)
             + lax.dot_general(wgt_hi_ref[...], x_lo, nt, preferred_element_type=f32)) + bg_row_ref[...]
    r_i = lax.broadcasted_iota(jnp.int32, (ts, ts), 0)
    c_i = lax.broadcasted_iota(jnp.int32, (ts, ts), 1)
    tri_row = jnp.where(((r_i // L) == (c_i // L)) & (r_i <= c_i), 1.0, 0.0).astype(bf16)
    lf_hi, lf_lo = _split_bf16(_log_sigmoid(g_row))
    cum_row = (jnp.dot(lf_hi, tri_row, preferred_element_type=f32)
               + jnp.dot(lf_lo, tri_row, preferred_element_type=f32))
    i_row, b_row = g_row[:nh, :], cum_row[nh:, :]

    g = i_row - b_row
    lane_in_chunk = lax.broadcasted_iota(jnp.int32, (nh, ts), 1) & (L - 1)
    cm = g
    shift = 1
    while shift < L:
        cm = jnp.maximum(cm, jnp.where(lane_in_chunk >= shift, pltpu.roll(cm, shift, axis=1), -jnp.inf))
        shift *= 2
    m_prev = m_ref[:, 0:1]
    u_parts, w_parts, gr_parts, gs_parts = [], [], [], []
    for c in range(n_chunks):
        sl = slice(c * L, (c + 1) * L)
        u_c = jnp.maximum(m_prev, cm[:, sl])
        b_last = b_row[:, (c + 1) * L - 1:(c + 1) * L]
        m_new = b_last + u_c[:, L - 1:L]
        u_parts.append(u_c)
        w_parts.append(jnp.exp(m_prev - u_c))
        gr_parts.append(jnp.exp(b_last - b_row[:, sl] + i_row[:, sl] - m_new))
        gs_parts.append(jnp.broadcast_to(jnp.exp(b_last + m_prev - m_new), (nh, L)))
        m_prev = m_new
    m_ref[...] = jnp.broadcast_to(m_prev, m_ref.shape)
    u = jnp.concatenate(u_parts, axis=1)
    w_inter = jnp.concatenate(w_parts, axis=1)
    emt = jnp.exp(-(b_row + u))
    w_hi = w_inter.astype(bf16).astype(f32)
    e_hi = emt.astype(bf16).astype(f32)
    col_rows = jnp.concatenate(
        [*_split3_f32(u), jnp.ones((_COL_W - _COL_ONE, ts), f32), w_hi, w_inter - w_hi, e_hi, emt - e_hi,
         jnp.zeros((LANES - _COL_END, ts), f32)], axis=0)
    colmat_ref[...] = col_rows.T.astype(bf16)
    b_exp = jnp.concatenate([jnp.full((_COL_ONE - _COL_U, ts), -1.0, f32), *_split3_f32(g),
                             jnp.zeros((LANES - _COL_W, ts), f32)], axis=0).astype(bf16)
    for c in range(n_chunks):
        sl = slice(c * L, (c + 1) * L)
        bexp_ref[c] = b_exp[:, sl]
        grow_ref[c] = jnp.concatenate([gr_parts[c], gs_parts[c]], axis=0)
        kt_ref[c] = k_t[:, sl]

    causal = lax.broadcasted_iota(jnp.int32, (L, L), 0) >= lax.broadcasted_iota(jnp.int32, (L, L), 1)
    ones_col = jnp.where(lax.broadcasted_iota(jnp.int32, (L, dv), 1) == 0, 1.0, 0.0).astype(bf16)
    lane_ll = lax.broadcasted_iota(jnp.int32, (L, LANES), 1)
    sel_r = lax.broadcasted_iota(jnp.int32, (LANES, 2 * LANES), 0)
    sel_c = lax.broadcasted_iota(jnp.int32, (LANES, 2 * LANES), 1)
    is_w_row = (sel_r >= _COL_W) & (sel_r < _COL_EMT)
    is_e_row = (sel_r >= _COL_EMT) & (sel_r < _COL_END)
    b_sel = jnp.where((is_w_row & (sel_c < LANES)) | (is_e_row & (sel_c >= LANES)), 1.0, 0.0).astype(bf16)
    r2_r = lax.broadcasted_iota(jnp.int32, (2 * dv, 2 * dv), 0)
    r2_c = lax.broadcasted_iota(jnp.int32, (2 * dv, 2 * dv), 1)
    rhs2 = jnp.where((r2_r < dv) & (r2_c < dv), 1.0 / dv,
                     jnp.where((r2_r == dv) & (r2_c >= dv), 1.0, 0.0)).astype(bf16)

    def chunk_body(c, carry):
        r0 = pl.multiple_of(c * L, L)
        gr = grow_ref[c]
        colmat = colmat_ref[pl.ds(r0, L), :]
        rhs1 = jnp.concatenate([bexp_ref[c], b_sel], axis=1)
        heads = range(nh)
        c_pairs = [c_ref[p] for p in range(nh // 2)]
        kt_pairs = [kt_ref[c, p * LANES:(p + 1) * LANES, :] for p in range(nh // 2)]

        ew, q_m, s_mat, v_ext = [], [], [], []
        for h in heads:
            p, hh = divmod(h, 2)
            lhs = jnp.where((lane_ll & (nh - 1)) == h, colmat, 0)
            ew.append(jnp.dot(lhs, rhs1, preferred_element_type=f32))
            q2 = proj_ref[pl.ds(r0, L), p * LANES:(p + 1) * LANES] * (dk ** -0.5)
            q_m.append(jnp.where((lane_ll >= hh * dk) & (lane_ll < (hh + 1) * dk), q2, 0.0))
            s_mat.append(jnp.dot(q_m[h].astype(bf16), kt_pairs[p].astype(bf16), preferred_element_type=f32))
            v = proj_ref[pl.ds(r0, L), ML_QK_W + h * dv:ML_QK_W + (h + 1) * dv].astype(bf16)
            v_ext.append(jnp.concatenate([v, ones_col], axis=1))
        tot = []
        for h in heads:
            p, hh = divmod(h, 2)
            a = jnp.exp(jnp.where(causal, ew[h][:, :L], -jnp.inf)) * s_mat[h]
            qw = q_m[h] * ew[h][:, L:L + LANES]
            lhs = jnp.concatenate([a.astype(bf16), qw.astype(bf16)], axis=1)
            rhs = jnp.concatenate([v_ext[h], c_pairs[p].astype(bf16)], axis=0)
            tot.append(jnp.dot(lhs, rhs, preferred_element_type=f32))
        for h in heads:
            num = tot[h][:, :dv]
            lhs2 = jnp.concatenate([(num * num).astype(bf16), tot[h][:, dv:].astype(bf16)], axis=1)
            r2 = jnp.dot(lhs2, rhs2, preferred_element_type=f32)
            inv = 1.0 / jnp.maximum(jnp.abs(r2[:, dv:]), ew[h][:, L + LANES:])
            hn = (num * inv) * lax.rsqrt(r2[:, :dv] * inv * inv + HEAD_NORM_EPS)
            o_pre = proj_ref[pl.ds(r0, L), o_base + h * dv:o_base + (h + 1) * dv]
            h_ref[pl.ds(r0, L), h * dv:(h + 1) * dv] = (
                jax.nn.sigmoid(o_pre) * hn * norm_g_ref[:, h * dv:(h + 1) * dv])
        for h in heads:
            p, hh = divmod(h, 2)
            rows = slice(hh * dk, (hh + 1) * dk)
            kg = (kt_pairs[p][rows, :] * gr[h:h + 1, :]).astype(bf16)
            gs = gr[nh + h:nh + h + 1, :]
            c_ref[p, rows, :] = (jnp.concatenate([gs, gs], axis=1) * c_pairs[p][rows, :]
                                 + jnp.dot(kg, v_ext[h], preferred_element_type=f32))
        return carry

    lax.fori_loop(0, n_chunks, chunk_body, 0)

    mix = jnp.dot(h_ref[...].astype(bf16), w_out_ref[...], preferred_element_type=f32)
    x1 = _layer_norm(ALPHA * x + mix, ln_g_ref[...], ln_b_ref[...])
    meta = _route_and_rank(x1, rw_hi_ref[...], rw_lo_ref[...], rb_ref[...], count_ref)
    out_ref[:, :d] = x1
    out_ref[:, d:] = jnp.concatenate([meta, jnp.zeros((LANES - SUBLANES, meta.shape[1]), meta.dtype)], axis=0).T
    meta_t_ref[...] = meta
    _cast_slabs((wg32_ref, wu32_ref, wd32_ref), (wg16_ref, wu16_ref, wd16_ref))


def _mlstm_mixer_layer(x2d, seq, w_in, b_gate, norm_g, w_out, ln_g, ln_b, rw_hi, rw_lo, rb, layer,
                       expert_weights):
    t, d = x2d.shape
    ts = MIX_TILE
    nh = ML_HEADS
    cast_args, cast_in, cast_shapes, cast_out = _cast_slab_specs(layer, t // ts, expert_weights)
    assert ML_CHUNK == LANES and ts % ML_CHUNK == 0 and 2 * ML_QK_DIM == LANES and ML_V_DIM == LANES
    n_qkvo = 2 * ML_QK_W + 2 * ML_V_W
    n_qvo = ML_QK_W + 2 * ML_V_W
    w_qvo = jnp.concatenate([w_in[:, :ML_QK_W], w_in[:, 2 * ML_QK_W:n_qkvo]], axis=1).astype(jnp.bfloat16)
    wkt = w_in[:, ML_QK_W:2 * ML_QK_W].T.astype(jnp.bfloat16)
    w_g = w_in[:, n_qkvo:].astype(jnp.float32)
    wgt_hi, wgt_lo = _split_bf16(w_g.T)
    bg_row = b_gate.astype(jnp.float32)[:, None]
    return pl.pallas_call(
        functools.partial(_mlstm_mixer_kernel, tiles_per_seq=seq // ts),
        out_shape=(jax.ShapeDtypeStruct((t, ROW_W), jnp.float32),
                   jax.ShapeDtypeStruct((SUBLANES, t), jnp.float32),
                   jax.ShapeDtypeStruct((CLASS_ROWS, LANES), jnp.float32), *cast_shapes),
        grid=(t // ts,),
        in_specs=[pl.BlockSpec((ts, d), lambda i: (i, 0)),
                  _const_spec((d, n_qvo)), _const_spec((ML_QK_W, d)),
                  _const_spec((2 * nh, d)), _const_spec((2 * nh, d)),
                  _const_spec((2 * nh, 1)),
                  _const_spec((1, d)), _const_spec((d, d)),
                  _const_spec((1, d)), _const_spec((1, d)),
                  _const_spec((d, LANES)), _const_spec((d, LANES)), _const_spec((N_EXPERTS, 1)), *cast_in],
        out_specs=(pl.BlockSpec((ts, ROW_W), lambda i: (i, 0)),
                   pl.BlockSpec((SUBLANES, ts), lambda i: (0, i)),
                   _const_spec((CLASS_ROWS, LANES)), *cast_out),
        scratch_shapes=[pltpu.VMEM((ts, n_qvo), jnp.float32),
                        pltpu.VMEM((ts // ML_CHUNK, ML_QK_W, ML_CHUNK), jnp.float32),
                        pltpu.VMEM((ts // ML_CHUNK, 2 * nh, ML_CHUNK), jnp.float32),
                        pltpu.VMEM((ts, LANES), jnp.bfloat16),
                        pltpu.VMEM((ts // ML_CHUNK, LANES, ML_CHUNK), jnp.bfloat16),
                        pltpu.VMEM((ts, d), jnp.float32),
                        pltpu.VMEM((nh // 2, 2 * ML_QK_DIM, 2 * ML_V_DIM), jnp.float32),
                        pltpu.VMEM((nh, ML_CHUNK), jnp.float32)],
        compiler_params=pltpu.CompilerParams(dimension_semantics=("arbitrary",),
                                             vmem_limit_bytes=VMEM_LIMIT_BYTES),
        name="mlstm_mixer_route",
    )(x2d, w_qvo, wkt, wgt_hi, wgt_lo, bg_row, norm_g[None, :], w_out.astype(jnp.bfloat16),
      ln_g, ln_b, rw_hi, rw_lo, rb, *cast_args)


def _expert_kernel(tok_ref, elo_ref, ehi_ref, nvalid_ref,
                   x_hbm, wg_lo_ref, wu_lo_ref, wd_lo_ref, wg_hi_ref, wu_hi_ref, wd_hi_ref,
                   ln_g_ref, ln_b_ref, out_hbm, xbuf0, xbuf1, obuf0, obuf1, gather_sem, scatter_sem):
    i = pl.program_id(0)
    nb = pl.num_programs(0)
    bm = xbuf0.shape[0]
    d = D_MODEL
    xbufs, obufs = (xbuf0, xbuf1), (obuf0, obuf1)
    nv = nvalid_ref[i]
    nv_prev = nvalid_ref[jnp.maximum(i - 1, 0)]

    def start_gather(j, s):
        for r in range(bm):
            tok = tok_ref[j * bm + r]
            pltpu.make_async_copy(x_hbm.at[pl.ds(tok, 1)], xbufs[s].at[pl.ds(r, 1)], gather_sem.at[s]).start()

    def wait_gather(s):
        pltpu.make_async_copy(xbufs[s], xbufs[s], gather_sem.at[s]).wait()

    def scatter_copy(j, s, r):
        tok = tok_ref[j * bm + r]
        return pltpu.make_async_copy(obufs[s].at[pl.ds(r, 1)], out_hbm.at[pl.ds(tok, 1)], scatter_sem.at[s])

    def wait_scatter(s, n):
        @pl.when(n == bm)
        def _():
            pltpu.make_async_copy(obufs[s], obufs[s], scatter_sem.at[s]).wait()

        @pl.when(n < bm)
        def _():
            def body(r, c):
                pltpu.make_async_copy(obufs[s].at[pl.ds(0, 1)], obufs[s].at[pl.ds(0, 1)], scatter_sem.at[s]).wait()
                return c
            lax.fori_loop(0, n, body, 0)

    @pl.when(i == 0)
    def _():
        start_gather(0, 0)

    def step(s):
        @pl.when((i == 0) | (nv_prev > 0))
        def _():
            wait_gather(s)

        @pl.when(nv > 0)
        def _():
            start_gather(i + 1, 1 - s)
            xb = xbufs[s][...]
            x = xb[:, :d]
            g_lo = xb[:, d + META_G_LO:d + META_G_LO + 1]
            g_hi = xb[:, d + META_G_HI:d + META_G_HI + 1]
            x16 = x.astype(jnp.bfloat16)

            def ffn(wg_ref, wu_ref, wd_ref):
                g = jnp.dot(x16, wg_ref[0], preferred_element_type=jnp.float32)
                u = jnp.dot(x16, wu_ref[0], preferred_element_type=jnp.float32)
                h = (g * jax.nn.sigmoid(g)) * u
                return jnp.dot(h.astype(jnp.bfloat16), wd_ref[0], preferred_element_type=jnp.float32)

            y = g_lo * ffn(wg_lo_ref, wu_lo_ref, wd_lo_ref) + g_hi * ffn(wg_hi_ref, wu_hi_ref, wd_hi_ref)
            obufs[s][...] = _layer_norm(ALPHA * x + y, ln_g_ref[...], ln_b_ref[...])

            @pl.when(nv == bm)
            def _():
                for r in range(bm):
                    scatter_copy(i, s, r).start()

            @pl.when(nv < bm)
            def _():
                def body(r, c):
                    scatter_copy(i, s, r).start()
                    return c
                lax.fori_loop(0, nv, body, 0)

        @pl.when(i > 0)
        def _():
            wait_scatter(1 - s, nv_prev)

        @pl.when(i == nb - 1)
        def _():
            wait_scatter(s, nv)

            @pl.when(nv > 0)
            def _():
                wait_gather(1 - s)

    for s in range(2):
        pl.when(i % 2 == s)(functools.partial(step, s))


def _moe_layer(xext, meta_t, counts, w_gate, w_up, w_down, ln_g, ln_b):
    t = xext.shape[0]
    d, f, bm = D_MODEL, D_EXPERT, EXPERT_BLOCK
    n_blocks = t // bm + N_CLASSES
    n_rows = n_blocks * bm

    cls = meta_t[META_CLASS].astype(jnp.int32)
    rank = meta_t[META_RANK].astype(jnp.int32)
    cnt = counts[:N_CLASSES, 0].astype(jnp.int32)
    cls_blocks = (cnt + bm - 1) // bm
    blk_end = jnp.cumsum(cls_blocks)
    blk_start = blk_end - cls_blocks
    total_blocks = blk_end[-1]
    row_start = jnp.sum(jnp.where(cls[None, :] == jnp.arange(N_CLASSES, dtype=jnp.int32)[:, None],
                                  blk_start[:, None] * bm, 0), axis=0)
    dest = row_start + rank
    tok_of_row = jnp.zeros((n_rows + bm,), jnp.int32).at[dest].set(
        jnp.arange(t, dtype=jnp.int32), unique_indices=True)
    blk = jnp.arange(n_blocks, dtype=jnp.int32)
    blk_cls = jnp.sum(jnp.minimum(blk, total_blocks - 1)[:, None] >= blk_end[None, :], axis=1).astype(jnp.int32)
    blk_cls = jnp.minimum(blk_cls, N_CLASSES - 1)
    nvalid = jnp.clip(cnt[blk_cls] - (blk - blk_start[blk_cls]) * bm, 0, bm)
    nvalid = jnp.where(blk < total_blocks, nvalid, 0).astype(jnp.int32)
    e_lo = jnp.asarray(CLASS_E_LO)[blk_cls]
    e_hi = jnp.asarray(CLASS_E_HI)[blk_cls]

    w_gate, w_up, w_down = (w_gate.reshape(N_EXPERTS, d, f), w_up.reshape(N_EXPERTS, d, f),
                            w_down.reshape(N_EXPERTS, f, d))
    w_lo = lambda shape: pl.BlockSpec((1,) + shape, lambda i, tok, elo, ehi, nv: (elo[i], 0, 0))
    w_hi = lambda shape: pl.BlockSpec((1,) + shape, lambda i, tok, elo, ehi, nv: (ehi[i], 0, 0))
    vec = pl.BlockSpec((1, d), lambda i, tok, elo, ehi, nv: (0, 0))
    grid_spec = pltpu.PrefetchScalarGridSpec(
        num_scalar_prefetch=4,
        grid=(n_blocks,),
        in_specs=[pl.BlockSpec(memory_space=pl.ANY),
                  w_lo((d, f)), w_lo((d, f)), w_lo((f, d)),
                  w_hi((d, f)), w_hi((d, f)), w_hi((f, d)),
                  vec, vec],
        out_specs=pl.BlockSpec(memory_space=pl.ANY),
        scratch_shapes=[pltpu.VMEM((bm, ROW_W), jnp.float32), pltpu.VMEM((bm, ROW_W), jnp.float32),
                        pltpu.VMEM((bm, d), jnp.float32), pltpu.VMEM((bm, d), jnp.float32),
                        pltpu.SemaphoreType.DMA((2,)),
                        pltpu.SemaphoreType.DMA((2,))],
    )
    return pl.pallas_call(
        _expert_kernel,
        out_shape=jax.ShapeDtypeStruct((t, d), jnp.float32),
        grid_spec=grid_spec,
        compiler_params=pltpu.CompilerParams(dimension_semantics=("arbitrary",),
                                             vmem_limit_bytes=VMEM_LIMIT_BYTES),
        name="expert_pair_ffn",
    )(tok_of_row, e_lo, e_hi, nvalid, xext, w_gate, w_up, w_down, w_gate, w_up, w_down, ln_g, ln_b)


def kernel(x, conv_w_in, conv_w, conv_w_out, ml_w_in, ml_b_gate, ml_norm_g, ml_w_out, ln_mix_g, ln_mix_b,
           ln_ffn_g, ln_ffn_b, router_w, router_b, exp_w_gate, exp_w_up, exp_w_down):
    bsz, seq, d = x.shape
    assert d == D_MODEL and seq % MIX_TILE == 0 and (bsz * seq) % EXPERT_BLOCK == 0
    bf16 = jnp.bfloat16
    vec = lambda a: a.astype(jnp.float32)[None, :]
    (rw_hi, rw_lo), rb = _prep_router(router_w, router_b)
    x2d = x.reshape(bsz * seq, d).astype(jnp.float32)
    expert_weights = (exp_w_gate.astype(jnp.float32), exp_w_up.astype(jnp.float32),
                      exp_w_down.astype(jnp.float32))
    for i in range(DEPTH):
        j = i // 2
        if i % 2 == 0:
            xext, meta_t, counts, *w16 = _conv_mixer_layer(
                x2d, seq, conv_w_in[j].astype(bf16), _pad_rows(conv_w[j].astype(jnp.float32)),
                conv_w_out[j].astype(bf16), vec(ln_mix_g[i]), vec(ln_mix_b[i]), rw_hi, rw_lo, rb,
                i, expert_weights)
        else:
            xext, meta_t, counts, *w16 = _mlstm_mixer_layer(
                x2d, seq, ml_w_in[j], ml_b_gate[j], ml_norm_g[j].astype(jnp.float32), ml_w_out[j],
                vec(ln_mix_g[i]), vec(ln_mix_b[i]), rw_hi, rw_lo, rb, i, expert_weights)
        x2d = _moe_layer(xext, meta_t, counts, *w16, vec(ln_ffn_g[i]), vec(ln_ffn_b[i]))
    return x2d.reshape(bsz, seq, d).astype(x.dtype)
```

```python
import functools
import itertools

import numpy as np
import jax
import jax.numpy as jnp
from jax import lax
from jax.experimental import pallas as pl
from jax.experimental.pallas import tpu as pltpu

D_MODEL = 1024
DEPTH = 2
CONV_WIDTH = 3
ML_HEADS = 8
ML_QK_DIM = D_MODEL // (2 * ML_HEADS)
ML_V_DIM = D_MODEL // ML_HEADS
ML_QK_W = ML_HEADS * ML_QK_DIM
ML_V_W = ML_HEADS * ML_V_DIM
N_EXPERTS = 16
N_GROUPS = 4
EXPERTS_PER_GROUP = N_EXPERTS // N_GROUPS
D_EXPERT = 3 * D_MODEL // 2
ALPHA = (2 * DEPTH) ** 0.25
LN_EPS = 1e-5
HEAD_NORM_EPS = 1e-6

LANES = 128
SUBLANES = 8
VMEM_LIMIT_BYTES = 56 * 1024 * 1024

PAIRS = tuple(itertools.combinations(range(EXPERTS_PER_GROUP), 2))
N_PAIRS = len(PAIRS)
N_CLASSES = N_GROUPS * N_PAIRS
CLASS_ROWS = -(-N_CLASSES // 16) * 16
CLASS_E_LO = np.array([g * EXPERTS_PER_GROUP + a for g in range(N_GROUPS) for a, _ in PAIRS], np.int32)
CLASS_E_HI = np.array([g * EXPERTS_PER_GROUP + b for g in range(N_GROUPS) for _, b in PAIRS], np.int32)

META_G_LO, META_G_HI, META_CLASS, META_RANK = 0, 1, 2, 3
ROW_W = D_MODEL + LANES

MIX_TILE = 512
MIX_SUBTILES = 2
EXPERT_BLOCK = 256
ML_CHUNK = 128


def _layer_norm(z, g, b):
    mu = jnp.mean(z, axis=-1, keepdims=True)
    zc = z - mu
    var = jnp.mean(zc * zc, axis=-1, keepdims=True)
    return zc * lax.rsqrt(var + LN_EPS) * g + b


def _split_bf16(a):
    hi = a.astype(jnp.bfloat16)
    lo = (a - hi.astype(jnp.float32)).astype(jnp.bfloat16)
    return hi, lo


def _route_and_rank(x1, rw_hi, rw_lo, rb_col, count_ref):
    ts = x1.shape[0]
    f32, bf16 = jnp.float32, jnp.bfloat16
    n_g, n_m = N_GROUPS, EXPERTS_PER_GROUP

    x_hi, x_lo = _split_bf16(x1)
    logits = (jnp.dot(x_hi, rw_hi, preferred_element_type=f32)
              + jnp.dot(x_hi, rw_lo, preferred_element_type=f32)
              + jnp.dot(x_lo, rw_hi, preferred_element_type=f32))
    lt = logits.T[:N_EXPERTS, :]
    ex = jnp.exp(lt - jnp.max(lt, axis=0, keepdims=True))
    probs = ex / jnp.sum(ex, axis=0, keepdims=True)
    sel = probs + rb_col
    member = lambda a, j: a[j * n_g:(j + 1) * n_g, :]

    in_top2 = []
    for j in range(n_m):
        beaten_by = jnp.zeros((n_g, ts), jnp.int32)
        for k in range(n_m):
            if k != j:
                wins = member(sel, k) > member(sel, j)
                if k < j:
                    wins = wins | (member(sel, k) == member(sel, j))
                beaten_by = beaten_by + wins.astype(jnp.int32)
        in_top2.append(beaten_by < 2)
    score = sum(jnp.where(in_top2[j], member(sel, j), 0.0) for j in range(n_m))
    group = lax.broadcasted_iota(jnp.int32, (n_g, ts), 0)
    best = jnp.max(score, axis=0, keepdims=True)
    g_best = jnp.min(jnp.where(score == best, group, n_g), axis=0, keepdims=True)
    in_best = group == g_best
    picked = [jnp.max(jnp.where(in_top2[j] & in_best, 1, 0), axis=0, keepdims=True) for j in range(n_m)]
    p_pick = [jnp.sum(jnp.where(in_top2[j] & in_best, member(probs, j), 0.0), axis=0, keepdims=True)
              for j in range(n_m)]
    a = functools.reduce(jnp.minimum, [jnp.where(picked[j] > 0, j, n_m) for j in range(n_m)])
    b = functools.reduce(jnp.maximum, [jnp.where(picked[j] > 0, j, -1) for j in range(n_m)])
    p_lo = sum(jnp.where(a == j, p_pick[j], 0.0) for j in range(n_m))
    p_hi = sum(jnp.where(b == j, p_pick[j], 0.0) for j in range(n_m))
    g_lo = p_lo / (p_lo + p_hi)
    g_hi = p_hi / (p_lo + p_hi)
    pair = ((a * (2 * n_m - 1 - a)) >> 1) + (b - a - 1)
    cls = g_best * N_PAIRS + pair

    n_rows = count_ref.shape[0]
    onehot = jnp.where(lax.broadcasted_iota(jnp.int32, (n_rows, ts), 0) == cls, 1.0, 0.0)
    earlier = jnp.where(lax.broadcasted_iota(jnp.int32, (ts, ts), 0)
                        < lax.broadcasted_iota(jnp.int32, (ts, ts), 1), 1.0, 0.0).astype(bf16)
    before = jnp.dot(onehot.astype(bf16), earlier, preferred_element_type=f32)
    running = count_ref[...]
    running_ts = jnp.concatenate([running] * (ts // LANES), axis=1)
    rank = jnp.sum(onehot * (before + running_ts), axis=0, keepdims=True)
    count_ref[...] = running + jnp.dot(onehot.astype(bf16), jnp.ones((ts, LANES), bf16),
                                       preferred_element_type=f32)
    return jnp.concatenate([g_lo, g_hi, cls.astype(f32), rank, jnp.zeros((SUBLANES - 4, ts), f32)], axis=0)


def _cast_slabs(w32_refs, w16_refs):
    for w32_ref, w16_ref in zip(w32_refs, w16_refs):
        w16_ref[...] = w32_ref[0].astype(w16_ref.dtype)


def _cast_slab_specs(layer, n_steps, weights):
    args, in_specs, out_shapes, out_specs = [], [], [], []
    for w in weights:
        n_layers, n_exp, rows, cols = w.shape
        slab, rem = divmod(n_exp * rows, n_steps)
        assert rem == 0 and slab % (2 * SUBLANES) == 0
        args.append(w.reshape(n_layers, n_exp * rows, cols))
        in_specs.append(pl.BlockSpec((1, slab, cols), lambda i: (layer, i, 0)))
        out_shapes.append(jax.ShapeDtypeStruct((n_exp * rows, cols), jnp.bfloat16))
        out_specs.append(pl.BlockSpec((slab, cols), lambda i: (i, 0)))
    return args, in_specs, out_shapes, out_specs


def _conv_mixer_kernel(x_ref, w_in_ref, w_conv_ref, w_out_ref, ln_g_ref, ln_b_ref,
                       rw_hi_ref, rw_lo_ref, rb_ref, wg32_ref, wu32_ref, wd32_ref,
                       out_ref, meta_t_ref, count_ref, wg16_ref, wu16_ref, wd16_ref, carry_ref, *, tiles_per_seq):
    i = pl.program_id(0)

    @pl.when(i == 0)
    def _():
        count_ref[...] = jnp.zeros_like(count_ref)

    @pl.when(i % tiles_per_seq == 0)
    def _():
        carry_ref[...] = jnp.zeros_like(carry_ref)

    f32 = jnp.float32
    d = D_MODEL
    ts = x_ref.shape[0]
    sub = ts // MIX_SUBTILES
    xs = [x_ref[k * sub:(k + 1) * sub, :] for k in range(MIX_SUBTILES)]
    projs = [jnp.dot(x.astype(jnp.bfloat16), w_in_ref[...], preferred_element_type=f32) for x in xs]
    row = lax.broadcasted_iota(jnp.int32, (sub, d), 0)
    wc = w_conv_ref[...]
    prev = carry_ref[...]
    gated = []
    for proj in projs:
        gate_b, gate_c, h = proj[:, :d], proj[:, d:2 * d], proj[:, 2 * d:]
        u = gate_c * h
        u1 = jnp.where(row == 0, prev[SUBLANES - 1:SUBLANES, :], pltpu.roll(u, 1, axis=0))
        u2 = jnp.where(row == 0, prev[SUBLANES - 2:SUBLANES - 1, :],
                       jnp.where(row == 1, prev[SUBLANES - 1:SUBLANES, :], pltpu.roll(u, 2, axis=0)))
        prev = u[sub - SUBLANES:, :]
        conv = wc[0:1, :] * u2 + wc[1:2, :] * u1 + wc[2:3, :] * u
        gated.append((gate_b * conv).astype(jnp.bfloat16))
    carry_ref[...] = prev
    mixes = [jnp.dot(v, w_out_ref[...], preferred_element_type=f32) for v in gated]
    x1 = jnp.concatenate([_layer_norm(ALPHA * x + mix, ln_g_ref[...], ln_b_ref[...])
                          for x, mix in zip(xs, mixes)], axis=0)
    meta = _route_and_rank(x1, rw_hi_ref[...], rw_lo_ref[...], rb_ref[...], count_ref)
    out_ref[:, :d] = x1
    out_ref[:, d:] = jnp.concatenate([meta, jnp.zeros((LANES - SUBLANES, meta.shape[1]), meta.dtype)], axis=0).T
    meta_t_ref[...] = meta
    _cast_slabs((wg32_ref, wu32_ref, wd32_ref), (wg16_ref, wu16_ref, wd16_ref))


def _const_spec(shape):
    return pl.BlockSpec(shape, lambda i: (0,) * len(shape))


def _pad_rows(a, rows=SUBLANES):
    return jnp.pad(a, ((0, rows - a.shape[0]), (0, 0)))


def _prep_router(router_w, router_b):
    order = np.array([g * EXPERTS_PER_GROUP + j for j in range(EXPERTS_PER_GROUP) for g in range(N_GROUPS)])
    rw = jnp.pad(router_w.astype(jnp.float32)[:, order], ((0, 0), (0, LANES - N_EXPERTS)))
    rb = router_b.astype(jnp.float32)[order][:, None]
    return _split_bf16(rw), rb


def _conv_mixer_layer(x2d, seq, w_in, w_conv, w_out, ln_g, ln_b, rw_hi, rw_lo, rb, layer, expert_weights):
    t, d = x2d.shape
    ts = MIX_TILE
    grid = (t // ts,)
    cast_args, cast_in, cast_shapes, cast_out = _cast_slab_specs(layer, grid[0], expert_weights)
    return pl.pallas_call(
        functools.partial(_conv_mixer_kernel, tiles_per_seq=seq // ts),
        out_shape=(jax.ShapeDtypeStruct((t, ROW_W), jnp.float32),
                   jax.ShapeDtypeStruct((SUBLANES, t), jnp.float32),
                   jax.ShapeDtypeStruct((CLASS_ROWS, LANES), jnp.float32), *cast_shapes),
        grid=grid,
        in_specs=[pl.BlockSpec((ts, d), lambda i: (i, 0)),
                  _const_spec((d, 3 * d)), _const_spec((SUBLANES, d)), _const_spec((d, d)),
                  _const_spec((1, d)), _const_spec((1, d)),
                  _const_spec((d, LANES)), _const_spec((d, LANES)), _const_spec((N_EXPERTS, 1)), *cast_in],
        out_specs=(pl.BlockSpec((ts, ROW_W), lambda i: (i, 0)),
                   pl.BlockSpec((SUBLANES, ts), lambda i: (0, i)),
                   _const_spec((CLASS_ROWS, LANES)), *cast_out),
        scratch_shapes=[pltpu.VMEM((SUBLANES, d), jnp.float32)],
        compiler_params=pltpu.CompilerParams(dimension_semantics=("arbitrary",),
                                             vmem_limit_bytes=VMEM_LIMIT_BYTES),
        name="conv_mixer_route",
    )(x2d, w_in, w_conv, w_out, ln_g, ln_b, rw_hi, rw_lo, rb, *cast_args)


def _log_sigmoid(z):
    return jnp.minimum(z, 0.0) - jnp.log1p(jnp.exp(-jnp.abs(z)))


def _split3_f32(a):
    f32, bf16 = jnp.float32, jnp.bfloat16
    hi = a.astype(bf16).astype(f32)
    r1 = a - hi
    mid = r1.astype(bf16).astype(f32)
    lo = (r1 - mid).astype(bf16).astype(f32)
    return hi, mid, lo


_COL_U, _COL_ONE, _COL_W, _COL_EMT, _COL_END = 0, 24, 48, 64, 80


def _mlstm_mixer_kernel(x_ref, w_qvo_ref, wkt_ref, wgt_hi_ref, wgt_lo_ref, bg_row_ref, norm_g_ref, w_out_ref,
                        ln_g_ref, ln_b_ref, rw_hi_ref, rw_lo_ref, rb_ref, wg32_ref, wu32_ref, wd32_ref,
                        out_ref, meta_t_ref, count_ref, wg16_ref, wu16_ref, wd16_ref,
                        proj_ref, kt_ref, grow_ref, colmat_ref, bexp_ref, h_ref, c_ref, m_ref, *, tiles_per_seq):
    i = pl.program_id(0)
    f32, bf16 = jnp.float32, jnp.bfloat16
    nh, dk, dv, d = ML_HEADS, ML_QK_DIM, ML_V_DIM, D_MODEL
    L = ML_CHUNK
    ts = x_ref.shape[0]
    n_chunks = ts // L
    o_base = ML_QK_W + ML_V_W

    @pl.when(i == 0)
    def _():
        count_ref[...] = jnp.zeros_like(count_ref)

    @pl.when(i % tiles_per_seq == 0)
    def _():
        c_ref[...] = jnp.zeros_like(c_ref)
        m_ref[...] = jnp.zeros_like(m_ref)

    x = x_ref[...]
    x_hi, x_lo = _split_bf16(x)
    nt = (((1,), (1,)), ((), ()))
    proj_ref[...] = jnp.dot(x_hi, w_qvo_ref[...], preferred_element_type=f32)
    k_t = lax.dot_general(wkt_ref[...], x_hi, nt, preferred_element_type=f32)
    g_row = (lax.dot_general(wgt_hi_ref[...], x_hi, nt, preferred_element_type=f32)
             + lax.dot_general(wgt_lo_ref[...], x_hi, nt, preferred_element_type=f32)
             + lax.dot_general(wgt_hi_ref[...], x_lo, nt, preferred_element_type=f32)) + bg_row_ref[...]
    r_i = lax.broadcasted_iota(jnp.int32, (ts, ts), 0)
    c_i = lax.broadcasted_iota(jnp.int32, (ts, ts), 1)
    tri_row = jnp.where(((r_i // L) == (c_i // L)) & (r_i <= c_i), 1.0, 0.0).astype(bf16)
    lf_hi, lf_lo = _split_bf16(_log_sigmoid(g_row))
    cum_row = (jnp.dot(lf_hi, tri_row, preferred_element_type=f32)
               + jnp.dot(lf_lo, tri_row, preferred_element_type=f32))
    i_row, b_row = g_row[:nh, :], cum_row[nh:, :]

    g = i_row - b_row
    lane_in_chunk = lax.broadcasted_iota(jnp.int32, (nh, ts), 1) & (L - 1)
    cm = g
    shift = 1
    while shift < L:
        cm = jnp.maximum(cm, jnp.where(lane_in_chunk >= shift, pltpu.roll(cm, shift, axis=1), -jnp.inf))
        shift *= 2
    m_prev = m_ref[:, 0:1]
    u_parts, w_parts, gr_parts, gs_parts = [], [], [], []
    for c in range(n_chunks):
        sl = slice(c * L, (c + 1) * L)
        u_c = jnp.maximum(m_prev, cm[:, sl])
        b_last = b_row[:, (c + 1) * L - 1:(c + 1) * L]
        m_new = b_last + u_c[:, L - 1:L]
        u_parts.append(u_c)
        w_parts.append(jnp.exp(m_prev - u_c))
        gr_parts.append(jnp.exp(b_last - b_row[:, sl] + i_row[:, sl] - m_new))
        gs_parts.append(jnp.broadcast_to(jnp.exp(b_last + m_prev - m_new), (nh, L)))
        m_prev = m_new
    m_ref[...] = jnp.broadcast_to(m_prev, m_ref.shape)
    u = jnp.concatenate(u_parts, axis=1)
    w_inter = jnp.concatenate(w_parts, axis=1)
    emt = jnp.exp(-(b_row + u))
    w_hi = w_inter.astype(bf16).astype(f32)
    e_hi = emt.astype(bf16).astype(f32)
    col_rows = jnp.concatenate(
        [*_split3_f32(u), jnp.ones((_COL_W - _COL_ONE, ts), f32), w_hi, w_inter - w_hi, e_hi, emt - e_hi,
         jnp.zeros((LANES - _COL_END, ts), f32)], axis=0)
    colmat_ref[...] = col_rows.T.astype(bf16)
    b_exp = jnp.concatenate([jnp.full((_COL_ONE - _COL_U, ts), -1.0, f32), *_split3_f32(g),
                             jnp.zeros((LANES - _COL_W, ts), f32)], axis=0).astype(bf16)
    for c in range(n_chunks):
        sl = slice(c * L, (c + 1) * L)
        bexp_ref[c] = b_exp[:, sl]
        grow_ref[c] = jnp.concatenate([gr_parts[c], gs_parts[c]], axis=0)
        kt_ref[c] = k_t[:, sl]

    causal = lax.broadcasted_iota(jnp.int32, (L, L), 0) >= lax.broadcasted_iota(jnp.int32, (L, L), 1)
    ones_col = jnp.where(lax.broadcasted_iota(jnp.int32, (L, dv), 1) == 0, 1.0, 0.0).astype(bf16)
    lane_ll = lax.broadcasted_iota(jnp.int32, (L, LANES), 1)
    sel_r = lax.broadcasted_iota(jnp.int32, (LANES, 2 * LANES), 0)
    sel_c = lax.broadcasted_iota(jnp.int32, (LANES, 2 * LANES), 1)
    is_w_row = (sel_r >= _COL_W) & (sel_r < _COL_EMT)
    is_e_row = (sel_r >= _COL_EMT) & (sel_r < _COL_END)
    b_sel = jnp.where((is_w_row & (sel_c < LANES)) | (is_e_row & (sel_c >= LANES)), 1.0, 0.0).astype(bf16)
    r2_r = lax.broadcasted_iota(jnp.int32, (2 * dv, 2 * dv), 0)
    r2_c = lax.broadcasted_iota(jnp.int32, (2 * dv, 2 * dv), 1)
    rhs2 = jnp.where((r2_r < dv) & (r2_c < dv), 1.0 / dv,
                     jnp.where((r2_r == dv) & (r2_c >= dv), 1.0, 0.0)).astype(bf16)

    def chunk_body(c, carry):
        r0 = pl.multiple_of(c * L, L)
        gr = grow_ref[c]
        colmat = colmat_ref[pl.ds(r0, L), :]
        rhs1 = jnp.concatenate([bexp_ref[c], b_sel], axis=1)
        heads = range(nh)
        c_pairs = [c_ref[p] for p in range(nh // 2)]
        kt_pairs = [kt_ref[c, p * LANES:(p + 1) * LANES, :] for p in range(nh // 2)]

        ew, q_m, s_mat, v_ext = [], [], [], []
        for h in heads:
            p, hh = divmod(h, 2)
            lhs = jnp.where((lane_ll & (nh - 1)) == h, colmat, 0)
            ew.append(jnp.dot(lhs, rhs1, preferred_element_type=f32))
            q2 = proj_ref[pl.ds(r0, L), p * LANES:(p + 1) * LANES] * (dk ** -0.5)
            q_m.append(jnp.where((lane_ll >= hh * dk) & (lane_ll < (hh + 1) * dk), q2, 0.0))
            s_mat.append(jnp.dot(q_m[h].astype(bf16), kt_pairs[p].astype(bf16), preferred_element_type=f32))
            v = proj_ref[pl.ds(r0, L), ML_QK_W + h * dv:ML_QK_W + (h + 1) * dv].astype(bf16)
            v_ext.append(jnp.concatenate([v, ones_col], axis=1))
        tot = []
        for h in heads:
            p, hh = divmod(h, 2)
            a = jnp.exp(jnp.where(causal, ew[h][:, :L], -jnp.inf)) * s_mat[h]
            qw = q_m[h] * ew[h][:, L:L + LANES]
            lhs = jnp.concatenate([a.astype(bf16), qw.astype(bf16)], axis=1)
            rhs = jnp.concatenate([v_ext[h], c_pairs[p].astype(bf16)], axis=0)
            tot.append(jnp.dot(lhs, rhs, preferred_element_type=f32))
        for h in heads:
            num = tot[h][:, :dv]
            lhs2 = jnp.concatenate([(num * num).astype(bf16), tot[h][:, dv:].astype(bf16)], axis=1)
            r2 = jnp.dot(lhs2, rhs2, preferred_element_type=f32)
            inv = 1.0 / jnp.maximum(jnp.abs(r2[:, dv:]), ew[h][:, L + LANES:])
            hn = (num * inv) * lax.rsqrt(r2[:, :dv] * inv * inv + HEAD_NORM_EPS)
            o_pre = proj_ref[pl.ds(r0, L), o_base + h * dv:o_base + (h + 1) * dv]
            h_ref[pl.ds(r0, L), h * dv:(h + 1) * dv] = (
                jax.nn.sigmoid(o_pre) * hn * norm_g_ref[:, h * dv:(h + 1) * dv])
        for h in heads:
            p, hh = divmod(h, 2)
            rows = slice(hh * dk, (hh + 1) * dk)
            kg = (kt_pairs[p][rows, :] * gr[h:h + 1, :]).astype(bf16)
            gs = gr[nh + h:nh + h + 1, :]
            c_ref[p, rows, :] = (jnp.concatenate([gs, gs], axis=1) * c_pairs[p][rows, :]
                                 + jnp.dot(kg, v_ext[h], preferred_element_type=f32))
        return carry

    lax.fori_loop(0, n_chunks, chunk_body, 0)

    sub = ts // MIX_SUBTILES
    mixes = [jnp.dot(h_ref[k * sub:(k + 1) * sub, :].astype(bf16), w_out_ref[...], preferred_element_type=f32)
             for k in range(MIX_SUBTILES)]
    x1 = jnp.concatenate([_layer_norm(ALPHA * x[k * sub:(k + 1) * sub, :] + mixes[k], ln_g_ref[...], ln_b_ref[...])
                          for k in range(MIX_SUBTILES)], axis=0)
    meta = _route_and_rank(x1, rw_hi_ref[...], rw_lo_ref[...], rb_ref[...], count_ref)
    out_ref[:, :d] = x1
    out_ref[:, d:] = jnp.concatenate([meta, jnp.zeros((LANES - SUBLANES, meta.shape[1]), meta.dtype)], axis=0).T
    meta_t_ref[...] = meta
    _cast_slabs((wg32_ref, wu32_ref, wd32_ref), (wg16_ref, wu16_ref, wd16_ref))


def _mlstm_mixer_layer(x2d, seq, w_in, b_gate, norm_g, w_out, ln_g, ln_b, rw_hi, rw_lo, rb, layer,
                       expert_weights):
    t, d = x2d.shape
    ts = MIX_TILE
    nh = ML_HEADS
    cast_args, cast_in, cast_shapes, cast_out = _cast_slab_specs(layer, t // ts, expert_weights)
    assert ML_CHUNK == LANES and ts % ML_CHUNK == 0 and 2 * ML_QK_DIM == LANES and ML_V_DIM == LANES
    n_qkvo = 2 * ML_QK_W + 2 * ML_V_W
    n_qvo = ML_QK_W + 2 * ML_V_W
    w_qvo = jnp.concatenate([w_in[:, :ML_QK_W], w_in[:, 2 * ML_QK_W:n_qkvo]], axis=1).astype(jnp.bfloat16)
    wkt = w_in[:, ML_QK_W:2 * ML_QK_W].T.astype(jnp.bfloat16)
    w_g = w_in[:, n_qkvo:].astype(jnp.float32)
    wgt_hi, wgt_lo = _split_bf16(w_g.T)
    bg_row = b_gate.astype(jnp.float32)[:, None]
    return pl.pallas_call(
        functools.partial(_mlstm_mixer_kernel, tiles_per_seq=seq // ts),
        out_shape=(jax.ShapeDtypeStruct((t, ROW_W), jnp.float32),
                   jax.ShapeDtypeStruct((SUBLANES, t), jnp.float32),
                   jax.ShapeDtypeStruct((CLASS_ROWS, LANES), jnp.float32), *cast_shapes),
        grid=(t // ts,),
        in_specs=[pl.BlockSpec((ts, d), lambda i: (i, 0)),
                  _const_spec((d, n_qvo)), _const_spec((ML_QK_W, d)),
                  _const_spec((2 * nh, d)), _const_spec((2 * nh, d)),
                  _const_spec((2 * nh, 1)),
                  _const_spec((1, d)), _const_spec((d, d)),
                  _const_spec((1, d)), _const_spec((1, d)),
                  _const_spec((d, LANES)), _const_spec((d, LANES)), _const_spec((N_EXPERTS, 1)), *cast_in],
        out_specs=(pl.BlockSpec((ts, ROW_W), lambda i: (i, 0)),
                   pl.BlockSpec((SUBLANES, ts), lambda i: (0, i)),
                   _const_spec((CLASS_ROWS, LANES)), *cast_out),
        scratch_shapes=[pltpu.VMEM((ts, n_qvo), jnp.float32),
                        pltpu.VMEM((ts // ML_CHUNK, ML_QK_W, ML_CHUNK), jnp.float32),
                        pltpu.VMEM((ts // ML_CHUNK, 2 * nh, ML_CHUNK), jnp.float32),
                        pltpu.VMEM((ts, LANES), jnp.bfloat16),
                        pltpu.VMEM((ts // ML_CHUNK, LANES, ML_CHUNK), jnp.bfloat16),
                        pltpu.VMEM((ts, d), jnp.float32),
                        pltpu.VMEM((nh // 2, 2 * ML_QK_DIM, 2 * ML_V_DIM), jnp.float32),
                        pltpu.VMEM((nh, ML_CHUNK), jnp.float32)],
        compiler_params=pltpu.CompilerParams(dimension_semantics=("arbitrary",),
                                             vmem_limit_bytes=VMEM_LIMIT_BYTES),
        name="mlstm_mixer_route",
    )(x2d, w_qvo, wkt, wgt_hi, wgt_lo, bg_row, norm_g[None, :], w_out.astype(jnp.bfloat16),
      ln_g, ln_b, rw_hi, rw_lo, rb, *cast_args)


def _expert_kernel(tok_ref, elo_ref, ehi_ref, nvalid_ref,
                   x_hbm, wg_lo_ref, wu_lo_ref, wd_lo_ref, wg_hi_ref, wu_hi_ref, wd_hi_ref,
                   ln_g_ref, ln_b_ref, out_hbm, xbuf0, xbuf1, obuf0, obuf1, gather_sem, scatter_sem):
    i = pl.program_id(0)
    nb = pl.num_programs(0)
    bm = xbuf0.shape[0]
    d = D_MODEL
    xbufs, obufs = (xbuf0, xbuf1), (obuf0, obuf1)
    nv = nvalid_ref[i]
    nv_prev = nvalid_ref[jnp.maximum(i - 1, 0)]

    def start_gather(j, s):
        for r in range(bm):
            tok = tok_ref[j * bm + r]
            pltpu.make_async_copy(x_hbm.at[pl.ds(tok, 1)], xbufs[s].at[pl.ds(r, 1)], gather_sem.at[s]).start()

    def wait_gather(s):
        pltpu.make_async_copy(xbufs[s], xbufs[s], gather_sem.at[s]).wait()

    def scatter_copy(j, s, r):
        tok = tok_ref[j * bm + r]
        return pltpu.make_async_copy(obufs[s].at[pl.ds(r, 1)], out_hbm.at[pl.ds(tok, 1)], scatter_sem.at[s])

    def wait_scatter(s, n):
        @pl.when(n == bm)
        def _():
            pltpu.make_async_copy(obufs[s], obufs[s], scatter_sem.at[s]).wait()

        @pl.when(n < bm)
        def _():
            def body(r, c):
                pltpu.make_async_copy(obufs[s].at[pl.ds(0, 1)], obufs[s].at[pl.ds(0, 1)], scatter_sem.at[s]).wait()
                return c
            lax.fori_loop(0, n, body, 0)

    @pl.when(i == 0)
    def _():
        start_gather(0, 0)

    def step(s):
        @pl.when((i == 0) | (nv_prev > 0))
        def _():
            wait_gather(s)

        @pl.when(nv > 0)
        def _():
            start_gather(i + 1, 1 - s)
            xb = xbufs[s][...]
            x = xb[:, :d]
            g_lo = xb[:, d + META_G_LO:d + META_G_LO + 1]
            g_hi = xb[:, d + META_G_HI:d + META_G_HI + 1]
            x16 = x.astype(jnp.bfloat16)

            def ffn(wg_ref, wu_ref, wd_ref):
                g = jnp.dot(x16, wg_ref[0], preferred_element_type=jnp.float32)
                u = jnp.dot(x16, wu_ref[0], preferred_element_type=jnp.float32)
                h = (g * jax.nn.sigmoid(g)) * u
                return jnp.dot(h.astype(jnp.bfloat16), wd_ref[0], preferred_element_type=jnp.float32)

            y = g_lo * ffn(wg_lo_ref, wu_lo_ref, wd_lo_ref) + g_hi * ffn(wg_hi_ref, wu_hi_ref, wd_hi_ref)
            obufs[s][...] = _layer_norm(ALPHA * x + y, ln_g_ref[...], ln_b_ref[...])

            @pl.when(nv == bm)
            def _():
                for r in range(bm):
                    scatter_copy(i, s, r).start()

            @pl.when(nv < bm)
            def _():
                def body(r, c):
                    scatter_copy(i, s, r).start()
                    return c
                lax.fori_loop(0, nv, body, 0)

        @pl.when(i > 0)
        def _():
            wait_scatter(1 - s, nv_prev)

        @pl.when(i == nb - 1)
        def _():
            wait_scatter(s, nv)

            @pl.when(nv > 0)
            def _():
                wait_gather(1 - s)

    for s in range(2):
        pl.when(i % 2 == s)(functools.partial(step, s))


def _moe_layer(xext, meta_t, counts, w_gate, w_up, w_down, ln_g, ln_b):
    t = xext.shape[0]
    d, f, bm = D_MODEL, D_EXPERT, EXPERT_BLOCK
    n_blocks = t // bm + N_CLASSES
    n_rows = n_blocks * bm

    cls = meta_t[META_CLASS].astype(jnp.int32)
    rank = meta_t[META_RANK].astype(jnp.int32)
    cnt = counts[:N_CLASSES, 0].astype(jnp.int32)
    cls_blocks = (cnt + bm - 1) // bm
    blk_end = jnp.cumsum(cls_blocks)
    blk_start = blk_end - cls_blocks
    total_blocks = blk_end[-1]
    row_start = jnp.sum(jnp.where(cls[None, :] == jnp.arange(N_CLASSES, dtype=jnp.int32)[:, None],
                                  blk_start[:, None] * bm, 0), axis=0)
    dest = row_start + rank
    tok_of_row = jnp.zeros((n_rows + bm,), jnp.int32).at[dest].set(
        jnp.arange(t, dtype=jnp.int32), unique_indices=True)
    blk = jnp.arange(n_blocks, dtype=jnp.int32)
    blk_cls = jnp.sum(jnp.minimum(blk, total_blocks - 1)[:, None] >= blk_end[None, :], axis=1).astype(jnp.int32)
    blk_cls = jnp.minimum(blk_cls, N_CLASSES - 1)
    nvalid = jnp.clip(cnt[blk_cls] - (blk - blk_start[blk_cls]) * bm, 0, bm)
    nvalid = jnp.where(blk < total_blocks, nvalid, 0).astype(jnp.int32)
    e_lo = jnp.asarray(CLASS_E_LO)[blk_cls]
    e_hi = jnp.asarray(CLASS_E_HI)[blk_cls]

    w_gate, w_up, w_down = (w_gate.reshape(N_EXPERTS, d, f), w_up.reshape(N_EXPERTS, d, f),
                            w_down.reshape(N_EXPERTS, f, d))
    w_lo = lambda shape: pl.BlockSpec((1,) + shape, lambda i, tok, elo, ehi, nv: (elo[i], 0, 0))
    w_hi = lambda shape: pl.BlockSpec((1,) + shape, lambda i, tok, elo, ehi, nv: (ehi[i], 0, 0))
    vec = pl.BlockSpec((1, d), lambda i, tok, elo, ehi, nv: (0, 0))
    grid_spec = pltpu.PrefetchScalarGridSpec(
        num_scalar_prefetch=4,
        grid=(n_blocks,),
        in_specs=[pl.BlockSpec(memory_space=pl.ANY),
                  w_lo((d, f)), w_lo((d, f)), w_lo((f, d)),
                  w_hi((d, f)), w_hi((d, f)), w_hi((f, d)),
                  vec, vec],
        out_specs=pl.BlockSpec(memory_space=pl.ANY),
        scratch_shapes=[pltpu.VMEM((bm, ROW_W), jnp.float32), pltpu.VMEM((bm, ROW_W), jnp.float32),
                        pltpu.VMEM((bm, d), jnp.float32), pltpu.VMEM((bm, d), jnp.float32),
                        pltpu.SemaphoreType.DMA((2,)),
                        pltpu.SemaphoreType.DMA((2,))],
    )
    return pl.pallas_call(
        _expert_kernel,
        out_shape=jax.ShapeDtypeStruct((t, d), jnp.float32),
        grid_spec=grid_spec,
        compiler_params=pltpu.CompilerParams(dimension_semantics=("arbitrary",),
                                             vmem_limit_bytes=VMEM_LIMIT_BYTES),
        name="expert_pair_ffn",
    )(tok_of_row, e_lo, e_hi, nvalid, xext, w_gate, w_up, w_down, w_gate, w_up, w_down, ln_g, ln_b)


def kernel(x, conv_w_in, conv_w, conv_w_out, ml_w_in, ml_b_gate, ml_norm_g, ml_w_out, ln_mix_g, ln_mix_b,
           ln_ffn_g, ln_ffn_b, router_w, router_b, exp_w_gate, exp_w_up, exp_w_down):
    bsz, seq, d = x.shape
    assert d == D_MODEL and seq % MIX_TILE == 0 and (bsz * seq) % EXPERT_BLOCK == 0
    bf16 = jnp.bfloat16
    vec = lambda a: a.astype(jnp.float32)[None, :]
    (rw_hi, rw_lo), rb = _prep_router(router_w, router_b)
    x2d = x.reshape(bsz * seq, d).astype(jnp.float32)
    expert_weights = (exp_w_gate.astype(jnp.float32), exp_w_up.astype(jnp.float32),
                      exp_w_down.astype(jnp.float32))
    for i in range(DEPTH):
        j = i // 2
        if i % 2 == 0:
            xext, meta_t, counts, *w16 = _conv_mixer_layer(
                x2d, seq, conv_w_in[j].astype(bf16), _pad_rows(conv_w[j].astype(jnp.float32)),
                conv_w_out[j].astype(bf16), vec(ln_mix_g[i]), vec(ln_mix_b[i]), rw_hi, rw_lo, rb,
                i, expert_weights)
        else:
            xext, meta_t, counts, *w16 = _mlstm_mixer_layer(
                x2d, seq, ml_w_in[j], ml_b_gate[j], ml_norm_g[j].astype(jnp.float32), ml_w_out[j],
                vec(ln_mix_g[i]), vec(ln_mix_b[i]), rw_hi, rw_lo, rb, i, expert_weights)
        x2d = _moe_layer(xext, meta_t, counts, *w16, vec(ln_ffn_g[i]), vec(ln_ffn_b[i]))
    return x2d.reshape(bsz, seq, d).astype(x.dtype)
```

```python
import functools
import itertools

import numpy as np
import jax
import jax.numpy as jnp
from jax import lax
from jax.experimental import pallas as pl
from jax.experimental.pallas import tpu as pltpu

D_MODEL = 1024
DEPTH = 2
CONV_WIDTH = 3
ML_HEADS = 8
ML_QK_DIM = D_MODEL // (2 * ML_HEADS)
ML_V_DIM = D_MODEL // ML_HEADS
ML_QK_W = ML_HEADS * ML_QK_DIM
ML_V_W = ML_HEADS * ML_V_DIM
N_EXPERTS = 16
N_GROUPS = 4
EXPERTS_PER_GROUP = N_EXPERTS // N_GROUPS
D_EXPERT = 3 * D_MODEL // 2
ALPHA = (2 * DEPTH) ** 0.25
LN_EPS = 1e-5
HEAD_NORM_EPS = 1e-6

LANES = 128
SUBLANES = 8
VMEM_LIMIT_BYTES = 56 * 1024 * 1024

PAIRS = tuple(itertools.combinations(range(EXPERTS_PER_GROUP), 2))
N_PAIRS = len(PAIRS)
N_CLASSES = N_GROUPS * N_PAIRS
CLASS_ROWS = -(-N_CLASSES // 16) * 16
CLASS_E_LO = np.array([g * EXPERTS_PER_GROUP + a for g in range(N_GROUPS) for a, _ in PAIRS], np.int32)
CLASS_E_HI = np.array([g * EXPERTS_PER_GROUP + b for g in range(N_GROUPS) for _, b in PAIRS], np.int32)

META_G_LO, META_G_HI, META_CLASS, META_RANK = 0, 1, 2, 3
ROW_W = D_MODEL + LANES

MIX_TILE = 512
MIX_SUBTILES = 2
EXPERT_BLOCK = 256
GATE_CHUNKS = 6
ML_CHUNK = 128


def _layer_norm(z, g, b):
    mu = jnp.mean(z, axis=-1, keepdims=True)
    zc = z - mu
    var = jnp.mean(zc * zc, axis=-1, keepdims=True)
    return zc * lax.rsqrt(var + LN_EPS) * g + b


def _split_bf16(a):
    hi = a.astype(jnp.bfloat16)
    lo = (a - hi.astype(jnp.float32)).astype(jnp.bfloat16)
    return hi, lo


def _route_and_rank(x1, rw_hi, rw_lo, rb_col, count_ref):
    ts = x1.shape[0]
    f32, bf16 = jnp.float32, jnp.bfloat16
    n_g, n_m = N_GROUPS, EXPERTS_PER_GROUP

    x_hi, x_lo = _split_bf16(x1)
    logits = (jnp.dot(x_hi, rw_hi, preferred_element_type=f32)
              + jnp.dot(x_hi, rw_lo, preferred_element_type=f32)
              + jnp.dot(x_lo, rw_hi, preferred_element_type=f32))
    lt = logits.T[:N_EXPERTS, :]
    ex = jnp.exp(lt - jnp.max(lt, axis=0, keepdims=True))
    probs = ex / jnp.sum(ex, axis=0, keepdims=True)
    sel = probs + rb_col
    member = lambda a, j: a[j * n_g:(j + 1) * n_g, :]

    in_top2 = []
    for j in range(n_m):
        beaten_by = jnp.zeros((n_g, ts), jnp.int32)
        for k in range(n_m):
            if k != j:
                wins = member(sel, k) > member(sel, j)
                if k < j:
                    wins = wins | (member(sel, k) == member(sel, j))
                beaten_by = beaten_by + wins.astype(jnp.int32)
        in_top2.append(beaten_by < 2)
    score = sum(jnp.where(in_top2[j], member(sel, j), 0.0) for j in range(n_m))
    group = lax.broadcasted_iota(jnp.int32, (n_g, ts), 0)
    best = jnp.max(score, axis=0, keepdims=True)
    g_best = jnp.min(jnp.where(score == best, group, n_g), axis=0, keepdims=True)
    in_best = group == g_best
    picked = [jnp.max(jnp.where(in_top2[j] & in_best, 1, 0), axis=0, keepdims=True) for j in range(n_m)]
    p_pick = [jnp.sum(jnp.where(in_top2[j] & in_best, member(probs, j), 0.0), axis=0, keepdims=True)
              for j in range(n_m)]
    a = functools.reduce(jnp.minimum, [jnp.where(picked[j] > 0, j, n_m) for j in range(n_m)])
    b = functools.reduce(jnp.maximum, [jnp.where(picked[j] > 0, j, -1) for j in range(n_m)])
    p_lo = sum(jnp.where(a == j, p_pick[j], 0.0) for j in range(n_m))
    p_hi = sum(jnp.where(b == j, p_pick[j], 0.0) for j in range(n_m))
    g_lo = p_lo / (p_lo + p_hi)
    g_hi = p_hi / (p_lo + p_hi)
    pair = ((a * (2 * n_m - 1 - a)) >> 1) + (b - a - 1)
    cls = g_best * N_PAIRS + pair

    n_rows = count_ref.shape[0]
    onehot = jnp.where(lax.broadcasted_iota(jnp.int32, (n_rows, ts), 0) == cls, 1.0, 0.0)
    earlier = jnp.where(lax.broadcasted_iota(jnp.int32, (ts, ts), 0)
                        < lax.broadcasted_iota(jnp.int32, (ts, ts), 1), 1.0, 0.0).astype(bf16)
    before = jnp.dot(onehot.astype(bf16), earlier, preferred_element_type=f32)
    running = count_ref[...]
    running_ts = jnp.concatenate([running] * (ts // LANES), axis=1)
    rank = jnp.sum(onehot * (before + running_ts), axis=0, keepdims=True)
    count_ref[...] = running + jnp.dot(onehot.astype(bf16), jnp.ones((ts, LANES), bf16),
                                       preferred_element_type=f32)
    return jnp.concatenate([g_lo, g_hi, cls.astype(f32), rank, jnp.zeros((SUBLANES - 4, ts), f32)], axis=0)


def _cast_slabs(w32_refs, w16_refs):
    for w32_ref, w16_ref in zip(w32_refs, w16_refs):
        w16_ref[...] = w32_ref[0].astype(w16_ref.dtype)


def _cast_slab_specs(layer, n_steps, weights):
    args, in_specs, out_shapes, out_specs = [], [], [], []
    for w in weights:
        n_layers, n_exp, rows, cols = w.shape
        slab, rem = divmod(n_exp * rows, n_steps)
        assert rem == 0 and slab % (2 * SUBLANES) == 0
        args.append(w.reshape(n_layers, n_exp * rows, cols))
        in_specs.append(pl.BlockSpec((1, slab, cols), lambda i: (layer, i, 0)))
        out_shapes.append(jax.ShapeDtypeStruct((n_exp * rows, cols), jnp.bfloat16))
        out_specs.append(pl.BlockSpec((slab, cols), lambda i: (i, 0)))
    return args, in_specs, out_shapes, out_specs


def _conv_mixer_kernel(x_ref, w_in_ref, w_conv_ref, w_out_ref, ln_g_ref, ln_b_ref,
                       rw_hi_ref, rw_lo_ref, rb_ref, wg32_ref, wu32_ref, wd32_ref,
                       out_ref, meta_t_ref, count_ref, wg16_ref, wu16_ref, wd16_ref, carry_ref, *, tiles_per_seq):
    i = pl.program_id(0)

    @pl.when(i == 0)
    def _():
        count_ref[...] = jnp.zeros_like(count_ref)

    @pl.when(i % tiles_per_seq == 0)
    def _():
        carry_ref[...] = jnp.zeros_like(carry_ref)

    f32 = jnp.float32
    d = D_MODEL
    ts = x_ref.shape[0]
    sub = ts // MIX_SUBTILES
    xs = [x_ref[k * sub:(k + 1) * sub, :] for k in range(MIX_SUBTILES)]
    projs = [jnp.dot(x.astype(jnp.bfloat16), w_in_ref[...], preferred_element_type=f32) for x in xs]
    row = lax.broadcasted_iota(jnp.int32, (sub, d), 0)
    wc = w_conv_ref[...]
    prev = carry_ref[...]
    gated = []
    for proj in projs:
        gate_b, gate_c, h = proj[:, :d], proj[:, d:2 * d], proj[:, 2 * d:]
        u = gate_c * h
        u1 = jnp.where(row == 0, prev[SUBLANES - 1:SUBLANES, :], pltpu.roll(u, 1, axis=0))
        u2 = jnp.where(row == 0, prev[SUBLANES - 2:SUBLANES - 1, :],
                       jnp.where(row == 1, prev[SUBLANES - 1:SUBLANES, :], pltpu.roll(u, 2, axis=0)))
        prev = u[sub - SUBLANES:, :]
        conv = wc[0:1, :] * u2 + wc[1:2, :] * u1 + wc[2:3, :] * u
        gated.append((gate_b * conv).astype(jnp.bfloat16))
    carry_ref[...] = prev
    mixes = [jnp.dot(v, w_out_ref[...], preferred_element_type=f32) for v in gated]
    x1 = jnp.concatenate([_layer_norm(ALPHA * x + mix, ln_g_ref[...], ln_b_ref[...])
                          for x, mix in zip(xs, mixes)], axis=0)
    meta = _route_and_rank(x1, rw_hi_ref[...], rw_lo_ref[...], rb_ref[...], count_ref)
    out_ref[:, :d] = x1
    out_ref[:, d:] = jnp.concatenate([meta, jnp.zeros((LANES - SUBLANES, meta.shape[1]), meta.dtype)], axis=0).T
    meta_t_ref[...] = meta
    _cast_slabs((wg32_ref, wu32_ref, wd32_ref), (wg16_ref, wu16_ref, wd16_ref))


def _const_spec(shape):
    return pl.BlockSpec(shape, lambda i: (0,) * len(shape))


def _pad_rows(a, rows=SUBLANES):
    return jnp.pad(a, ((0, rows - a.shape[0]), (0, 0)))


def _prep_router(router_w, router_b):
    order = np.array([g * EXPERTS_PER_GROUP + j for j in range(EXPERTS_PER_GROUP) for g in range(N_GROUPS)])
    rw = jnp.pad(router_w.astype(jnp.float32)[:, order], ((0, 0), (0, LANES - N_EXPERTS)))
    rb = router_b.astype(jnp.float32)[order][:, None]
    return _split_bf16(rw), rb


def _conv_mixer_layer(x2d, seq, w_in, w_conv, w_out, ln_g, ln_b, rw_hi, rw_lo, rb, layer, expert_weights):
    t, d = x2d.shape
    ts = MIX_TILE
    grid = (t // ts,)
    cast_args, cast_in, cast_shapes, cast_out = _cast_slab_specs(layer, grid[0], expert_weights)
    return pl.pallas_call(
        functools.partial(_conv_mixer_kernel, tiles_per_seq=seq // ts),
        out_shape=(jax.ShapeDtypeStruct((t, ROW_W), jnp.float32),
                   jax.ShapeDtypeStruct((SUBLANES, t), jnp.float32),
                   jax.ShapeDtypeStruct((CLASS_ROWS, LANES), jnp.float32), *cast_shapes),
        grid=grid,
        in_specs=[pl.BlockSpec((ts, d), lambda i: (i, 0)),
                  _const_spec((d, 3 * d)), _const_spec((SUBLANES, d)), _const_spec((d, d)),
                  _const_spec((1, d)), _const_spec((1, d)),
                  _const_spec((d, LANES)), _const_spec((d, LANES)), _const_spec((N_EXPERTS, 1)), *cast_in],
        out_specs=(pl.BlockSpec((ts, ROW_W), lambda i: (i, 0)),
                   pl.BlockSpec((SUBLANES, ts), lambda i: (0, i)),
                   _const_spec((CLASS_ROWS, LANES)), *cast_out),
        scratch_shapes=[pltpu.VMEM((SUBLANES, d), jnp.float32)],
        compiler_params=pltpu.CompilerParams(dimension_semantics=("arbitrary",),
                                             vmem_limit_bytes=VMEM_LIMIT_BYTES),
        name="conv_mixer_route",
    )(x2d, w_in, w_conv, w_out, ln_g, ln_b, rw_hi, rw_lo, rb, *cast_args)


def _log_sigmoid(z):
    return jnp.minimum(z, 0.0) - jnp.log1p(jnp.exp(-jnp.abs(z)))


def _split3_f32(a):
    f32, bf16 = jnp.float32, jnp.bfloat16
    hi = a.astype(bf16).astype(f32)
    r1 = a - hi
    mid = r1.astype(bf16).astype(f32)
    lo = (r1 - mid).astype(bf16).astype(f32)
    return hi, mid, lo


_COL_U, _COL_ONE, _COL_W, _COL_EMT, _COL_END = 0, 24, 48, 64, 80


def _mlstm_mixer_kernel(x_ref, w_qvo_ref, wkt_ref, wgt_hi_ref, wgt_lo_ref, bg_row_ref, norm_g_ref, w_out_ref,
                        ln_g_ref, ln_b_ref, rw_hi_ref, rw_lo_ref, rb_ref, wg32_ref, wu32_ref, wd32_ref,
                        out_ref, meta_t_ref, count_ref, wg16_ref, wu16_ref, wd16_ref,
                        proj_ref, kt_ref, grow_ref, colmat_ref, bexp_ref, h_ref, c_ref, m_ref, *, tiles_per_seq):
    i = pl.program_id(0)
    f32, bf16 = jnp.float32, jnp.bfloat16
    nh, dk, dv, d = ML_HEADS, ML_QK_DIM, ML_V_DIM, D_MODEL
    L = ML_CHUNK
    ts = x_ref.shape[0]
    n_chunks = ts // L
    o_base = ML_QK_W + ML_V_W

    @pl.when(i == 0)
    def _():
        count_ref[...] = jnp.zeros_like(count_ref)

    @pl.when(i % tiles_per_seq == 0)
    def _():
        c_ref[...] = jnp.zeros_like(c_ref)
        m_ref[...] = jnp.zeros_like(m_ref)

    x = x_ref[...]
    x_hi, x_lo = _split_bf16(x)
    nt = (((1,), (1,)), ((), ()))
    proj_ref[...] = jnp.dot(x_hi, w_qvo_ref[...], preferred_element_type=f32)
    k_t = lax.dot_general(wkt_ref[...], x_hi, nt, preferred_element_type=f32)
    g_row = (lax.dot_general(wgt_hi_ref[...], x_hi, nt, preferred_element_type=f32)
             + lax.dot_general(wgt_lo_ref[...], x_hi, nt, preferred_element_type=f32)
             + lax.dot_general(wgt_hi_ref[...], x_lo, nt, preferred_element_type=f32)) + bg_row_ref[...]
    r_i = lax.broadcasted_iota(jnp.int32, (ts, ts), 0)
    c_i = lax.broadcasted_iota(jnp.int32, (ts, ts), 1)
    tri_row = jnp.where(((r_i // L) == (c_i // L)) & (r_i <= c_i), 1.0, 0.0).astype(bf16)
    lf_hi, lf_lo = _split_bf16(_log_sigmoid(g_row))
    cum_row = (jnp.dot(lf_hi, tri_row, preferred_element_type=f32)
               + jnp.dot(lf_lo, tri_row, preferred_element_type=f32))
    i_row, b_row = g_row[:nh, :], cum_row[nh:, :]

    g = i_row - b_row
    lane_in_chunk = lax.broadcasted_iota(jnp.int32, (nh, ts), 1) & (L - 1)
    cm = g
    shift = 1
    while shift < L:
        cm = jnp.maximum(cm, jnp.where(lane_in_chunk >= shift, pltpu.roll(cm, shift, axis=1), -jnp.inf))
        shift *= 2
    m_prev = m_ref[:, 0:1]
    u_parts, w_parts, gr_parts, gs_parts = [], [], [], []
    for c in range(n_chunks):
        sl = slice(c * L, (c + 1) * L)
        u_c = jnp.maximum(m_prev, cm[:, sl])
        b_last = b_row[:, (c + 1) * L - 1:(c + 1) * L]
        m_new = b_last + u_c[:, L - 1:L]
        u_parts.append(u_c)
        w_parts.append(jnp.exp(m_prev - u_c))
        gr_parts.append(jnp.exp(b_last - b_row[:, sl] + i_row[:, sl] - m_new))
        gs_parts.append(jnp.broadcast_to(jnp.exp(b_last + m_prev - m_new), (nh, L)))
        m_prev = m_new
    m_ref[...] = jnp.broadcast_to(m_prev, m_ref.shape)
    u = jnp.concatenate(u_parts, axis=1)
    w_inter = jnp.concatenate(w_parts, axis=1)
    emt = jnp.exp(-(b_row + u))
    w_hi = w_inter.astype(bf16).astype(f32)
    e_hi = emt.astype(bf16).astype(f32)
    col_rows = jnp.concatenate(
        [*_split3_f32(u), jnp.ones((_COL_W - _COL_ONE, ts), f32), w_hi, w_inter - w_hi, e_hi, emt - e_hi,
         jnp.zeros((LANES - _COL_END, ts), f32)], axis=0)
    colmat_ref[...] = col_rows.T.astype(bf16)
    b_exp = jnp.concatenate([jnp.full((_COL_ONE - _COL_U, ts), -1.0, f32), *_split3_f32(g),
                             jnp.zeros((LANES - _COL_W, ts), f32)], axis=0).astype(bf16)
    for c in range(n_chunks):
        sl = slice(c * L, (c + 1) * L)
        bexp_ref[c] = b_exp[:, sl]
        grow_ref[c] = jnp.concatenate([gr_parts[c], gs_parts[c]], axis=0)
        kt_ref[c] = k_t[:, sl]

    causal = lax.broadcasted_iota(jnp.int32, (L, L), 0) >= lax.broadcasted_iota(jnp.int32, (L, L), 1)
    ones_col = jnp.where(lax.broadcasted_iota(jnp.int32, (L, dv), 1) == 0, 1.0, 0.0).astype(bf16)
    lane_ll = lax.broadcasted_iota(jnp.int32, (L, LANES), 1)
    sel_r = lax.broadcasted_iota(jnp.int32, (LANES, 2 * LANES), 0)
    sel_c = lax.broadcasted_iota(jnp.int32, (LANES, 2 * LANES), 1)
    is_w_row = (sel_r >= _COL_W) & (sel_r < _COL_EMT)
    is_e_row = (sel_r >= _COL_EMT) & (sel_r < _COL_END)
    b_sel = jnp.where((is_w_row & (sel_c < LANES)) | (is_e_row & (sel_c >= LANES)), 1.0, 0.0).astype(bf16)
    r2_r = lax.broadcasted_iota(jnp.int32, (2 * dv, 2 * dv), 0)
    r2_c = lax.broadcasted_iota(jnp.int32, (2 * dv, 2 * dv), 1)
    rhs2 = jnp.where((r2_r < dv) & (r2_c < dv), 1.0 / dv,
                     jnp.where((r2_r == dv) & (r2_c >= dv), 1.0, 0.0)).astype(bf16)

    def chunk_body(c, carry):
        r0 = pl.multiple_of(c * L, L)
        gr = grow_ref[c]
        colmat = colmat_ref[pl.ds(r0, L), :]
        rhs1 = jnp.concatenate([bexp_ref[c], b_sel], axis=1)
        heads = range(nh)
        c_pairs = [c_ref[p] for p in range(nh // 2)]
        kt_pairs = [kt_ref[c, p * LANES:(p + 1) * LANES, :] for p in range(nh // 2)]

        ew, q_m, s_mat, v_ext = [], [], [], []
        for h in heads:
            p, hh = divmod(h, 2)
            lhs = jnp.where((lane_ll & (nh - 1)) == h, colmat, 0)
            ew.append(jnp.dot(lhs, rhs1, preferred_element_type=f32))
            q2 = proj_ref[pl.ds(r0, L), p * LANES:(p + 1) * LANES] * (dk ** -0.5)
            q_m.append(jnp.where((lane_ll >= hh * dk) & (lane_ll < (hh + 1) * dk), q2, 0.0))
            s_mat.append(jnp.dot(q_m[h].astype(bf16), kt_pairs[p].astype(bf16), preferred_element_type=f32))
            v = proj_ref[pl.ds(r0, L), ML_QK_W + h * dv:ML_QK_W + (h + 1) * dv].astype(bf16)
            v_ext.append(jnp.concatenate([v, ones_col], axis=1))
        tot = []
        for h in heads:
            p, hh = divmod(h, 2)
            a = jnp.exp(jnp.where(causal, ew[h][:, :L], -jnp.inf)) * s_mat[h]
            qw = q_m[h] * ew[h][:, L:L + LANES]
            lhs = jnp.concatenate([a.astype(bf16), qw.astype(bf16)], axis=1)
            rhs = jnp.concatenate([v_ext[h], c_pairs[p].astype(bf16)], axis=0)
            tot.append(jnp.dot(lhs, rhs, preferred_element_type=f32))
        for h in heads:
            num = tot[h][:, :dv]
            lhs2 = jnp.concatenate([(num * num).astype(bf16), tot[h][:, dv:].astype(bf16)], axis=1)
            r2 = jnp.dot(lhs2, rhs2, preferred_element_type=f32)
            inv = 1.0 / jnp.maximum(jnp.abs(r2[:, dv:]), ew[h][:, L + LANES:])
            hn = (num * inv) * lax.rsqrt(r2[:, :dv] * inv * inv + HEAD_NORM_EPS)
            o_pre = proj_ref[pl.ds(r0, L), o_base + h * dv:o_base + (h + 1) * dv]
            h_ref[pl.ds(r0, L), h * dv:(h + 1) * dv] = (
                jax.nn.sigmoid(o_pre) * hn * norm_g_ref[:, h * dv:(h + 1) * dv])
        for h in heads:
            p, hh = divmod(h, 2)
            rows = slice(hh * dk, (hh + 1) * dk)
            kg = (kt_pairs[p][rows, :] * gr[h:h + 1, :]).astype(bf16)
            gs = gr[nh + h:nh + h + 1, :]
            c_ref[p, rows, :] = (jnp.concatenate([gs, gs], axis=1) * c_pairs[p][rows, :]
                                 + jnp.dot(kg, v_ext[h], preferred_element_type=f32))
        return carry

    lax.fori_loop(0, n_chunks, chunk_body, 0)

    sub = ts // MIX_SUBTILES
    mixes = [jnp.dot(h_ref[k * sub:(k + 1) * sub, :].astype(bf16), w_out_ref[...], preferred_element_type=f32)
             for k in range(MIX_SUBTILES)]
    x1 = jnp.concatenate([_layer_norm(ALPHA * x[k * sub:(k + 1) * sub, :] + mixes[k], ln_g_ref[...], ln_b_ref[...])
                          for k in range(MIX_SUBTILES)], axis=0)
    meta = _route_and_rank(x1, rw_hi_ref[...], rw_lo_ref[...], rb_ref[...], count_ref)
    out_ref[:, :d] = x1
    out_ref[:, d:] = jnp.concatenate([meta, jnp.zeros((LANES - SUBLANES, meta.shape[1]), meta.dtype)], axis=0).T
    meta_t_ref[...] = meta
    _cast_slabs((wg32_ref, wu32_ref, wd32_ref), (wg16_ref, wu16_ref, wd16_ref))


def _mlstm_mixer_layer(x2d, seq, w_in, b_gate, norm_g, w_out, ln_g, ln_b, rw_hi, rw_lo, rb, layer,
                       expert_weights):
    t, d = x2d.shape
    ts = MIX_TILE
    nh = ML_HEADS
    cast_args, cast_in, cast_shapes, cast_out = _cast_slab_specs(layer, t // ts, expert_weights)
    assert ML_CHUNK == LANES and ts % ML_CHUNK == 0 and 2 * ML_QK_DIM == LANES and ML_V_DIM == LANES
    n_qkvo = 2 * ML_QK_W + 2 * ML_V_W
    n_qvo = ML_QK_W + 2 * ML_V_W
    w_qvo = jnp.concatenate([w_in[:, :ML_QK_W], w_in[:, 2 * ML_QK_W:n_qkvo]], axis=1).astype(jnp.bfloat16)
    wkt = w_in[:, ML_QK_W:2 * ML_QK_W].T.astype(jnp.bfloat16)
    w_g = w_in[:, n_qkvo:].astype(jnp.float32)
    wgt_hi, wgt_lo = _split_bf16(w_g.T)
    bg_row = b_gate.astype(jnp.float32)[:, None]
    return pl.pallas_call(
        functools.partial(_mlstm_mixer_kernel, tiles_per_seq=seq // ts),
        out_shape=(jax.ShapeDtypeStruct((t, ROW_W), jnp.float32),
                   jax.ShapeDtypeStruct((SUBLANES, t), jnp.float32),
                   jax.ShapeDtypeStruct((CLASS_ROWS, LANES), jnp.float32), *cast_shapes),
        grid=(t // ts,),
        in_specs=[pl.BlockSpec((ts, d), lambda i: (i, 0)),
                  _const_spec((d, n_qvo)), _const_spec((ML_QK_W, d)),
                  _const_spec((2 * nh, d)), _const_spec((2 * nh, d)),
                  _const_spec((2 * nh, 1)),
                  _const_spec((1, d)), _const_spec((d, d)),
                  _const_spec((1, d)), _const_spec((1, d)),
                  _const_spec((d, LANES)), _const_spec((d, LANES)), _const_spec((N_EXPERTS, 1)), *cast_in],
        out_specs=(pl.BlockSpec((ts, ROW_W), lambda i: (i, 0)),
                   pl.BlockSpec((SUBLANES, ts), lambda i: (0, i)),
                   _const_spec((CLASS_ROWS, LANES)), *cast_out),
        scratch_shapes=[pltpu.VMEM((ts, n_qvo), jnp.float32),
                        pltpu.VMEM((ts // ML_CHUNK, ML_QK_W, ML_CHUNK), jnp.float32),
                        pltpu.VMEM((ts // ML_CHUNK, 2 * nh, ML_CHUNK), jnp.float32),
                        pltpu.VMEM((ts, LANES), jnp.bfloat16),
                        pltpu.VMEM((ts // ML_CHUNK, LANES, ML_CHUNK), jnp.bfloat16),
                        pltpu.VMEM((ts, d), jnp.float32),
                        pltpu.VMEM((nh // 2, 2 * ML_QK_DIM, 2 * ML_V_DIM), jnp.float32),
                        pltpu.VMEM((nh, ML_CHUNK), jnp.float32)],
        compiler_params=pltpu.CompilerParams(dimension_semantics=("arbitrary",),
                                             vmem_limit_bytes=VMEM_LIMIT_BYTES),
        name="mlstm_mixer_route",
    )(x2d, w_qvo, wkt, wgt_hi, wgt_lo, bg_row, norm_g[None, :], w_out.astype(jnp.bfloat16),
      ln_g, ln_b, rw_hi, rw_lo, rb, *cast_args)


def _expert_kernel(tok_ref, elo_ref, ehi_ref, nvalid_ref,
                   x_hbm, wg_lo_ref, wu_lo_ref, wd_lo_ref, wg_hi_ref, wu_hi_ref, wd_hi_ref,
                   ln_g_ref, ln_b_ref, out_hbm, xbuf0, xbuf1, obuf0, obuf1, gather_sem, scatter_sem):
    i = pl.program_id(0)
    nb = pl.num_programs(0)
    bm = obuf0.shape[0]
    d = D_MODEL
    xbufs, obufs = (xbuf0, xbuf1), (obuf0, obuf1)
    nv = nvalid_ref[i]
    nv_prev = nvalid_ref[jnp.maximum(i - 1, 0)]

    def start_gather(j, s, rows=range(bm)):
        for r in rows:
            tok = tok_ref[j * bm + r]
            pltpu.make_async_copy(x_hbm.at[pl.ds(tok, 1)], xbufs[s].at[pl.ds(r, 1)], gather_sem.at[s]).start()

    def wait_gather(s):
        pltpu.make_async_copy(xbufs[s].at[pl.ds(0, bm)], xbufs[s].at[pl.ds(0, bm)], gather_sem.at[s]).wait()

    def scatter_copy(j, s, r):
        tok = tok_ref[j * bm + r]
        return pltpu.make_async_copy(obufs[s].at[pl.ds(r, 1)], out_hbm.at[pl.ds(tok, 1)], scatter_sem.at[s])

    def wait_scatter(s, n):
        @pl.when(n == bm)
        def _():
            pltpu.make_async_copy(obufs[s], obufs[s], scatter_sem.at[s]).wait()

        @pl.when(n < bm)
        def _():
            def body(r, c):
                pltpu.make_async_copy(obufs[s].at[pl.ds(0, 1)], obufs[s].at[pl.ds(0, 1)], scatter_sem.at[s]).wait()
                return c
            lax.fori_loop(0, n, body, 0)

    @pl.when(i == 0)
    def _():
        for buf in xbufs:
            buf[bm:, :] = jnp.zeros((buf.shape[0] - bm, buf.shape[1]), buf.dtype)
        start_gather(0, 0)

    def step(s):
        @pl.when((i == 0) | (nv_prev > 0))
        def _():
            wait_gather(s)

        @pl.when(nv > 0)
        def _():
            xb = xbufs[s][0:bm, :]
            x = xb[:, :d]
            g_lo = xb[:, d + META_G_LO:d + META_G_LO + 1]
            g_hi = xb[:, d + META_G_HI:d + META_G_HI + 1]
            x16 = x.astype(jnp.bfloat16)
            n_portions = 2 * GATE_CHUNKS

            def gate_proj(w_ref, k0):
                g = jnp.dot(x16, w_ref[0], preferred_element_type=jnp.float32)
                width = g.shape[1] // GATE_CHUNKS
                chunks = []
                for c in range(GATE_CHUNKS):
                    k = k0 + c
                    start_gather(i + 1, 1 - s, range(k * bm // n_portions, (k + 1) * bm // n_portions))
                    zero_row = xbufs[1 - s][bm:bm + 1, 0:LANES]
                    chunks.append(g[:, c * width:(c + 1) * width]
                                  + jnp.concatenate([zero_row] * (width // LANES), axis=1))
                return jnp.concatenate(chunks, axis=1)

            def ffn(k0, wg_ref, wu_ref, wd_ref):
                g = gate_proj(wg_ref, k0)
                u = jnp.dot(x16, wu_ref[0], preferred_element_type=jnp.float32)
                h = (g * jax.nn.sigmoid(g)) * u
                return jnp.dot(h.astype(jnp.bfloat16), wd_ref[0], preferred_element_type=jnp.float32)

            y = (g_lo * ffn(0, wg_lo_ref, wu_lo_ref, wd_lo_ref)
                 + g_hi * ffn(GATE_CHUNKS, wg_hi_ref, wu_hi_ref, wd_hi_ref))
            obufs[s][...] = _layer_norm(ALPHA * x + y, ln_g_ref[...], ln_b_ref[...])

            @pl.when(nv == bm)
            def _():
                for r in range(bm):
                    scatter_copy(i, s, r).start()

            @pl.when(nv < bm)
            def _():
                def body(r, c):
                    scatter_copy(i, s, r).start()
                    return c
                lax.fori_loop(0, nv, body, 0)

        @pl.when(i > 0)
        def _():
            wait_scatter(1 - s, nv_prev)

        @pl.when(i == nb - 1)
        def _():
            wait_scatter(s, nv)

            @pl.when(nv > 0)
            def _():
                wait_gather(1 - s)

    for s in range(2):
        pl.when(i % 2 == s)(functools.partial(step, s))


def _moe_layer(xext, meta_t, counts, w_gate, w_up, w_down, ln_g, ln_b):
    t = xext.shape[0]
    d, f, bm = D_MODEL, D_EXPERT, EXPERT_BLOCK
    n_blocks = t // bm + N_CLASSES
    n_rows = n_blocks * bm

    cls = meta_t[META_CLASS].astype(jnp.int32)
    rank = meta_t[META_RANK].astype(jnp.int32)
    cnt = counts[:N_CLASSES, 0].astype(jnp.int32)
    cls_blocks = (cnt + bm - 1) // bm
    blk_end = jnp.cumsum(cls_blocks)
    blk_start = blk_end - cls_blocks
    total_blocks = blk_end[-1]
    row_start = jnp.sum(jnp.where(cls[None, :] == jnp.arange(N_CLASSES, dtype=jnp.int32)[:, None],
                                  blk_start[:, None] * bm, 0), axis=0)
    dest = row_start + rank
    tok_of_row = jnp.zeros((n_rows + bm,), jnp.int32).at[dest].set(
        jnp.arange(t, dtype=jnp.int32), unique_indices=True)
    blk = jnp.arange(n_blocks, dtype=jnp.int32)
    blk_cls = jnp.sum(jnp.minimum(blk, total_blocks - 1)[:, None] >= blk_end[None, :], axis=1).astype(jnp.int32)
    blk_cls = jnp.minimum(blk_cls, N_CLASSES - 1)
    nvalid = jnp.clip(cnt[blk_cls] - (blk - blk_start[blk_cls]) * bm, 0, bm)
    nvalid = jnp.where(blk < total_blocks, nvalid, 0).astype(jnp.int32)
    e_lo = jnp.asarray(CLASS_E_LO)[blk_cls]
    e_hi = jnp.asarray(CLASS_E_HI)[blk_cls]

    w_gate, w_up, w_down = (w_gate.reshape(N_EXPERTS, d, f), w_up.reshape(N_EXPERTS, d, f),
                            w_down.reshape(N_EXPERTS, f, d))
    w_lo = lambda shape: pl.BlockSpec((1,) + shape, lambda i, tok, elo, ehi, nv: (elo[i], 0, 0))
    w_hi = lambda shape: pl.BlockSpec((1,) + shape, lambda i, tok, elo, ehi, nv: (ehi[i], 0, 0))
    vec = pl.BlockSpec((1, d), lambda i, tok, elo, ehi, nv: (0, 0))
    grid_spec = pltpu.PrefetchScalarGridSpec(
        num_scalar_prefetch=4,
        grid=(n_blocks,),
        in_specs=[pl.BlockSpec(memory_space=pl.ANY),
                  w_lo((d, f)), w_lo((d, f)), w_lo((f, d)),
                  w_hi((d, f)), w_hi((d, f)), w_hi((f, d)),
                  vec, vec],
        out_specs=pl.BlockSpec(memory_space=pl.ANY),
        scratch_shapes=[pltpu.VMEM((bm + SUBLANES, ROW_W), jnp.float32), pltpu.VMEM((bm + SUBLANES, ROW_W), jnp.float32),
                        pltpu.VMEM((bm, d), jnp.float32), pltpu.VMEM((bm, d), jnp.float32),
                        pltpu.SemaphoreType.DMA((2,)),
                        pltpu.SemaphoreType.DMA((2,))],
    )
    return pl.pallas_call(
        _expert_kernel,
        out_shape=jax.ShapeDtypeStruct((t, d), jnp.float32),
        grid_spec=grid_spec,
        compiler_params=pltpu.CompilerParams(dimension_semantics=("arbitrary",),
                                             vmem_limit_bytes=VMEM_LIMIT_BYTES),
        name="expert_pair_ffn",
    )(tok_of_row, e_lo, e_hi, nvalid, xext, w_gate, w_up, w_down, w_gate, w_up, w_down, ln_g, ln_b)


def kernel(x, conv_w_in, conv_w, conv_w_out, ml_w_in, ml_b_gate, ml_norm_g, ml_w_out, ln_mix_g, ln_mix_b,
           ln_ffn_g, ln_ffn_b, router_w, router_b, exp_w_gate, exp_w_up, exp_w_down):
    bsz, seq, d = x.shape
    assert d == D_MODEL and seq % MIX_TILE == 0 and (bsz * seq) % EXPERT_BLOCK == 0
    bf16 = jnp.bfloat16
    vec = lambda a: a.astype(jnp.float32)[None, :]
    (rw_hi, rw_lo), rb = _prep_router(router_w, router_b)
    x2d = x.reshape(bsz * seq, d).astype(jnp.float32)
    expert_weights = (exp_w_gate.astype(jnp.float32), exp_w_up.astype(jnp.float32),
                      exp_w_down.astype(jnp.float32))
    for i in range(DEPTH):
        j = i // 2
        if i % 2 == 0:
            xext, meta_t, counts, *w16 = _conv_mixer_layer(
                x2d, seq, conv_w_in[j].astype(bf16), _pad_rows(conv_w[j].astype(jnp.float32)),
                conv_w_out[j].astype(bf16), vec(ln_mix_g[i]), vec(ln_mix_b[i]), rw_hi, rw_lo, rb,
                i, expert_weights)
        else:
            xext, meta_t, counts, *w16 = _mlstm_mixer_layer(
                x2d, seq, ml_w_in[j], ml_b_gate[j], ml_norm_g[j].astype(jnp.float32), ml_w_out[j],
                vec(ln_mix_g[i]), vec(ln_mix_b[i]), rw_hi, rw_lo, rb, i, expert_weights)
        x2d = _moe_layer(xext, meta_t, counts, *w16, vec(ln_ffn_g[i]), vec(ln_ffn_b[i]))
    return x2d.reshape(bsz, seq, d).astype(x.dtype)
```

```python
import functools
import itertools

import numpy as np
import jax
import jax.numpy as jnp
from jax import lax
from jax.experimental import pallas as pl
from jax.experimental.pallas import tpu as pltpu

D_MODEL = 1024
DEPTH = 2
CONV_WIDTH = 3
ML_HEADS = 8
ML_QK_DIM = D_MODEL // (2 * ML_HEADS)
ML_V_DIM = D_MODEL // ML_HEADS
ML_QK_W = ML_HEADS * ML_QK_DIM
ML_V_W = ML_HEADS * ML_V_DIM
N_EXPERTS = 16
N_GROUPS = 4
EXPERTS_PER_GROUP = N_EXPERTS // N_GROUPS
D_EXPERT = 3 * D_MODEL // 2
ALPHA = (2 * DEPTH) ** 0.25
LN_EPS = 1e-5
HEAD_NORM_EPS = 1e-6

LANES = 128
SUBLANES = 8
VMEM_LIMIT_BYTES = 56 * 1024 * 1024

PAIRS = tuple(itertools.combinations(range(EXPERTS_PER_GROUP), 2))
N_PAIRS = len(PAIRS)
N_CLASSES = N_GROUPS * N_PAIRS
CLASS_ROWS = -(-N_CLASSES // 16) * 16
CLASS_E_LO = np.array([g * EXPERTS_PER_GROUP + a for g in range(N_GROUPS) for a, _ in PAIRS], np.int32)
CLASS_E_HI = np.array([g * EXPERTS_PER_GROUP + b for g in range(N_GROUPS) for _, b in PAIRS], np.int32)

META_G_LO, META_G_HI, META_CLASS, META_RANK = 0, 1, 2, 3
ROW_W = D_MODEL + LANES

MIX_TILE = 512
MIX_SUBTILES = 2
EXPERT_BLOCK = 256
GATE_CHUNKS = 6
ML_CHUNK = 128


def _layer_norm(z, g, b):
    mu = jnp.mean(z, axis=-1, keepdims=True)
    zc = z - mu
    var = jnp.mean(zc * zc, axis=-1, keepdims=True)
    return zc * lax.rsqrt(var + LN_EPS) * g + b


def _split_bf16(a):
    hi = a.astype(jnp.bfloat16)
    lo = (a - hi.astype(jnp.float32)).astype(jnp.bfloat16)
    return hi, lo


def _route_and_rank(x1, rw_hi, rw_lo, rb_col, count_ref):
    ts = x1.shape[0]
    f32, bf16 = jnp.float32, jnp.bfloat16
    n_g, n_m = N_GROUPS, EXPERTS_PER_GROUP

    x_hi, x_lo = _split_bf16(x1)
    logits = (jnp.dot(x_hi, rw_hi, preferred_element_type=f32)
              + jnp.dot(x_hi, rw_lo, preferred_element_type=f32)
              + jnp.dot(x_lo, rw_hi, preferred_element_type=f32))
    lt = logits.T[:N_EXPERTS, :]
    ex = jnp.exp(lt - jnp.max(lt, axis=0, keepdims=True))
    probs = ex / jnp.sum(ex, axis=0, keepdims=True)
    sel = probs + rb_col
    member = lambda a, j: a[j * n_g:(j + 1) * n_g, :]

    in_top2 = []
    for j in range(n_m):
        beaten_by = jnp.zeros((n_g, ts), jnp.int32)
        for k in range(n_m):
            if k != j:
                wins = member(sel, k) > member(sel, j)
                if k < j:
                    wins = wins | (member(sel, k) == member(sel, j))
                beaten_by = beaten_by + wins.astype(jnp.int32)
        in_top2.append(beaten_by < 2)
    score = sum(jnp.where(in_top2[j], member(sel, j), 0.0) for j in range(n_m))
    group = lax.broadcasted_iota(jnp.int32, (n_g, ts), 0)
    best = jnp.max(score, axis=0, keepdims=True)
    g_best = jnp.min(jnp.where(score == best, group, n_g), axis=0, keepdims=True)
    in_best = group == g_best
    picked = [jnp.max(jnp.where(in_top2[j] & in_best, 1, 0), axis=0, keepdims=True) for j in range(n_m)]
    p_pick = [jnp.sum(jnp.where(in_top2[j] & in_best, member(probs, j), 0.0), axis=0, keepdims=True)
              for j in range(n_m)]
    a = functools.reduce(jnp.minimum, [jnp.where(picked[j] > 0, j, n_m) for j in range(n_m)])
    b = functools.reduce(jnp.maximum, [jnp.where(picked[j] > 0, j, -1) for j in range(n_m)])
    p_lo = sum(jnp.where(a == j, p_pick[j], 0.0) for j in range(n_m))
    p_hi = sum(jnp.where(b == j, p_pick[j], 0.0) for j in range(n_m))
    g_lo = p_lo / (p_lo + p_hi)
    g_hi = p_hi / (p_lo + p_hi)
    pair = ((a * (2 * n_m - 1 - a)) >> 1) + (b - a - 1)
    cls = g_best * N_PAIRS + pair

    n_rows = count_ref.shape[0]
    onehot = jnp.where(lax.broadcasted_iota(jnp.int32, (n_rows, ts), 0) == cls, 1.0, 0.0)
    earlier = jnp.where(lax.broadcasted_iota(jnp.int32, (ts, ts), 0)
                        < lax.broadcasted_iota(jnp.int32, (ts, ts), 1), 1.0, 0.0).astype(bf16)
    before = jnp.dot(onehot.astype(bf16), earlier, preferred_element_type=f32)
    running = count_ref[...]
    running_ts = jnp.concatenate([running] * (ts // LANES), axis=1)
    rank = jnp.sum(onehot * (before + running_ts), axis=0, keepdims=True)
    count_ref[...] = running + jnp.dot(onehot.astype(bf16), jnp.ones((ts, LANES), bf16),
                                       preferred_element_type=f32)
    return jnp.concatenate([g_lo, g_hi, cls.astype(f32), rank, jnp.zeros((SUBLANES - 4, ts), f32)], axis=0)


def _cast_slabs(w32_refs, w16_refs):
    for w32_ref, w16_ref in zip(w32_refs, w16_refs):
        w16_ref[...] = w32_ref[0].astype(w16_ref.dtype)


def _cast_slab_specs(layer, n_steps, weights):
    args, in_specs, out_shapes, out_specs = [], [], [], []
    for w in weights:
        n_layers, n_exp, rows, cols = w.shape
        slab, rem = divmod(n_exp * rows, n_steps)
        assert rem == 0 and slab % (2 * SUBLANES) == 0
        args.append(w.reshape(n_layers, n_exp * rows, cols))
        in_specs.append(pl.BlockSpec((1, slab, cols), lambda i: (layer, i, 0)))
        out_shapes.append(jax.ShapeDtypeStruct((n_exp * rows, cols), jnp.bfloat16))
        out_specs.append(pl.BlockSpec((slab, cols), lambda i: (i, 0)))
    return args, in_specs, out_shapes, out_specs


def _conv_mixer_kernel(x_ref, w_in_ref, w_conv_ref, w_out_ref, ln_g_ref, ln_b_ref,
                       rw_hi_ref, rw_lo_ref, rb_ref, wg32_ref, wu32_ref, wd32_ref,
                       out_ref, meta_t_ref, count_ref, wg16_ref, wu16_ref, wd16_ref, carry_ref, *, tiles_per_seq):
    i = pl.program_id(0)

    @pl.when(i == 0)
    def _():
        count_ref[...] = jnp.zeros_like(count_ref)

    @pl.when(i % tiles_per_seq == 0)
    def _():
        carry_ref[...] = jnp.zeros_like(carry_ref)

    f32 = jnp.float32
    d = D_MODEL
    ts = x_ref.shape[0]
    sub = ts // MIX_SUBTILES
    xs = [x_ref[k * sub:(k + 1) * sub, :] for k in range(MIX_SUBTILES)]
    projs = [jnp.dot(x.astype(jnp.bfloat16), w_in_ref[...], preferred_element_type=f32) for x in xs]
    row = lax.broadcasted_iota(jnp.int32, (sub, d), 0)
    wc = w_conv_ref[...]
    prev = carry_ref[...]
    gated = []
    for proj in projs:
        gate_b, gate_c, h = proj[:, :d], proj[:, d:2 * d], proj[:, 2 * d:]
        u = gate_c * h
        u1 = jnp.where(row == 0, prev[SUBLANES - 1:SUBLANES, :], pltpu.roll(u, 1, axis=0))
        u2 = jnp.where(row == 0, prev[SUBLANES - 2:SUBLANES - 1, :],
                       jnp.where(row == 1, prev[SUBLANES - 1:SUBLANES, :], pltpu.roll(u, 2, axis=0)))
        prev = u[sub - SUBLANES:, :]
        conv = wc[0:1, :] * u2 + wc[1:2, :] * u1 + wc[2:3, :] * u
        gated.append((gate_b * conv).astype(jnp.bfloat16))
    carry_ref[...] = prev
    mixes = [jnp.dot(v, w_out_ref[...], preferred_element_type=f32) for v in gated]
    x1 = jnp.concatenate([_layer_norm(ALPHA * x + mix, ln_g_ref[...], ln_b_ref[...])
                          for x, mix in zip(xs, mixes)], axis=0)
    meta = _route_and_rank(x1, rw_hi_ref[...], rw_lo_ref[...], rb_ref[...], count_ref)
    out_ref[:, :d] = x1
    out_ref[:, d:] = jnp.concatenate([meta, jnp.zeros((LANES - SUBLANES, meta.shape[1]), meta.dtype)], axis=0).T
    meta_t_ref[...] = meta
    _cast_slabs((wg32_ref, wu32_ref, wd32_ref), (wg16_ref, wu16_ref, wd16_ref))


def _const_spec(shape):
    return pl.BlockSpec(shape, lambda i: (0,) * len(shape))


def _pad_rows(a, rows=SUBLANES):
    return jnp.pad(a, ((0, rows - a.shape[0]), (0, 0)))


def _prep_router(router_w, router_b):
    order = np.array([g * EXPERTS_PER_GROUP + j for j in range(EXPERTS_PER_GROUP) for g in range(N_GROUPS)])
    rw = jnp.pad(router_w.astype(jnp.float32)[:, order], ((0, 0), (0, LANES - N_EXPERTS)))
    rb = router_b.astype(jnp.float32)[order][:, None]
    return _split_bf16(rw), rb


def _conv_mixer_layer(x2d, seq, w_in, w_conv, w_out, ln_g, ln_b, rw_hi, rw_lo, rb, layer, expert_weights):
    t, d = x2d.shape
    ts = MIX_TILE
    grid = (t // ts,)
    cast_args, cast_in, cast_shapes, cast_out = _cast_slab_specs(layer, grid[0], expert_weights)
    return pl.pallas_call(
        functools.partial(_conv_mixer_kernel, tiles_per_seq=seq // ts),
        out_shape=(jax.ShapeDtypeStruct((t, ROW_W), jnp.float32),
                   jax.ShapeDtypeStruct((SUBLANES, t), jnp.float32),
                   jax.ShapeDtypeStruct((CLASS_ROWS, LANES), jnp.float32), *cast_shapes),
        grid=grid,
        in_specs=[pl.BlockSpec((ts, d), lambda i: (i, 0)),
                  _const_spec((d, 3 * d)), _const_spec((SUBLANES, d)), _const_spec((d, d)),
                  _const_spec((1, d)), _const_spec((1, d)),
                  _const_spec((d, LANES)), _const_spec((d, LANES)), _const_spec((N_EXPERTS, 1)), *cast_in],
        out_specs=(pl.BlockSpec((ts, ROW_W), lambda i: (i, 0)),
                   pl.BlockSpec((SUBLANES, ts), lambda i: (0, i)),
                   _const_spec((CLASS_ROWS, LANES)), *cast_out),
        scratch_shapes=[pltpu.VMEM((SUBLANES, d), jnp.float32)],
        compiler_params=pltpu.CompilerParams(dimension_semantics=("arbitrary",),
                                             vmem_limit_bytes=VMEM_LIMIT_BYTES),
        name="conv_mixer_route",
    )(x2d, w_in, w_conv, w_out, ln_g, ln_b, rw_hi, rw_lo, rb, *cast_args)


def _log_sigmoid(z):
    return jnp.minimum(z, 0.0) - jnp.log1p(jnp.exp(-jnp.abs(z)))


def _split3_f32(a):
    f32, bf16 = jnp.float32, jnp.bfloat16
    hi = a.astype(bf16).astype(f32)
    r1 = a - hi
    mid = r1.astype(bf16).astype(f32)
    lo = (r1 - mid).astype(bf16).astype(f32)
    return hi, mid, lo


_COL_U, _COL_ONE, _COL_W, _COL_EMT, _COL_END = 0, 24, 48, 64, 80


def _mlstm_mixer_kernel(x_ref, w_qvo_ref, wkt_ref, wgt_hi_ref, wgt_lo_ref, bg_row_ref, norm_g_ref, w_out_ref,
                        ln_g_ref, ln_b_ref, rw_hi_ref, rw_lo_ref, rb_ref, wg32_ref, wu32_ref, wd32_ref,
                        out_ref, meta_t_ref, count_ref, wg16_ref, wu16_ref, wd16_ref,
                        proj_ref, kt_ref, grow_ref, colmat_ref, bexp_ref, h_ref, c_ref, m_ref, *, tiles_per_seq):
    i = pl.program_id(0)
    f32, bf16 = jnp.float32, jnp.bfloat16
    nh, dk, dv, d = ML_HEADS, ML_QK_DIM, ML_V_DIM, D_MODEL
    L = ML_CHUNK
    ts = x_ref.shape[0]
    n_chunks = ts // L
    o_base = ML_QK_W + ML_V_W

    @pl.when(i == 0)
    def _():
        count_ref[...] = jnp.zeros_like(count_ref)

    @pl.when(i % tiles_per_seq == 0)
    def _():
        c_ref[...] = jnp.zeros_like(c_ref)
        m_ref[...] = jnp.zeros_like(m_ref)

    x = x_ref[...]
    x_hi, x_lo = _split_bf16(x)
    nt = (((1,), (1,)), ((), ()))
    proj_ref[...] = jnp.dot(x_hi, w_qvo_ref[...], preferred_element_type=f32)
    k_t = lax.dot_general(wkt_ref[...], x_hi, nt, preferred_element_type=f32)
    g_row = (lax.dot_general(wgt_hi_ref[...], x_hi, nt, preferred_element_type=f32)
             + lax.dot_general(wgt_lo_ref[...], x_hi, nt, preferred_element_type=f32)
             + lax.dot_general(wgt_hi_ref[...], x_lo, nt, preferred_element_type=f32)) + bg_row_ref[...]
    r_i = lax.broadcasted_iota(jnp.int32, (ts, ts), 0)
    c_i = lax.broadcasted_iota(jnp.int32, (ts, ts), 1)
    tri_row = jnp.where(((r_i // L) == (c_i // L)) & (r_i <= c_i), 1.0, 0.0).astype(bf16)
    lf_hi, lf_lo = _split_bf16(_log_sigmoid(g_row))
    cum_row = (jnp.dot(lf_hi, tri_row, preferred_element_type=f32)
               + jnp.dot(lf_lo, tri_row, preferred_element_type=f32))
    i_row, b_row = g_row[:nh, :], cum_row[nh:, :]

    g = i_row - b_row
    lane_in_chunk = lax.broadcasted_iota(jnp.int32, (nh, ts), 1) & (L - 1)
    cm = g
    shift = 1
    while shift < L:
        cm = jnp.maximum(cm, jnp.where(lane_in_chunk >= shift, pltpu.roll(cm, shift, axis=1), -jnp.inf))
        shift *= 2
    m_prev = m_ref[:, 0:1]
    u_parts, w_parts, gr_parts, gs_parts = [], [], [], []
    for c in range(n_chunks):
        sl = slice(c * L, (c + 1) * L)
        u_c = jnp.maximum(m_prev, cm[:, sl])
        b_last = b_row[:, (c + 1) * L - 1:(c + 1) * L]
        m_new = b_last + u_c[:, L - 1:L]
        u_parts.append(u_c)
        w_parts.append(jnp.exp(m_prev - u_c))
        gr_parts.append(jnp.exp(b_last - b_row[:, sl] + i_row[:, sl] - m_new))
        gs_parts.append(jnp.broadcast_to(jnp.exp(b_last + m_prev - m_new), (nh, L)))
        m_prev = m_new
    m_ref[...] = jnp.broadcast_to(m_prev, m_ref.shape)
    u = jnp.concatenate(u_parts, axis=1)
    w_inter = jnp.concatenate(w_parts, axis=1)
    emt = jnp.exp(-(b_row + u))
    w_hi = w_inter.astype(bf16).astype(f32)
    e_hi = emt.astype(bf16).astype(f32)
    col_rows = jnp.concatenate(
        [*_split3_f32(u), jnp.ones((_COL_W - _COL_ONE, ts), f32), w_hi, w_inter - w_hi, e_hi, emt - e_hi,
         jnp.zeros((LANES - _COL_END, ts), f32)], axis=0)
    colmat_ref[...] = col_rows.T.astype(bf16)
    b_exp = jnp.concatenate([jnp.full((_COL_ONE - _COL_U, ts), -1.0, f32), *_split3_f32(g),
                             jnp.zeros((LANES - _COL_W, ts), f32)], axis=0).astype(bf16)
    for c in range(n_chunks):
        sl = slice(c * L, (c + 1) * L)
        bexp_ref[c] = b_exp[:, sl]
        grow_ref[c] = jnp.concatenate([gr_parts[c], gs_parts[c]], axis=0)
        kt_ref[c] = k_t[:, sl]

    causal = lax.broadcasted_iota(jnp.int32, (L, L), 0) >= lax.broadcasted_iota(jnp.int32, (L, L), 1)
    ones_col = jnp.where(lax.broadcasted_iota(jnp.int32, (L, dv), 1) == 0, 1.0, 0.0).astype(bf16)
    lane_ll = lax.broadcasted_iota(jnp.int32, (L, LANES), 1)
    sel_r = lax.broadcasted_iota(jnp.int32, (LANES, 2 * LANES), 0)
    sel_c = lax.broadcasted_iota(jnp.int32, (LANES, 2 * LANES), 1)
    is_w_row = (sel_r >= _COL_W) & (sel_r < _COL_EMT)
    is_e_row = (sel_r >= _COL_EMT) & (sel_r < _COL_END)
    b_sel = jnp.where((is_w_row & (sel_c < LANES)) | (is_e_row & (sel_c >= LANES)), 1.0, 0.0).astype(bf16)
    r2_r = lax.broadcasted_iota(jnp.int32, (2 * dv, 2 * dv), 0)
    r2_c = lax.broadcasted_iota(jnp.int32, (2 * dv, 2 * dv), 1)
    rhs2 = jnp.where((r2_r < dv) & (r2_c < dv), 1.0 / dv,
                     jnp.where((r2_r == dv) & (r2_c >= dv), 1.0, 0.0)).astype(bf16)

    def chunk_body(c, carry):
        r0 = pl.multiple_of(c * L, L)
        gr = grow_ref[c]
        colmat = colmat_ref[pl.ds(r0, L), :]
        rhs1 = jnp.concatenate([bexp_ref[c], b_sel], axis=1)
        heads = range(nh)
        c_pairs = [c_ref[p] for p in range(nh // 2)]
        kt_pairs = [kt_ref[c, p * LANES:(p + 1) * LANES, :] for p in range(nh // 2)]

        ew, q_m, s_mat, v_ext = [], [], [], []
        for h in heads:
            p, hh = divmod(h, 2)
            lhs = jnp.where((lane_ll & (nh - 1)) == h, colmat, 0)
            ew.append(jnp.dot(lhs, rhs1, preferred_element_type=f32))
            q2 = proj_ref[pl.ds(r0, L), p * LANES:(p + 1) * LANES] * (dk ** -0.5)
            q_m.append(jnp.where((lane_ll >= hh * dk) & (lane_ll < (hh + 1) * dk), q2, 0.0))
            s_mat.append(jnp.dot(q_m[h].astype(bf16), kt_pairs[p].astype(bf16), preferred_element_type=f32))
            v = proj_ref[pl.ds(r0, L), ML_QK_W + h * dv:ML_QK_W + (h + 1) * dv].astype(bf16)
            v_ext.append(jnp.concatenate([v, ones_col], axis=1))
        tot = []
        for h in heads:
            p, hh = divmod(h, 2)
            a = jnp.exp(jnp.where(causal, ew[h][:, :L], -jnp.inf)) * s_mat[h]
            qw = q_m[h] * ew[h][:, L:L + LANES]
            lhs = jnp.concatenate([a.astype(bf16), qw.astype(bf16)], axis=1)
            rhs = jnp.concatenate([v_ext[h], c_pairs[p].astype(bf16)], axis=0)
            tot.append(jnp.dot(lhs, rhs, preferred_element_type=f32))
        for h in heads:
            num = tot[h][:, :dv]
            lhs2 = jnp.concatenate([(num * num).astype(bf16), tot[h][:, dv:].astype(bf16)], axis=1)
            r2 = jnp.dot(lhs2, rhs2, preferred_element_type=f32)
            inv = 1.0 / jnp.maximum(jnp.abs(r2[:, dv:]), ew[h][:, L + LANES:])
            hn = (num * inv) * lax.rsqrt(r2[:, :dv] * inv * inv + HEAD_NORM_EPS)
            o_pre = proj_ref[pl.ds(r0, L), o_base + h * dv:o_base + (h + 1) * dv]
            h_ref[pl.ds(r0, L), h * dv:(h + 1) * dv] = (
                jax.nn.sigmoid(o_pre) * hn * norm_g_ref[:, h * dv:(h + 1) * dv])
        for h in heads:
            p, hh = divmod(h, 2)
            rows = slice(hh * dk, (hh + 1) * dk)
            kg = (kt_pairs[p][rows, :] * gr[h:h + 1, :]).astype(bf16)
            gs = gr[nh + h:nh + h + 1, :]
            c_ref[p, rows, :] = (jnp.concatenate([gs, gs], axis=1) * c_pairs[p][rows, :]
                                 + jnp.dot(kg, v_ext[h], preferred_element_type=f32))
        return carry

    lax.fori_loop(0, n_chunks, chunk_body, 0)

    sub = ts // MIX_SUBTILES
    mixes = [jnp.dot(h_ref[k * sub:(k + 1) * sub, :].astype(bf16), w_out_ref[...], preferred_element_type=f32)
             for k in range(MIX_SUBTILES)]
    x1 = jnp.concatenate([_layer_norm(ALPHA * x[k * sub:(k + 1) * sub, :] + mixes[k], ln_g_ref[...], ln_b_ref[...])
                          for k in range(MIX_SUBTILES)], axis=0)
    meta = _route_and_rank(x1, rw_hi_ref[...], rw_lo_ref[...], rb_ref[...], count_ref)
    out_ref[:, :d] = x1
    out_ref[:, d:] = jnp.concatenate([meta, jnp.zeros((LANES - SUBLANES, meta.shape[1]), meta.dtype)], axis=0).T
    meta_t_ref[...] = meta
    _cast_slabs((wg32_ref, wu32_ref, wd32_ref), (wg16_ref, wu16_ref, wd16_ref))


def _mlstm_mixer_layer(x2d, seq, w_in, b_gate, norm_g, w_out, ln_g, ln_b, rw_hi, rw_lo, rb, layer,
                       expert_weights):
    t, d = x2d.shape
    ts = MIX_TILE
    nh = ML_HEADS
    cast_args, cast_in, cast_shapes, cast_out = _cast_slab_specs(layer, t // ts, expert_weights)
    assert ML_CHUNK == LANES and ts % ML_CHUNK == 0 and 2 * ML_QK_DIM == LANES and ML_V_DIM == LANES
    n_qkvo = 2 * ML_QK_W + 2 * ML_V_W
    n_qvo = ML_QK_W + 2 * ML_V_W
    w_qvo = jnp.concatenate([w_in[:, :ML_QK_W], w_in[:, 2 * ML_QK_W:n_qkvo]], axis=1).astype(jnp.bfloat16)
    wkt = w_in[:, ML_QK_W:2 * ML_QK_W].T.astype(jnp.bfloat16)
    w_g = w_in[:, n_qkvo:].astype(jnp.float32)
    wgt_hi, wgt_lo = _split_bf16(w_g.T)
    bg_row = b_gate.astype(jnp.float32)[:, None]
    return pl.pallas_call(
        functools.partial(_mlstm_mixer_kernel, tiles_per_seq=seq // ts),
        out_shape=(jax.ShapeDtypeStruct((t, ROW_W), jnp.float32),
                   jax.ShapeDtypeStruct((SUBLANES, t), jnp.float32),
                   jax.ShapeDtypeStruct((CLASS_ROWS, LANES), jnp.float32), *cast_shapes),
        grid=(t // ts,),
        in_specs=[pl.BlockSpec((ts, d), lambda i: (i, 0)),
                  _const_spec((d, n_qvo)), _const_spec((ML_QK_W, d)),
                  _const_spec((2 * nh, d)), _const_spec((2 * nh, d)),
                  _const_spec((2 * nh, 1)),
                  _const_spec((1, d)), _const_spec((d, d)),
                  _const_spec((1, d)), _const_spec((1, d)),
                  _const_spec((d, LANES)), _const_spec((d, LANES)), _const_spec((N_EXPERTS, 1)), *cast_in],
        out_specs=(pl.BlockSpec((ts, ROW_W), lambda i: (i, 0)),
                   pl.BlockSpec((SUBLANES, ts), lambda i: (0, i)),
                   _const_spec((CLASS_ROWS, LANES)), *cast_out),
        scratch_shapes=[pltpu.VMEM((ts, n_qvo), jnp.float32),
                        pltpu.VMEM((ts // ML_CHUNK, ML_QK_W, ML_CHUNK), jnp.float32),
                        pltpu.VMEM((ts // ML_CHUNK, 2 * nh, ML_CHUNK), jnp.float32),
                        pltpu.VMEM((ts, LANES), jnp.bfloat16),
                        pltpu.VMEM((ts // ML_CHUNK, LANES, ML_CHUNK), jnp.bfloat16),
                        pltpu.VMEM((ts, d), jnp.float32),
                        pltpu.VMEM((nh // 2, 2 * ML_QK_DIM, 2 * ML_V_DIM), jnp.float32),
                        pltpu.VMEM((nh, ML_CHUNK), jnp.float32)],
        compiler_params=pltpu.CompilerParams(dimension_semantics=("arbitrary",),
                                             vmem_limit_bytes=VMEM_LIMIT_BYTES),
        name="mlstm_mixer_route",
    )(x2d, w_qvo, wkt, wgt_hi, wgt_lo, bg_row, norm_g[None, :], w_out.astype(jnp.bfloat16),
      ln_g, ln_b, rw_hi, rw_lo, rb, *cast_args)


def _expert_kernel(dest_ref, elo_ref, ehi_ref, nvalid_ref,
                   x_hbm, wg_lo_ref, wu_lo_ref, wd_lo_ref, wg_hi_ref, wu_hi_ref, wd_hi_ref,
                   ln_g_ref, ln_b_ref, out_hbm, xbuf0, xbuf1, obuf0, obuf1, tok_ref, gather_sem, scatter_sem):
    i = pl.program_id(0)
    nb = pl.num_programs(0)
    bm = obuf0.shape[0]
    d = D_MODEL

    @pl.when(i == 0)
    def _():
        def pad_block(b, carry):
            def pad_row(r, c):
                tok_ref[b * bm + r] = 0
                return c
            first = jnp.where(b < nb, nvalid_ref[jnp.minimum(b, nb - 1)], 0)
            return lax.fori_loop(first, bm, pad_row, carry)

        def place(t, c):
            tok_ref[dest_ref[t]] = t
            return c

        lax.fori_loop(0, nb + 1, pad_block, 0)
        lax.fori_loop(0, x_hbm.shape[0], place, 0, unroll=8)
    xbufs, obufs = (xbuf0, xbuf1), (obuf0, obuf1)
    nv = nvalid_ref[i]
    nv_prev = nvalid_ref[jnp.maximum(i - 1, 0)]

    def start_gather(j, s, rows=range(bm)):
        for r in rows:
            tok = tok_ref[j * bm + r]
            pltpu.make_async_copy(x_hbm.at[pl.ds(tok, 1)], xbufs[s].at[pl.ds(r, 1)], gather_sem.at[s]).start()

    def wait_gather(s):
        pltpu.make_async_copy(xbufs[s].at[pl.ds(0, bm)], xbufs[s].at[pl.ds(0, bm)], gather_sem.at[s]).wait()

    def scatter_copy(j, s, r):
        tok = tok_ref[j * bm + r]
        return pltpu.make_async_copy(obufs[s].at[pl.ds(r, 1)], out_hbm.at[pl.ds(tok, 1)], scatter_sem.at[s])

    def wait_scatter(s, n):
        @pl.when(n == bm)
        def _():
            pltpu.make_async_copy(obufs[s], obufs[s], scatter_sem.at[s]).wait()

        @pl.when(n < bm)
        def _():
            def body(r, c):
                pltpu.make_async_copy(obufs[s].at[pl.ds(0, 1)], obufs[s].at[pl.ds(0, 1)], scatter_sem.at[s]).wait()
                return c
            lax.fori_loop(0, n, body, 0)

    @pl.when(i == 0)
    def _():
        for buf in xbufs:
            buf[bm:, :] = jnp.zeros((buf.shape[0] - bm, buf.shape[1]), buf.dtype)
        start_gather(0, 0)

    def step(s):
        @pl.when((i == 0) | (nv_prev > 0))
        def _():
            wait_gather(s)

        @pl.when(nv > 0)
        def _():
            xb = xbufs[s][0:bm, :]
            x = xb[:, :d]
            g_lo = xb[:, d + META_G_LO:d + META_G_LO + 1]
            g_hi = xb[:, d + META_G_HI:d + META_G_HI + 1]
            x16 = x.astype(jnp.bfloat16)
            n_portions = 2 * GATE_CHUNKS

            def gate_proj(w_ref, k0):
                g = jnp.dot(x16, w_ref[0], preferred_element_type=jnp.float32)
                width = g.shape[1] // GATE_CHUNKS
                chunks = []
                for c in range(GATE_CHUNKS):
                    k = k0 + c
                    start_gather(i + 1, 1 - s, range(k * bm // n_portions, (k + 1) * bm // n_portions))
                    zero_row = xbufs[1 - s][bm:bm + 1, 0:LANES]
                    chunks.append(g[:, c * width:(c + 1) * width]
                                  + jnp.concatenate([zero_row] * (width // LANES), axis=1))
                return jnp.concatenate(chunks, axis=1)

            def ffn(k0, wg_ref, wu_ref, wd_ref):
                g = gate_proj(wg_ref, k0)
                u = jnp.dot(x16, wu_ref[0], preferred_element_type=jnp.float32)
                h = (g * jax.nn.sigmoid(g)) * u
                return jnp.dot(h.astype(jnp.bfloat16), wd_ref[0], preferred_element_type=jnp.float32)

            y = (g_lo * ffn(0, wg_lo_ref, wu_lo_ref, wd_lo_ref)
                 + g_hi * ffn(GATE_CHUNKS, wg_hi_ref, wu_hi_ref, wd_hi_ref))
            obufs[s][...] = _layer_norm(ALPHA * x + y, ln_g_ref[...], ln_b_ref[...])

            @pl.when(nv == bm)
            def _():
                for r in range(bm):
                    scatter_copy(i, s, r).start()

            @pl.when(nv < bm)
            def _():
                def body(r, c):
                    scatter_copy(i, s, r).start()
                    return c
                lax.fori_loop(0, nv, body, 0)

        @pl.when(i > 0)
        def _():
            wait_scatter(1 - s, nv_prev)

        @pl.when(i == nb - 1)
        def _():
            wait_scatter(s, nv)

            @pl.when(nv > 0)
            def _():
                wait_gather(1 - s)

    for s in range(2):
        pl.when(i % 2 == s)(functools.partial(step, s))


def _moe_layer(xext, meta_t, counts, w_gate, w_up, w_down, ln_g, ln_b):
    t = xext.shape[0]
    d, f, bm = D_MODEL, D_EXPERT, EXPERT_BLOCK
    n_blocks = t // bm + N_CLASSES
    n_rows = n_blocks * bm

    cls = meta_t[META_CLASS].astype(jnp.int32)
    rank = meta_t[META_RANK].astype(jnp.int32)
    cnt = counts[:N_CLASSES, 0].astype(jnp.int32)
    cls_blocks = (cnt + bm - 1) // bm
    blk_end = jnp.cumsum(cls_blocks)
    blk_start = blk_end - cls_blocks
    total_blocks = blk_end[-1]
    row_start = jnp.sum(jnp.where(cls[None, :] == jnp.arange(N_CLASSES, dtype=jnp.int32)[:, None],
                                  blk_start[:, None] * bm, 0), axis=0)
    dest = (row_start + rank).astype(jnp.int32)
    blk = jnp.arange(n_blocks, dtype=jnp.int32)
    blk_cls = jnp.sum(jnp.minimum(blk, total_blocks - 1)[:, None] >= blk_end[None, :], axis=1).astype(jnp.int32)
    blk_cls = jnp.minimum(blk_cls, N_CLASSES - 1)
    nvalid = jnp.clip(cnt[blk_cls] - (blk - blk_start[blk_cls]) * bm, 0, bm)
    nvalid = jnp.where(blk < total_blocks, nvalid, 0).astype(jnp.int32)
    e_lo = jnp.asarray(CLASS_E_LO)[blk_cls]
    e_hi = jnp.asarray(CLASS_E_HI)[blk_cls]

    w_gate, w_up, w_down = (w_gate.reshape(N_EXPERTS, d, f), w_up.reshape(N_EXPERTS, d, f),
                            w_down.reshape(N_EXPERTS, f, d))
    w_lo = lambda shape: pl.BlockSpec((1,) + shape, lambda i, tok, elo, ehi, nv: (elo[i], 0, 0))
    w_hi = lambda shape: pl.BlockSpec((1,) + shape, lambda i, tok, elo, ehi, nv: (ehi[i], 0, 0))
    vec = pl.BlockSpec((1, d), lambda i, tok, elo, ehi, nv: (0, 0))
    grid_spec = pltpu.PrefetchScalarGridSpec(
        num_scalar_prefetch=4,
        grid=(n_blocks,),
        in_specs=[pl.BlockSpec(memory_space=pl.ANY),
                  w_lo((d, f)), w_lo((d, f)), w_lo((f, d)),
                  w_hi((d, f)), w_hi((d, f)), w_hi((f, d)),
                  vec, vec],
        out_specs=pl.BlockSpec(memory_space=pl.ANY),
        scratch_shapes=[pltpu.VMEM((bm + SUBLANES, ROW_W), jnp.float32), pltpu.VMEM((bm + SUBLANES, ROW_W), jnp.float32),
                        pltpu.VMEM((bm, d), jnp.float32), pltpu.VMEM((bm, d), jnp.float32),
                        pltpu.SMEM((n_rows + bm,), jnp.int32),
                        pltpu.SemaphoreType.DMA((2,)),
                        pltpu.SemaphoreType.DMA((2,))],
    )
    return pl.pallas_call(
        _expert_kernel,
        out_shape=jax.ShapeDtypeStruct((t, d), jnp.float32),
        grid_spec=grid_spec,
        compiler_params=pltpu.CompilerParams(dimension_semantics=("arbitrary",),
                                             vmem_limit_bytes=VMEM_LIMIT_BYTES),
        name="expert_pair_ffn",
    )(dest, e_lo, e_hi, nvalid, xext, w_gate, w_up, w_down, w_gate, w_up, w_down, ln_g, ln_b)


def kernel(x, conv_w_in, conv_w, conv_w_out, ml_w_in, ml_b_gate, ml_norm_g, ml_w_out, ln_mix_g, ln_mix_b,
           ln_ffn_g, ln_ffn_b, router_w, router_b, exp_w_gate, exp_w_up, exp_w_down):
    bsz, seq, d = x.shape
    assert d == D_MODEL and seq % MIX_TILE == 0 and (bsz * seq) % EXPERT_BLOCK == 0
    bf16 = jnp.bfloat16
    vec = lambda a: a.astype(jnp.float32)[None, :]
    (rw_hi, rw_lo), rb = _prep_router(router_w, router_b)
    x2d = x.reshape(bsz * seq, d).astype(jnp.float32)
    expert_weights = (exp_w_gate.astype(jnp.float32), exp_w_up.astype(jnp.float32),
                      exp_w_down.astype(jnp.float32))
    for i in range(DEPTH):
        j = i // 2
        if i % 2 == 0:
            xext, meta_t, counts, *w16 = _conv_mixer_layer(
                x2d, seq, conv_w_in[j].astype(bf16), _pad_rows(conv_w[j].astype(jnp.float32)),
                conv_w_out[j].astype(bf16), vec(ln_mix_g[i]), vec(ln_mix_b[i]), rw_hi, rw_lo, rb,
                i, expert_weights)
        else:
            xext, meta_t, counts, *w16 = _mlstm_mixer_layer(
                x2d, seq, ml_w_in[j], ml_b_gate[j], ml_norm_g[j].astype(jnp.float32), ml_w_out[j],
                vec(ln_mix_g[i]), vec(ln_mix_b[i]), rw_hi, rw_lo, rb, i, expert_weights)
        x2d = _moe_layer(xext, meta_t, counts, *w16, vec(ln_ffn_g[i]), vec(ln_ffn_b[i]))
    return x2d.reshape(bsz, seq, d).astype(x.dtype)
```

```python
import functools
import itertools

import numpy as np
import jax
import jax.numpy as jnp
from jax import lax
from jax.experimental import pallas as pl
from jax.experimental.pallas import tpu as pltpu

D_MODEL = 1024
DEPTH = 2
CONV_WIDTH = 3
ML_HEADS = 8
ML_QK_DIM = D_MODEL // (2 * ML_HEADS)
ML_V_DIM = D_MODEL // ML_HEADS
ML_QK_W = ML_HEADS * ML_QK_DIM
ML_V_W = ML_HEADS * ML_V_DIM
N_EXPERTS = 16
N_GROUPS = 4
EXPERTS_PER_GROUP = N_EXPERTS // N_GROUPS
D_EXPERT = 3 * D_MODEL // 2
ALPHA = (2 * DEPTH) ** 0.25
LN_EPS = 1e-5
HEAD_NORM_EPS = 1e-6

LANES = 128
SUBLANES = 8
VMEM_LIMIT_BYTES = 56 * 1024 * 1024

PAIRS = tuple(itertools.combinations(range(EXPERTS_PER_GROUP), 2))
N_PAIRS = len(PAIRS)
N_CLASSES = N_GROUPS * N_PAIRS
CLASS_ROWS = -(-N_CLASSES // 16) * 16
CLASS_E_LO = np.array([g * EXPERTS_PER_GROUP + a for g in range(N_GROUPS) for a, _ in PAIRS], np.int32)
CLASS_E_HI = np.array([g * EXPERTS_PER_GROUP + b for g in range(N_GROUPS) for _, b in PAIRS], np.int32)

META_G_LO, META_G_HI, META_CLASS, META_RANK = 0, 1, 2, 3
ROW_W = D_MODEL + LANES

MIX_TILE = 512
MIX_SUBTILES = 2
EXPERT_BLOCK = 256
GATE_CHUNKS = 6
ML_CHUNK = 128


def _layer_norm(z, g, b):
    mu = jnp.mean(z, axis=-1, keepdims=True)
    zc = z - mu
    var = jnp.mean(zc * zc, axis=-1, keepdims=True)
    return zc * lax.rsqrt(var + LN_EPS) * g + b


def _split_bf16(a):
    hi = a.astype(jnp.bfloat16)
    lo = (a - hi.astype(jnp.float32)).astype(jnp.bfloat16)
    return hi, lo


def _route_and_rank(x1, rw_hi, rw_lo, rb_col, count_ref):
    ts = x1.shape[0]
    f32, bf16 = jnp.float32, jnp.bfloat16
    n_g, n_m = N_GROUPS, EXPERTS_PER_GROUP

    x_hi, x_lo = _split_bf16(x1)
    logits = (jnp.dot(x_hi, rw_hi, preferred_element_type=f32)
              + jnp.dot(x_hi, rw_lo, preferred_element_type=f32)
              + jnp.dot(x_lo, rw_hi, preferred_element_type=f32))
    lt = logits.T[:N_EXPERTS, :]
    ex = jnp.exp(lt - jnp.max(lt, axis=0, keepdims=True))
    probs = ex / jnp.sum(ex, axis=0, keepdims=True)
    sel = probs + rb_col
    member = lambda a, j: a[j * n_g:(j + 1) * n_g, :]

    in_top2 = []
    for j in range(n_m):
        beaten_by = jnp.zeros((n_g, ts), jnp.int32)
        for k in range(n_m):
            if k != j:
                wins = member(sel, k) > member(sel, j)
                if k < j:
                    wins = wins | (member(sel, k) == member(sel, j))
                beaten_by = beaten_by + wins.astype(jnp.int32)
        in_top2.append(beaten_by < 2)
    score = sum(jnp.where(in_top2[j], member(sel, j), 0.0) for j in range(n_m))
    group = lax.broadcasted_iota(jnp.int32, (n_g, ts), 0)
    best = jnp.max(score, axis=0, keepdims=True)
    g_best = jnp.min(jnp.where(score == best, group, n_g), axis=0, keepdims=True)
    in_best = group == g_best
    picked = [jnp.max(jnp.where(in_top2[j] & in_best, 1, 0), axis=0, keepdims=True) for j in range(n_m)]
    p_pick = [jnp.sum(jnp.where(in_top2[j] & in_best, member(probs, j), 0.0), axis=0, keepdims=True)
              for j in range(n_m)]
    a = functools.reduce(jnp.minimum, [jnp.where(picked[j] > 0, j, n_m) for j in range(n_m)])
    b = functools.reduce(jnp.maximum, [jnp.where(picked[j] > 0, j, -1) for j in range(n_m)])
    p_lo = sum(jnp.where(a == j, p_pick[j], 0.0) for j in range(n_m))
    p_hi = sum(jnp.where(b == j, p_pick[j], 0.0) for j in range(n_m))
    g_lo = p_lo / (p_lo + p_hi)
    g_hi = p_hi / (p_lo + p_hi)
    pair = ((a * (2 * n_m - 1 - a)) >> 1) + (b - a - 1)
    cls = g_best * N_PAIRS + pair

    n_rows = count_ref.shape[0]
    onehot = jnp.where(lax.broadcasted_iota(jnp.int32, (n_rows, ts), 0) == cls, 1.0, 0.0)
    earlier = jnp.where(lax.broadcasted_iota(jnp.int32, (ts, ts), 0)
                        < lax.broadcasted_iota(jnp.int32, (ts, ts), 1), 1.0, 0.0).astype(bf16)
    before = jnp.dot(onehot.astype(bf16), earlier, preferred_element_type=f32)
    running = count_ref[...]
    running_ts = jnp.concatenate([running] * (ts // LANES), axis=1)
    rank = jnp.sum(onehot * (before + running_ts), axis=0, keepdims=True)
    count_ref[...] = running + jnp.dot(onehot.astype(bf16), jnp.ones((ts, LANES), bf16),
                                       preferred_element_type=f32)
    return jnp.concatenate([g_lo, g_hi, cls.astype(f32), rank, jnp.zeros((SUBLANES - 4, ts), f32)], axis=0)


def _cast_slabs(w32_refs, w16_refs):
    for w32_ref, w16_ref in zip(w32_refs, w16_refs):
        w16_ref[...] = w32_ref[0].astype(w16_ref.dtype)


def _cast_slab_specs(layer, n_steps, weights):
    args, in_specs, out_shapes, out_specs = [], [], [], []
    for w in weights:
        n_layers, n_exp, rows, cols = w.shape
        slab, rem = divmod(n_exp * rows, n_steps)
        assert rem == 0 and slab % (2 * SUBLANES) == 0
        args.append(w.reshape(n_layers, n_exp * rows, cols))
        in_specs.append(pl.BlockSpec((1, slab, cols), lambda i: (layer, i, 0)))
        out_shapes.append(jax.ShapeDtypeStruct((n_exp * rows, cols), jnp.bfloat16))
        out_specs.append(pl.BlockSpec((slab, cols), lambda i: (i, 0)))
    return args, in_specs, out_shapes, out_specs


def _conv_mixer_kernel(x_ref, w_in_ref, w_conv_ref, w_out_ref, ln_g_ref, ln_b_ref,
                       rw_hi_ref, rw_lo_ref, rb_ref, wg32_ref, wu32_ref, wd32_ref,
                       out_ref, meta_t_ref, count_ref, wg16_ref, wu16_ref, wd16_ref, carry_ref, *, tiles_per_seq):
    i = pl.program_id(0)

    @pl.when(i == 0)
    def _():
        count_ref[...] = jnp.zeros_like(count_ref)

    @pl.when(i % tiles_per_seq == 0)
    def _():
        carry_ref[...] = jnp.zeros_like(carry_ref)

    f32 = jnp.float32
    d = D_MODEL
    ts = x_ref.shape[0]
    sub = ts // MIX_SUBTILES
    xs = [x_ref[k * sub:(k + 1) * sub, :] for k in range(MIX_SUBTILES)]
    projs = [jnp.dot(x.astype(jnp.bfloat16), w_in_ref[...], preferred_element_type=f32) for x in xs]
    row = lax.broadcasted_iota(jnp.int32, (sub, d), 0)
    wc = w_conv_ref[...]
    prev = carry_ref[...]
    gated = []
    for proj in projs:
        gate_b, gate_c, h = proj[:, :d], proj[:, d:2 * d], proj[:, 2 * d:]
        u = gate_c * h
        u1 = jnp.where(row == 0, prev[SUBLANES - 1:SUBLANES, :], pltpu.roll(u, 1, axis=0))
        u2 = jnp.where(row == 0, prev[SUBLANES - 2:SUBLANES - 1, :],
                       jnp.where(row == 1, prev[SUBLANES - 1:SUBLANES, :], pltpu.roll(u, 2, axis=0)))
        prev = u[sub - SUBLANES:, :]
        conv = wc[0:1, :] * u2 + wc[1:2, :] * u1 + wc[2:3, :] * u
        gated.append((gate_b * conv).astype(jnp.bfloat16))
    carry_ref[...] = prev
    mixes = [jnp.dot(v, w_out_ref[...], preferred_element_type=f32) for v in gated]
    x1 = jnp.concatenate([_layer_norm(ALPHA * x + mix, ln_g_ref[...], ln_b_ref[...])
                          for x, mix in zip(xs, mixes)], axis=0)
    meta = _route_and_rank(x1, rw_hi_ref[...], rw_lo_ref[...], rb_ref[...], count_ref)
    out_ref[:, :d] = x1
    out_ref[:, d:] = jnp.concatenate([meta, jnp.zeros((LANES - SUBLANES, meta.shape[1]), meta.dtype)], axis=0).T
    meta_t_ref[...] = meta
    _cast_slabs((wg32_ref, wu32_ref, wd32_ref), (wg16_ref, wu16_ref, wd16_ref))


def _const_spec(shape):
    return pl.BlockSpec(shape, lambda i: (0,) * len(shape))


def _pad_rows(a, rows=SUBLANES):
    return jnp.pad(a, ((0, rows - a.shape[0]), (0, 0)))


def _prep_router(router_w, router_b):
    order = np.array([g * EXPERTS_PER_GROUP + j for j in range(EXPERTS_PER_GROUP) for g in range(N_GROUPS)])
    rw = jnp.pad(router_w.astype(jnp.float32)[:, order], ((0, 0), (0, LANES - N_EXPERTS)))
    rb = router_b.astype(jnp.float32)[order][:, None]
    return _split_bf16(rw), rb


def _conv_mixer_layer(x2d, seq, w_in, w_conv, w_out, ln_g, ln_b, rw_hi, rw_lo, rb, layer, expert_weights):
    t, d = x2d.shape
    ts = MIX_TILE
    grid = (t // ts,)
    cast_args, cast_in, cast_shapes, cast_out = _cast_slab_specs(layer, grid[0], expert_weights)
    return pl.pallas_call(
        functools.partial(_conv_mixer_kernel, tiles_per_seq=seq // ts),
        out_shape=(jax.ShapeDtypeStruct((t, ROW_W), jnp.float32),
                   jax.ShapeDtypeStruct((SUBLANES, t), jnp.float32),
                   jax.ShapeDtypeStruct((CLASS_ROWS, LANES), jnp.float32), *cast_shapes),
        grid=grid,
        in_specs=[pl.BlockSpec((ts, d), lambda i: (i, 0)),
                  _const_spec((d, 3 * d)), _const_spec((SUBLANES, d)), _const_spec((d, d)),
                  _const_spec((1, d)), _const_spec((1, d)),
                  _const_spec((d, LANES)), _const_spec((d, LANES)), _const_spec((N_EXPERTS, 1)), *cast_in],
        out_specs=(pl.BlockSpec((ts, ROW_W), lambda i: (i, 0)),
                   pl.BlockSpec((SUBLANES, ts), lambda i: (0, i)),
                   _const_spec((CLASS_ROWS, LANES)), *cast_out),
        scratch_shapes=[pltpu.VMEM((SUBLANES, d), jnp.float32)],
        compiler_params=pltpu.CompilerParams(dimension_semantics=("arbitrary",),
                                             vmem_limit_bytes=VMEM_LIMIT_BYTES),
        name="conv_mixer_route",
    )(x2d, w_in, w_conv, w_out, ln_g, ln_b, rw_hi, rw_lo, rb, *cast_args)


def _log_sigmoid(z):
    return jnp.minimum(z, 0.0) - jnp.log1p(jnp.exp(-jnp.abs(z)))


def _split3_f32(a):
    f32, bf16 = jnp.float32, jnp.bfloat16
    hi = a.astype(bf16).astype(f32)
    r1 = a - hi
    mid = r1.astype(bf16).astype(f32)
    lo = (r1 - mid).astype(bf16).astype(f32)
    return hi, mid, lo


_COL_U, _COL_ONE, _COL_W, _COL_EMT, _COL_END = 0, 24, 48, 64, 80


def _mlstm_mixer_kernel(x_ref, w_qvo_ref, wkt_ref, wgt_hi_ref, wgt_lo_ref, bg_row_ref, norm_g_ref, w_out_ref,
                        ln_g_ref, ln_b_ref, rw_hi_ref, rw_lo_ref, rb_ref, wg32_ref, wu32_ref, wd32_ref,
                        out_ref, meta_t_ref, count_ref, wg16_ref, wu16_ref, wd16_ref,
                        proj_ref, kt_ref, grow_ref, colmat_ref, bexp_ref, h_ref, c_ref, m_ref, *, tiles_per_seq):
    i = pl.program_id(0)
    f32, bf16 = jnp.float32, jnp.bfloat16
    nh, dk, dv, d = ML_HEADS, ML_QK_DIM, ML_V_DIM, D_MODEL
    L = ML_CHUNK
    ts = x_ref.shape[0]
    n_chunks = ts // L
    o_base = ML_QK_W + ML_V_W

    @pl.when(i == 0)
    def _():
        count_ref[...] = jnp.zeros_like(count_ref)

    @pl.when(i % tiles_per_seq == 0)
    def _():
        c_ref[...] = jnp.zeros_like(c_ref)
        m_ref[...] = jnp.zeros_like(m_ref)

    x = x_ref[...]
    x_hi, x_lo = _split_bf16(x)
    nt = (((1,), (1,)), ((), ()))
    proj_ref[...] = jnp.dot(x_hi, w_qvo_ref[...], preferred_element_type=f32)
    k_t = lax.dot_general(wkt_ref[...], x_hi, nt, preferred_element_type=f32)
    g_row = (lax.dot_general(wgt_hi_ref[...], x_hi, nt, preferred_element_type=f32)
             + lax.dot_general(wgt_lo_ref[...], x_hi, nt, preferred_element_type=f32)
             + lax.dot_general(wgt_hi_ref[...], x_lo, nt, preferred_element_type=f32)) + bg_row_ref[...]
    r_i = lax.broadcasted_iota(jnp.int32, (ts, ts), 0)
    c_i = lax.broadcasted_iota(jnp.int32, (ts, ts), 1)
    tri_row = jnp.where(((r_i // L) == (c_i // L)) & (r_i <= c_i), 1.0, 0.0).astype(bf16)
    lf_hi, lf_lo = _split_bf16(_log_sigmoid(g_row))
    cum_row = (jnp.dot(lf_hi, tri_row, preferred_element_type=f32)
               + jnp.dot(lf_lo, tri_row, preferred_element_type=f32))
    i_row, b_row = g_row[:nh, :], cum_row[nh:, :]

    g = i_row - b_row
    lane_in_chunk = lax.broadcasted_iota(jnp.int32, (nh, ts), 1) & (L - 1)
    cm = g
    shift = 1
    while shift < L:
        cm = jnp.maximum(cm, jnp.where(lane_in_chunk >= shift, pltpu.roll(cm, shift, axis=1), -jnp.inf))
        shift *= 2
    m_prev = m_ref[:, 0:1]
    u_parts, w_parts, gr_parts, gs_parts = [], [], [], []
    for c in range(n_chunks):
        sl = slice(c * L, (c + 1) * L)
        u_c = jnp.maximum(m_prev, cm[:, sl])
        b_last = b_row[:, (c + 1) * L - 1:(c + 1) * L]
        m_new = b_last + u_c[:, L - 1:L]
        u_parts.append(u_c)
        w_parts.append(jnp.exp(m_prev - u_c))
        gr_parts.append(jnp.exp(b_last - b_row[:, sl] + i_row[:, sl] - m_new))
        gs_parts.append(jnp.broadcast_to(jnp.exp(b_last + m_prev - m_new), (nh, L)))
        m_prev = m_new
    m_ref[...] = jnp.broadcast_to(m_prev, m_ref.shape)
    u = jnp.concatenate(u_parts, axis=1)
    w_inter = jnp.concatenate(w_parts, axis=1)
    emt = jnp.exp(-(b_row + u))
    w_hi = w_inter.astype(bf16).astype(f32)
    e_hi = emt.astype(bf16).astype(f32)
    col_rows = jnp.concatenate(
        [*_split3_f32(u), jnp.ones((_COL_W - _COL_ONE, ts), f32), w_hi, w_inter - w_hi, e_hi, emt - e_hi,
         jnp.zeros((LANES - _COL_END, ts), f32)], axis=0)
    colmat_ref[...] = col_rows.T.astype(bf16)
    b_exp = jnp.concatenate([jnp.full((_COL_ONE - _COL_U, ts), -1.0, f32), *_split3_f32(g),
                             jnp.zeros((LANES - _COL_W, ts), f32)], axis=0).astype(bf16)
    for c in range(n_chunks):
        sl = slice(c * L, (c + 1) * L)
        bexp_ref[c] = b_exp[:, sl]
        grow_ref[c] = jnp.concatenate([gr_parts[c], gs_parts[c]], axis=0)
        kt_ref[c] = k_t[:, sl]

    causal = lax.broadcasted_iota(jnp.int32, (L, L), 0) >= lax.broadcasted_iota(jnp.int32, (L, L), 1)
    ones_col = jnp.where(lax.broadcasted_iota(jnp.int32, (L, dv), 1) == 0, 1.0, 0.0).astype(bf16)
    lane_ll = lax.broadcasted_iota(jnp.int32, (L, LANES), 1)
    sel_r = lax.broadcasted_iota(jnp.int32, (LANES, 2 * LANES), 0)
    sel_c = lax.broadcasted_iota(jnp.int32, (LANES, 2 * LANES), 1)
    is_w_row = (sel_r >= _COL_W) & (sel_r < _COL_EMT)
    is_e_row = (sel_r >= _COL_EMT) & (sel_r < _COL_END)
    b_sel = jnp.where((is_w_row & (sel_c < LANES)) | (is_e_row & (sel_c >= LANES)), 1.0, 0.0).astype(bf16)
    r2_r = lax.broadcasted_iota(jnp.int32, (2 * dv, 2 * dv), 0)
    r2_c = lax.broadcasted_iota(jnp.int32, (2 * dv, 2 * dv), 1)
    rhs2 = jnp.where((r2_r < dv) & (r2_c < dv), 1.0 / dv,
                     jnp.where((r2_r == dv) & (r2_c >= dv), 1.0, 0.0)).astype(bf16)

    def chunk_body(c, carry):
        r0 = pl.multiple_of(c * L, L)
        gr = grow_ref[c]
        colmat = colmat_ref[pl.ds(r0, L), :]
        rhs1 = jnp.concatenate([bexp_ref[c], b_sel], axis=1)
        heads = range(nh)
        c_pairs = [c_ref[p] for p in range(nh // 2)]
        kt_pairs = [kt_ref[c, p * LANES:(p + 1) * LANES, :] for p in range(nh // 2)]

        ew, q_m, s_mat, v_ext = [], [], [], []
        for h in heads:
            p, hh = divmod(h, 2)
            lhs = jnp.where((lane_ll & (nh - 1)) == h, colmat, 0)
            ew.append(jnp.dot(lhs, rhs1, preferred_element_type=f32))
            q2 = proj_ref[pl.ds(r0, L), p * LANES:(p + 1) * LANES] * (dk ** -0.5)
            q_m.append(jnp.where((lane_ll >= hh * dk) & (lane_ll < (hh + 1) * dk), q2, 0.0))
            s_mat.append(jnp.dot(q_m[h].astype(bf16), kt_pairs[p].astype(bf16), preferred_element_type=f32))
            v = proj_ref[pl.ds(r0, L), ML_QK_W + h * dv:ML_QK_W + (h + 1) * dv].astype(bf16)
            v_ext.append(jnp.concatenate([v, ones_col], axis=1))
        tot = []
        for h in heads:
            p, hh = divmod(h, 2)
            a = jnp.exp(jnp.where(causal, ew[h][:, :L], -jnp.inf)) * s_mat[h]
            qw = q_m[h] * ew[h][:, L:L + LANES]
            lhs = jnp.concatenate([a.astype(bf16), qw.astype(bf16)], axis=1)
            rhs = jnp.concatenate([v_ext[h], c_pairs[p].astype(bf16)], axis=0)
            tot.append(jnp.dot(lhs, rhs, preferred_element_type=f32))
        for h in heads:
            num = tot[h][:, :dv]
            lhs2 = jnp.concatenate([(num * num).astype(bf16), tot[h][:, dv:].astype(bf16)], axis=1)
            r2 = jnp.dot(lhs2, rhs2, preferred_element_type=f32)
            inv = 1.0 / jnp.maximum(jnp.abs(r2[:, dv:]), ew[h][:, L + LANES:])
            hn = (num * inv) * lax.rsqrt(r2[:, :dv] * inv * inv + HEAD_NORM_EPS)
            o_pre = proj_ref[pl.ds(r0, L), o_base + h * dv:o_base + (h + 1) * dv]
            h_ref[pl.ds(r0, L), h * dv:(h + 1) * dv] = (
                jax.nn.sigmoid(o_pre) * hn * norm_g_ref[:, h * dv:(h + 1) * dv])
        for h in heads:
            p, hh = divmod(h, 2)
            rows = slice(hh * dk, (hh + 1) * dk)
            kg = (kt_pairs[p][rows, :] * gr[h:h + 1, :]).astype(bf16)
            gs = gr[nh + h:nh + h + 1, :]
            c_ref[p, rows, :] = (jnp.concatenate([gs, gs], axis=1) * c_pairs[p][rows, :]
                                 + jnp.dot(kg, v_ext[h], preferred_element_type=f32))
        return carry

    lax.fori_loop(0, n_chunks, chunk_body, 0)

    sub = ts // MIX_SUBTILES
    mixes = [jnp.dot(h_ref[k * sub:(k + 1) * sub, :].astype(bf16), w_out_ref[...], preferred_element_type=f32)
             for k in range(MIX_SUBTILES)]
    x1 = jnp.concatenate([_layer_norm(ALPHA * x[k * sub:(k + 1) * sub, :] + mixes[k], ln_g_ref[...], ln_b_ref[...])
                          for k in range(MIX_SUBTILES)], axis=0)
    meta = _route_and_rank(x1, rw_hi_ref[...], rw_lo_ref[...], rb_ref[...], count_ref)
    out_ref[:, :d] = x1
    out_ref[:, d:] = jnp.concatenate([meta, jnp.zeros((LANES - SUBLANES, meta.shape[1]), meta.dtype)], axis=0).T
    meta_t_ref[...] = meta
    _cast_slabs((wg32_ref, wu32_ref, wd32_ref), (wg16_ref, wu16_ref, wd16_ref))


def _mlstm_mixer_layer(x2d, seq, w_in, b_gate, norm_g, w_out, ln_g, ln_b, rw_hi, rw_lo, rb, layer,
                       expert_weights):
    t, d = x2d.shape
    ts = MIX_TILE
    nh = ML_HEADS
    cast_args, cast_in, cast_shapes, cast_out = _cast_slab_specs(layer, t // ts, expert_weights)
    assert ML_CHUNK == LANES and ts % ML_CHUNK == 0 and 2 * ML_QK_DIM == LANES and ML_V_DIM == LANES
    n_qkvo = 2 * ML_QK_W + 2 * ML_V_W
    n_qvo = ML_QK_W + 2 * ML_V_W
    w_qvo = jnp.concatenate([w_in[:, :ML_QK_W], w_in[:, 2 * ML_QK_W:n_qkvo]], axis=1).astype(jnp.bfloat16)
    wkt = w_in[:, ML_QK_W:2 * ML_QK_W].T.astype(jnp.bfloat16)
    w_g = w_in[:, n_qkvo:].astype(jnp.float32)
    wgt_hi, wgt_lo = _split_bf16(w_g.T)
    bg_row = b_gate.astype(jnp.float32)[:, None]
    return pl.pallas_call(
        functools.partial(_mlstm_mixer_kernel, tiles_per_seq=seq // ts),
        out_shape=(jax.ShapeDtypeStruct((t, ROW_W), jnp.float32),
                   jax.ShapeDtypeStruct((SUBLANES, t), jnp.float32),
                   jax.ShapeDtypeStruct((CLASS_ROWS, LANES), jnp.float32), *cast_shapes),
        grid=(t // ts,),
        in_specs=[pl.BlockSpec((ts, d), lambda i: (i, 0)),
                  _const_spec((d, n_qvo)), _const_spec((ML_QK_W, d)),
                  _const_spec((2 * nh, d)), _const_spec((2 * nh, d)),
                  _const_spec((2 * nh, 1)),
                  _const_spec((1, d)), _const_spec((d, d)),
                  _const_spec((1, d)), _const_spec((1, d)),
                  _const_spec((d, LANES)), _const_spec((d, LANES)), _const_spec((N_EXPERTS, 1)), *cast_in],
        out_specs=(pl.BlockSpec((ts, ROW_W), lambda i: (i, 0)),
                   pl.BlockSpec((SUBLANES, ts), lambda i: (0, i)),
                   _const_spec((CLASS_ROWS, LANES)), *cast_out),
        scratch_shapes=[pltpu.VMEM((ts, n_qvo), jnp.float32),
                        pltpu.VMEM((ts // ML_CHUNK, ML_QK_W, ML_CHUNK), jnp.float32),
                        pltpu.VMEM((ts // ML_CHUNK, 2 * nh, ML_CHUNK), jnp.float32),
                        pltpu.VMEM((ts, LANES), jnp.bfloat16),
                        pltpu.VMEM((ts // ML_CHUNK, LANES, ML_CHUNK), jnp.bfloat16),
                        pltpu.VMEM((ts, d), jnp.float32),
                        pltpu.VMEM((nh // 2, 2 * ML_QK_DIM, 2 * ML_V_DIM), jnp.float32),
                        pltpu.VMEM((nh, ML_CHUNK), jnp.float32)],
        compiler_params=pltpu.CompilerParams(dimension_semantics=("arbitrary",),
                                             vmem_limit_bytes=VMEM_LIMIT_BYTES),
        name="mlstm_mixer_route",
    )(x2d, w_qvo, wkt, wgt_hi, wgt_lo, bg_row, norm_g[None, :], w_out.astype(jnp.bfloat16),
      ln_g, ln_b, rw_hi, rw_lo, rb, *cast_args)


def _expert_kernel(dest_ref, elo_ref, ehi_ref, nvalid_ref,
                   x_hbm, wg_lo_ref, wu_lo_ref, wd_lo_ref, wg_hi_ref, wu_hi_ref, wd_hi_ref,
                   ln_g_ref, ln_b_ref, out_hbm, xbuf0, xbuf1, obuf0, obuf1, tok_ref, gather_sem, scatter_sem):
    i = pl.program_id(0)
    nb = pl.num_programs(0)
    bm = obuf0.shape[0]
    d = D_MODEL

    @pl.when(i == 0)
    def _():
        def pad_block(b, carry):
            def pad_row(r, c):
                tok_ref[b * bm + r] = 0
                return c
            first = jnp.where(b < nb, nvalid_ref[jnp.minimum(b, nb - 1)], 0)
            return lax.fori_loop(first, bm, pad_row, carry)

        def place(t, c):
            tok_ref[dest_ref[t]] = t
            return c

        lax.fori_loop(0, nb + 1, pad_block, 0)
        lax.fori_loop(0, x_hbm.shape[0], place, 0, unroll=8)
    xbufs, obufs = (xbuf0, xbuf1), (obuf0, obuf1)
    nv = nvalid_ref[i]
    nv_prev = nvalid_ref[jnp.maximum(i - 1, 0)]

    def start_gather(j, s, rows=range(bm)):
        for r in rows:
            tok = tok_ref[j * bm + r]
            pltpu.make_async_copy(x_hbm.at[pl.ds(tok, 1)], xbufs[s].at[pl.ds(r, 1)], gather_sem.at[s]).start()

    def wait_gather(s):
        pltpu.make_async_copy(xbufs[s].at[pl.ds(0, bm)], xbufs[s].at[pl.ds(0, bm)], gather_sem.at[s]).wait()

    def scatter_copy(j, s, r):
        tok = tok_ref[j * bm + r]
        return pltpu.make_async_copy(obufs[s].at[pl.ds(r, 1)], out_hbm.at[pl.ds(tok, 1)], scatter_sem.at[s])

    def wait_scatter(s, n):
        @pl.when(n == bm)
        def _():
            pltpu.make_async_copy(obufs[s], obufs[s], scatter_sem.at[s]).wait()

        @pl.when(n < bm)
        def _():
            def body(r, c):
                pltpu.make_async_copy(obufs[s].at[pl.ds(0, 1)], obufs[s].at[pl.ds(0, 1)], scatter_sem.at[s]).wait()
                return c
            lax.fori_loop(0, n, body, 0)

    @pl.when(i == 0)
    def _():
        for buf in xbufs:
            buf[bm:, :] = jnp.zeros((buf.shape[0] - bm, buf.shape[1]), buf.dtype)
        start_gather(0, 0)

    def step(s):
        @pl.when((i == 0) | (nv_prev > 0))
        def _():
            wait_gather(s)

        def compute(rows):
            xb = xbufs[s][0:rows, :]
            x = xb[:, :d]
            g_lo = xb[:, d + META_G_LO:d + META_G_LO + 1]
            g_hi = xb[:, d + META_G_HI:d + META_G_HI + 1]
            x16 = x.astype(jnp.bfloat16)
            n_portions = 2 * GATE_CHUNKS

            def gate_proj(w_ref, k0):
                g = jnp.dot(x16, w_ref[0], preferred_element_type=jnp.float32)
                width = g.shape[1] // GATE_CHUNKS
                chunks = []
                for c in range(GATE_CHUNKS):
                    k = k0 + c
                    start_gather(i + 1, 1 - s, range(k * bm // n_portions, (k + 1) * bm // n_portions))
                    zero_row = xbufs[1 - s][bm:bm + 1, 0:LANES]
                    chunks.append(g[:, c * width:(c + 1) * width]
                                  + jnp.concatenate([zero_row] * (width // LANES), axis=1))
                return jnp.concatenate(chunks, axis=1)

            def ffn(k0, wg_ref, wu_ref, wd_ref):
                g = gate_proj(wg_ref, k0)
                u = jnp.dot(x16, wu_ref[0], preferred_element_type=jnp.float32)
                h = (g * jax.nn.sigmoid(g)) * u
                return jnp.dot(h.astype(jnp.bfloat16), wd_ref[0], preferred_element_type=jnp.float32)

            y = (g_lo * ffn(0, wg_lo_ref, wu_lo_ref, wd_lo_ref)
                 + g_hi * ffn(GATE_CHUNKS, wg_hi_ref, wu_hi_ref, wd_hi_ref))
            obufs[s][0:rows, :] = _layer_norm(ALPHA * x + y, ln_g_ref[...], ln_b_ref[...])

        def scatter_real_rows():
            def body(r, c):
                scatter_copy(i, s, r).start()
                return c
            lax.fori_loop(0, nv, body, 0)

        @pl.when(nv > bm // 2)
        def _():
            compute(bm)

            @pl.when(nv == bm)
            def _():
                for r in range(bm):
                    scatter_copy(i, s, r).start()

            @pl.when(nv < bm)
            def _():
                scatter_real_rows()

        @pl.when((nv > 0) & (nv <= bm // 2))
        def _():
            compute(bm // 2)
            scatter_real_rows()

        @pl.when(i > 0)
        def _():
            wait_scatter(1 - s, nv_prev)

        @pl.when(i == nb - 1)
        def _():
            wait_scatter(s, nv)

            @pl.when(nv > 0)
            def _():
                wait_gather(1 - s)

    for s in range(2):
        pl.when(i % 2 == s)(functools.partial(step, s))


def _moe_layer(xext, meta_t, counts, w_gate, w_up, w_down, ln_g, ln_b):
    t = xext.shape[0]
    d, f, bm = D_MODEL, D_EXPERT, EXPERT_BLOCK
    n_blocks = t // bm + N_CLASSES
    n_rows = n_blocks * bm

    cls = meta_t[META_CLASS].astype(jnp.int32)
    rank = meta_t[META_RANK].astype(jnp.int32)
    cnt = counts[:N_CLASSES, 0].astype(jnp.int32)
    cls_blocks = (cnt + bm - 1) // bm
    blk_end = jnp.cumsum(cls_blocks)
    blk_start = blk_end - cls_blocks
    total_blocks = blk_end[-1]
    row_start = jnp.sum(jnp.where(cls[None, :] == jnp.arange(N_CLASSES, dtype=jnp.int32)[:, None],
                                  blk_start[:, None] * bm, 0), axis=0)
    dest = (row_start + rank).astype(jnp.int32)
    blk = jnp.arange(n_blocks, dtype=jnp.int32)
    blk_cls = jnp.sum(jnp.minimum(blk, total_blocks - 1)[:, None] >= blk_end[None, :], axis=1).astype(jnp.int32)
    blk_cls = jnp.minimum(blk_cls, N_CLASSES - 1)
    nvalid = jnp.clip(cnt[blk_cls] - (blk - blk_start[blk_cls]) * bm, 0, bm)
    nvalid = jnp.where(blk < total_blocks, nvalid, 0).astype(jnp.int32)
    e_lo = jnp.asarray(CLASS_E_LO)[blk_cls]
    e_hi = jnp.asarray(CLASS_E_HI)[blk_cls]

    w_gate, w_up, w_down = (w_gate.reshape(N_EXPERTS, d, f), w_up.reshape(N_EXPERTS, d, f),
                            w_down.reshape(N_EXPERTS, f, d))
    w_lo = lambda shape: pl.BlockSpec((1,) + shape, lambda i, tok, elo, ehi, nv: (elo[i], 0, 0))
    w_hi = lambda shape: pl.BlockSpec((1,) + shape, lambda i, tok, elo, ehi, nv: (ehi[i], 0, 0))
    vec = pl.BlockSpec((1, d), lambda i, tok, elo, ehi, nv: (0, 0))
    grid_spec = pltpu.PrefetchScalarGridSpec(
        num_scalar_prefetch=4,
        grid=(n_blocks,),
        in_specs=[pl.BlockSpec(memory_space=pl.ANY),
                  w_lo((d, f)), w_lo((d, f)), w_lo((f, d)),
                  w_hi((d, f)), w_hi((d, f)), w_hi((f, d)),
                  vec, vec],
        out_specs=pl.BlockSpec(memory_space=pl.ANY),
        scratch_shapes=[pltpu.VMEM((bm + SUBLANES, ROW_W), jnp.float32), pltpu.VMEM((bm + SUBLANES, ROW_W), jnp.float32),
                        pltpu.VMEM((bm, d), jnp.float32), pltpu.VMEM((bm, d), jnp.float32),
                        pltpu.SMEM((n_rows + bm,), jnp.int32),
                        pltpu.SemaphoreType.DMA((2,)),
                        pltpu.SemaphoreType.DMA((2,))],
    )
    return pl.pallas_call(
        _expert_kernel,
        out_shape=jax.ShapeDtypeStruct((t, d), jnp.float32),
        grid_spec=grid_spec,
        compiler_params=pltpu.CompilerParams(dimension_semantics=("arbitrary",),
                                             vmem_limit_bytes=VMEM_LIMIT_BYTES),
        name="expert_pair_ffn",
    )(dest, e_lo, e_hi, nvalid, xext, w_gate, w_up, w_down, w_gate, w_up, w_down, ln_g, ln_b)


def kernel(x, conv_w_in, conv_w, conv_w_out, ml_w_in, ml_b_gate, ml_norm_g, ml_w_out, ln_mix_g, ln_mix_b,
           ln_ffn_g, ln_ffn_b, router_w, router_b, exp_w_gate, exp_w_up, exp_w_down):
    bsz, seq, d = x.shape
    assert d == D_MODEL and seq % MIX_TILE == 0 and (bsz * seq) % EXPERT_BLOCK == 0
    bf16 = jnp.bfloat16
    vec = lambda a: a.astype(jnp.float32)[None, :]
    (rw_hi, rw_lo), rb = _prep_router(router_w, router_b)
    x2d = x.reshape(bsz * seq, d).astype(jnp.float32)
    expert_weights = (exp_w_gate.astype(jnp.float32), exp_w_up.astype(jnp.float32),
                      exp_w_down.astype(jnp.float32))
    for i in range(DEPTH):
        j = i // 2
        if i % 2 == 0:
            xext, meta_t, counts, *w16 = _conv_mixer_layer(
                x2d, seq, conv_w_in[j].astype(bf16), _pad_rows(conv_w[j].astype(jnp.float32)),
                conv_w_out[j].astype(bf16), vec(ln_mix_g[i]), vec(ln_mix_b[i]), rw_hi, rw_lo, rb,
                i, expert_weights)
        else:
            xext, meta_t, counts, *w16 = _mlstm_mixer_layer(
                x2d, seq, ml_w_in[j], ml_b_gate[j], ml_norm_g[j].astype(jnp.float32), ml_w_out[j],
                vec(ln_mix_g[i]), vec(ln_mix_b[i]), rw_hi, rw_lo, rb, i, expert_weights)
        x2d = _moe_layer(xext, meta_t, counts, *w16, vec(ln_ffn_g[i]), vec(ln_ffn_b[i]))
    return x2d.reshape(bsz, seq, d).astype(x.dtype)
```

```python
import functools
import itertools

import numpy as np
import jax
import jax.numpy as jnp
from jax import lax
from jax.experimental import pallas as pl
from jax.experimental.pallas import tpu as pltpu

D_MODEL = 1024
DEPTH = 2
CONV_WIDTH = 3
ML_HEADS = 8
ML_QK_DIM = D_MODEL // (2 * ML_HEADS)
ML_V_DIM = D_MODEL // ML_HEADS
ML_QK_W = ML_HEADS * ML_QK_DIM
ML_V_W = ML_HEADS * ML_V_DIM
N_EXPERTS = 16
N_GROUPS = 4
EXPERTS_PER_GROUP = N_EXPERTS // N_GROUPS
D_EXPERT = 3 * D_MODEL // 2
ALPHA = (2 * DEPTH) ** 0.25
LN_EPS = 1e-5
HEAD_NORM_EPS = 1e-6

LANES = 128
SUBLANES = 8
VMEM_LIMIT_BYTES = 56 * 1024 * 1024

PAIRS = tuple(itertools.combinations(range(EXPERTS_PER_GROUP), 2))
N_PAIRS = len(PAIRS)
N_CLASSES = N_GROUPS * N_PAIRS
CLASS_ROWS = -(-N_CLASSES // 16) * 16
CLASS_E_LO = np.array([g * EXPERTS_PER_GROUP + a for g in range(N_GROUPS) for a, _ in PAIRS], np.int32)
CLASS_E_HI = np.array([g * EXPERTS_PER_GROUP + b for g in range(N_GROUPS) for _, b in PAIRS], np.int32)

META_G_LO, META_G_HI, META_CLASS, META_RANK = 0, 1, 2, 3
ROW_W = D_MODEL + LANES

MIX_TILE = 512
MIX_SUBTILES = 2
EXPERT_BLOCK = 256
GATE_CHUNKS = 6
ML_CHUNK = 128


def _layer_norm(z, g, b):
    mu = jnp.mean(z, axis=-1, keepdims=True)
    zc = z - mu
    var = jnp.mean(zc * zc, axis=-1, keepdims=True)
    return zc * lax.rsqrt(var + LN_EPS) * g + b


def _split_bf16(a):
    hi = a.astype(jnp.bfloat16)
    lo = (a - hi.astype(jnp.float32)).astype(jnp.bfloat16)
    return hi, lo


def _route_and_rank(x1, rw_hi, rw_lo, rb_col, count_ref):
    ts = x1.shape[0]
    f32, bf16 = jnp.float32, jnp.bfloat16
    n_g, n_m = N_GROUPS, EXPERTS_PER_GROUP

    x_hi, x_lo = _split_bf16(x1)
    hi_terms = jnp.dot(x_hi, jnp.concatenate([rw_hi, rw_lo], axis=1), preferred_element_type=f32)
    logits = (hi_terms[:, :LANES] + hi_terms[:, LANES:]
              + jnp.dot(x_lo, rw_hi, preferred_element_type=f32))
    lt = logits.T[:N_EXPERTS, :]
    ex = jnp.exp(lt - jnp.max(lt, axis=0, keepdims=True))
    probs = ex / jnp.sum(ex, axis=0, keepdims=True)
    sel = probs + rb_col
    member = lambda a, j: a[j * n_g:(j + 1) * n_g, :]

    in_top2 = []
    for j in range(n_m):
        beaten_by = jnp.zeros((n_g, ts), jnp.int32)
        for k in range(n_m):
            if k != j:
                wins = member(sel, k) > member(sel, j)
                if k < j:
                    wins = wins | (member(sel, k) == member(sel, j))
                beaten_by = beaten_by + wins.astype(jnp.int32)
        in_top2.append(beaten_by < 2)
    score = sum(jnp.where(in_top2[j], member(sel, j), 0.0) for j in range(n_m))
    group = lax.broadcasted_iota(jnp.int32, (n_g, ts), 0)
    best = jnp.max(score, axis=0, keepdims=True)
    g_best = jnp.min(jnp.where(score == best, group, n_g), axis=0, keepdims=True)
    in_best = group == g_best
    picked = [jnp.max(jnp.where(in_top2[j] & in_best, 1, 0), axis=0, keepdims=True) for j in range(n_m)]
    p_pick = [jnp.sum(jnp.where(in_top2[j] & in_best, member(probs, j), 0.0), axis=0, keepdims=True)
              for j in range(n_m)]
    a = functools.reduce(jnp.minimum, [jnp.where(picked[j] > 0, j, n_m) for j in range(n_m)])
    b = functools.reduce(jnp.maximum, [jnp.where(picked[j] > 0, j, -1) for j in range(n_m)])
    p_lo = sum(jnp.where(a == j, p_pick[j], 0.0) for j in range(n_m))
    p_hi = sum(jnp.where(b == j, p_pick[j], 0.0) for j in range(n_m))
    g_lo = p_lo / (p_lo + p_hi)
    g_hi = p_hi / (p_lo + p_hi)
    pair = ((a * (2 * n_m - 1 - a)) >> 1) + (b - a - 1)
    cls = g_best * N_PAIRS + pair

    n_rows = count_ref.shape[0]
    onehot = jnp.where(lax.broadcasted_iota(jnp.int32, (n_rows, ts), 0) == cls, 1.0, 0.0)
    earlier = jnp.where(lax.broadcasted_iota(jnp.int32, (ts, ts), 0)
                        < lax.broadcasted_iota(jnp.int32, (ts, ts), 1), 1.0, 0.0).astype(bf16)
    before = jnp.dot(onehot.astype(bf16), earlier, preferred_element_type=f32)
    running = count_ref[...]
    running_ts = jnp.concatenate([running] * (ts // LANES), axis=1)
    rank = jnp.sum(onehot * (before + running_ts), axis=0, keepdims=True)
    count_ref[...] = running + jnp.dot(onehot.astype(bf16), jnp.ones((ts, LANES), bf16),
                                       preferred_element_type=f32)
    return jnp.concatenate([g_lo, g_hi, cls.astype(f32), rank, jnp.zeros((SUBLANES - 4, ts), f32)], axis=0)


def _cast_slabs(w32_refs, w16_refs):
    for w32_ref, w16_ref in zip(w32_refs, w16_refs):
        w16_ref[...] = w32_ref[0].astype(w16_ref.dtype)


def _cast_slab_specs(layer, n_steps, weights):
    args, in_specs, out_shapes, out_specs = [], [], [], []
    for w in weights:
        n_layers, n_exp, rows, cols = w.shape
        slab, rem = divmod(n_exp * rows, n_steps)
        assert rem == 0 and slab % (2 * SUBLANES) == 0
        args.append(w.reshape(n_layers, n_exp * rows, cols))
        in_specs.append(pl.BlockSpec((1, slab, cols), lambda i: (layer, i, 0)))
        out_shapes.append(jax.ShapeDtypeStruct((n_exp * rows, cols), jnp.bfloat16))
        out_specs.append(pl.BlockSpec((slab, cols), lambda i: (i, 0)))
    return args, in_specs, out_shapes, out_specs


def _conv_mixer_kernel(x_ref, w_in_ref, w_conv_ref, w_out_ref, ln_g_ref, ln_b_ref,
                       rw_hi_ref, rw_lo_ref, rb_ref, wg32_ref, wu32_ref, wd32_ref,
                       out_ref, meta_t_ref, count_ref, wg16_ref, wu16_ref, wd16_ref, carry_ref, *, tiles_per_seq):
    i = pl.program_id(0)

    @pl.when(i == 0)
    def _():
        count_ref[...] = jnp.zeros_like(count_ref)

    @pl.when(i % tiles_per_seq == 0)
    def _():
        carry_ref[...] = jnp.zeros_like(carry_ref)

    f32 = jnp.float32
    d = D_MODEL
    ts = x_ref.shape[0]
    sub = ts // MIX_SUBTILES
    xs = [x_ref[k * sub:(k + 1) * sub, :] for k in range(MIX_SUBTILES)]
    projs = [jnp.dot(x.astype(jnp.bfloat16), w_in_ref[...], preferred_element_type=f32) for x in xs]
    row = lax.broadcasted_iota(jnp.int32, (sub, d), 0)
    wc = w_conv_ref[...]
    prev = carry_ref[...]
    gated = []
    for proj in projs:
        gate_b, gate_c, h = proj[:, :d], proj[:, d:2 * d], proj[:, 2 * d:]
        u = gate_c * h
        u1 = jnp.where(row == 0, prev[SUBLANES - 1:SUBLANES, :], pltpu.roll(u, 1, axis=0))
        u2 = jnp.where(row == 0, prev[SUBLANES - 2:SUBLANES - 1, :],
                       jnp.where(row == 1, prev[SUBLANES - 1:SUBLANES, :], pltpu.roll(u, 2, axis=0)))
        prev = u[sub - SUBLANES:, :]
        conv = wc[0:1, :] * u2 + wc[1:2, :] * u1 + wc[2:3, :] * u
        gated.append((gate_b * conv).astype(jnp.bfloat16))
    carry_ref[...] = prev
    mixes = [jnp.dot(v, w_out_ref[...], preferred_element_type=f32) for v in gated]
    x1 = jnp.concatenate([_layer_norm(ALPHA * x + mix, ln_g_ref[...], ln_b_ref[...])
                          for x, mix in zip(xs, mixes)], axis=0)
    meta = _route_and_rank(x1, rw_hi_ref[...], rw_lo_ref[...], rb_ref[...], count_ref)
    out_ref[:, :d] = x1
    out_ref[:, d:] = jnp.concatenate([meta, jnp.zeros((LANES - SUBLANES, meta.shape[1]), meta.dtype)], axis=0).T
    meta_t_ref[...] = meta
    _cast_slabs((wg32_ref, wu32_ref, wd32_ref), (wg16_ref, wu16_ref, wd16_ref))


def _const_spec(shape):
    return pl.BlockSpec(shape, lambda i: (0,) * len(shape))


def _pad_rows(a, rows=SUBLANES):
    return jnp.pad(a, ((0, rows - a.shape[0]), (0, 0)))


def _prep_router(router_w, router_b):
    order = np.array([g * EXPERTS_PER_GROUP + j for j in range(EXPERTS_PER_GROUP) for g in range(N_GROUPS)])
    rw = jnp.pad(router_w.astype(jnp.float32)[:, order], ((0, 0), (0, LANES - N_EXPERTS)))
    rb = router_b.astype(jnp.float32)[order][:, None]
    return _split_bf16(rw), rb


def _conv_mixer_layer(x2d, seq, w_in, w_conv, w_out, ln_g, ln_b, rw_hi, rw_lo, rb, layer, expert_weights):
    t, d = x2d.shape
    ts = MIX_TILE
    grid = (t // ts,)
    cast_args, cast_in, cast_shapes, cast_out = _cast_slab_specs(layer, grid[0], expert_weights)
    return pl.pallas_call(
        functools.partial(_conv_mixer_kernel, tiles_per_seq=seq // ts),
        out_shape=(jax.ShapeDtypeStruct((t, ROW_W), jnp.float32),
                   jax.ShapeDtypeStruct((SUBLANES, t), jnp.float32),
                   jax.ShapeDtypeStruct((CLASS_ROWS, LANES), jnp.float32), *cast_shapes),
        grid=grid,
        in_specs=[pl.BlockSpec((ts, d), lambda i: (i, 0)),
                  _const_spec((d, 3 * d)), _const_spec((SUBLANES, d)), _const_spec((d, d)),
                  _const_spec((1, d)), _const_spec((1, d)),
                  _const_spec((d, LANES)), _const_spec((d, LANES)), _const_spec((N_EXPERTS, 1)), *cast_in],
        out_specs=(pl.BlockSpec((ts, ROW_W), lambda i: (i, 0)),
                   pl.BlockSpec((SUBLANES, ts), lambda i: (0, i)),
                   _const_spec((CLASS_ROWS, LANES)), *cast_out),
        scratch_shapes=[pltpu.VMEM((SUBLANES, d), jnp.float32)],
        compiler_params=pltpu.CompilerParams(dimension_semantics=("arbitrary",),
                                             vmem_limit_bytes=VMEM_LIMIT_BYTES),
        name="conv_mixer_route",
    )(x2d, w_in, w_conv, w_out, ln_g, ln_b, rw_hi, rw_lo, rb, *cast_args)


def _log_sigmoid(z):
    return jnp.minimum(z, 0.0) - jnp.log1p(jnp.exp(-jnp.abs(z)))


def _split3_f32(a):
    f32, bf16 = jnp.float32, jnp.bfloat16
    hi = a.astype(bf16).astype(f32)
    r1 = a - hi
    mid = r1.astype(bf16).astype(f32)
    lo = (r1 - mid).astype(bf16).astype(f32)
    return hi, mid, lo


_COL_U, _COL_ONE, _COL_W, _COL_EMT, _COL_END = 0, 24, 48, 64, 80


def _mlstm_mixer_kernel(x_ref, w_qvo_ref, wkt_ref, wgt_hi_ref, bg_row_ref, norm_g_ref, w_out_ref,
                        ln_g_ref, ln_b_ref, rw_hi_ref, rw_lo_ref, rb_ref, wg32_ref, wu32_ref, wd32_ref,
                        out_ref, meta_t_ref, count_ref, wg16_ref, wu16_ref, wd16_ref,
                        proj_ref, kt_ref, grow_ref, colmat_ref, bexp_ref, h_ref, c_ref, m_ref, *, tiles_per_seq):
    i = pl.program_id(0)
    f32, bf16 = jnp.float32, jnp.bfloat16
    nh, dk, dv, d = ML_HEADS, ML_QK_DIM, ML_V_DIM, D_MODEL
    L = ML_CHUNK
    ts = x_ref.shape[0]
    n_chunks = ts // L
    o_base = ML_QK_W + ML_V_W

    @pl.when(i == 0)
    def _():
        count_ref[...] = jnp.zeros_like(count_ref)

    @pl.when(i % tiles_per_seq == 0)
    def _():
        c_ref[...] = jnp.zeros_like(c_ref)
        m_ref[...] = jnp.zeros_like(m_ref)

    x = x_ref[...]
    x_hi, x_lo = _split_bf16(x)
    nt = (((1,), (1,)), ((), ()))
    proj_ref[...] = jnp.dot(x_hi, w_qvo_ref[...], preferred_element_type=f32)
    kg_t = lax.dot_general(wkt_ref[...], x_hi, nt, preferred_element_type=f32)
    k_t, hi_terms = kg_t[:ML_QK_W, :], kg_t[ML_QK_W:, :]
    g_row = (hi_terms[:2 * nh, :] + hi_terms[2 * nh:, :]
             + lax.dot_general(wgt_hi_ref[...], x_lo, nt, preferred_element_type=f32)) + bg_row_ref[...]
    r_i = lax.broadcasted_iota(jnp.int32, (ts, ts), 0)
    c_i = lax.broadcasted_iota(jnp.int32, (ts, ts), 1)
    tri_row = jnp.where(((r_i // L) == (c_i // L)) & (r_i <= c_i), 1.0, 0.0).astype(bf16)
    lf_hi, lf_lo = _split_bf16(_log_sigmoid(g_row[nh:, :]))
    cum = jnp.dot(jnp.concatenate([lf_hi, lf_lo], axis=0), tri_row, preferred_element_type=f32)
    i_row, b_row = g_row[:nh, :], cum[:nh, :] + cum[nh:, :]

    g = i_row - b_row
    lane_in_chunk = lax.broadcasted_iota(jnp.int32, (nh, ts), 1) & (L - 1)
    cm = g
    shift = 1
    while shift < L:
        cm = jnp.maximum(cm, jnp.where(lane_in_chunk >= shift, pltpu.roll(cm, shift, axis=1), -jnp.inf))
        shift *= 2
    m_prev = m_ref[:, 0:1]
    u_parts, w_parts, gr_parts, gs_parts = [], [], [], []
    for c in range(n_chunks):
        sl = slice(c * L, (c + 1) * L)
        u_c = jnp.maximum(m_prev, cm[:, sl])
        b_last = b_row[:, (c + 1) * L - 1:(c + 1) * L]
        m_new = b_last + u_c[:, L - 1:L]
        u_parts.append(u_c)
        w_parts.append(jnp.exp(m_prev - u_c))
        gr_parts.append(jnp.exp(b_last - b_row[:, sl] + i_row[:, sl] - m_new))
        gs_parts.append(jnp.broadcast_to(jnp.exp(b_last + m_prev - m_new), (nh, L)))
        m_prev = m_new
    m_ref[...] = jnp.broadcast_to(m_prev, m_ref.shape)
    u = jnp.concatenate(u_parts, axis=1)
    w_inter = jnp.concatenate(w_parts, axis=1)
    emt = jnp.exp(-(b_row + u))
    w_hi = w_inter.astype(bf16).astype(f32)
    e_hi = emt.astype(bf16).astype(f32)
    col_rows = jnp.concatenate(
        [*_split3_f32(u), jnp.ones((_COL_W - _COL_ONE, ts), f32), w_hi, w_inter - w_hi, e_hi, emt - e_hi,
         jnp.zeros((LANES - _COL_END, ts), f32)], axis=0)
    colmat_ref[...] = col_rows.T.astype(bf16)
    b_exp = jnp.concatenate([jnp.full((_COL_ONE - _COL_U, ts), -1.0, f32), *_split3_f32(g),
                             jnp.zeros((LANES - _COL_W, ts), f32)], axis=0).astype(bf16)
    for c in range(n_chunks):
        sl = slice(c * L, (c + 1) * L)
        bexp_ref[c] = b_exp[:, sl]
        grow_ref[c] = jnp.concatenate([gr_parts[c], gs_parts[c]], axis=0)
        kt_ref[c] = k_t[:, sl]

    causal = lax.broadcasted_iota(jnp.int32, (L, L), 0) >= lax.broadcasted_iota(jnp.int32, (L, L), 1)
    ones_col = jnp.where(lax.broadcasted_iota(jnp.int32, (L, dv), 1) == 0, 1.0, 0.0).astype(bf16)
    lane_ll = lax.broadcasted_iota(jnp.int32, (L, LANES), 1)
    sel_r = lax.broadcasted_iota(jnp.int32, (LANES, 2 * LANES), 0)
    sel_c = lax.broadcasted_iota(jnp.int32, (LANES, 2 * LANES), 1)
    is_w_row = (sel_r >= _COL_W) & (sel_r < _COL_EMT)
    is_e_row = (sel_r >= _COL_EMT) & (sel_r < _COL_END)
    b_sel = jnp.where((is_w_row & (sel_c < LANES)) | (is_e_row & (sel_c >= LANES)), 1.0, 0.0).astype(bf16)
    r2_r = lax.broadcasted_iota(jnp.int32, (2 * dv, 2 * dv), 0)
    r2_c = lax.broadcasted_iota(jnp.int32, (2 * dv, 2 * dv), 1)
    rhs2 = jnp.where((r2_r < dv) & (r2_c < dv), 1.0 / dv,
                     jnp.where((r2_r == dv) & (r2_c >= dv), 1.0, 0.0)).astype(bf16)

    def chunk_body(c, carry):
        r0 = pl.multiple_of(c * L, L)
        gr = grow_ref[c]
        colmat = colmat_ref[pl.ds(r0, L), :]
        rhs1 = jnp.concatenate([bexp_ref[c], b_sel], axis=1)
        heads = range(nh)
        c_pairs = [c_ref[p] for p in range(nh // 2)]
        kt_pairs = [kt_ref[c, p * LANES:(p + 1) * LANES, :] for p in range(nh // 2)]

        ew, q_m, s_mat, v_ext = [], [], [], []
        for h in heads:
            p, hh = divmod(h, 2)
            lhs = jnp.where((lane_ll & (nh - 1)) == h, colmat, 0)
            ew.append(jnp.dot(lhs, rhs1, preferred_element_type=f32))
            q2 = proj_ref[pl.ds(r0, L), p * LANES:(p + 1) * LANES] * (dk ** -0.5)
            q_m.append(jnp.where((lane_ll >= hh * dk) & (lane_ll < (hh + 1) * dk), q2, 0.0))
            s_mat.append(jnp.dot(q_m[h].astype(bf16), kt_pairs[p].astype(bf16), preferred_element_type=f32))
            v = proj_ref[pl.ds(r0, L), ML_QK_W + h * dv:ML_QK_W + (h + 1) * dv].astype(bf16)
            v_ext.append(jnp.concatenate([v, ones_col], axis=1))
        tot = []
        for h in heads:
            p, hh = divmod(h, 2)
            a = jnp.exp(jnp.where(causal, ew[h][:, :L], -jnp.inf)) * s_mat[h]
            qw = q_m[h] * ew[h][:, L:L + LANES]
            lhs = jnp.concatenate([a.astype(bf16), qw.astype(bf16)], axis=1)
            rhs = jnp.concatenate([v_ext[h], c_pairs[p].astype(bf16)], axis=0)
            tot.append(jnp.dot(lhs, rhs, preferred_element_type=f32))
        for h in heads:
            num = tot[h][:, :dv]
            lhs2 = jnp.concatenate([(num * num).astype(bf16), tot[h][:, dv:].astype(bf16)], axis=1)
            r2 = jnp.dot(lhs2, rhs2, preferred_element_type=f32)
            inv = 1.0 / jnp.maximum(jnp.abs(r2[:, dv:]), ew[h][:, L + LANES:])
            hn = (num * inv) * lax.rsqrt(r2[:, :dv] * inv * inv + HEAD_NORM_EPS)
            o_pre = proj_ref[pl.ds(r0, L), o_base + h * dv:o_base + (h + 1) * dv]
            h_ref[pl.ds(r0, L), h * dv:(h + 1) * dv] = (
                jax.nn.sigmoid(o_pre) * hn * norm_g_ref[:, h * dv:(h + 1) * dv])
        for h in heads:
            p, hh = divmod(h, 2)
            rows = slice(hh * dk, (hh + 1) * dk)
            kg = (kt_pairs[p][rows, :] * gr[h:h + 1, :]).astype(bf16)
            gs = gr[nh + h:nh + h + 1, :]
            c_ref[p, rows, :] = (jnp.concatenate([gs, gs], axis=1) * c_pairs[p][rows, :]
                                 + jnp.dot(kg, v_ext[h], preferred_element_type=f32))
        return carry

    lax.fori_loop(0, n_chunks, chunk_body, 0)

    sub = ts // MIX_SUBTILES
    mixes = [jnp.dot(h_ref[k * sub:(k + 1) * sub, :].astype(bf16), w_out_ref[...], preferred_element_type=f32)
             for k in range(MIX_SUBTILES)]
    x1 = jnp.concatenate([_layer_norm(ALPHA * x[k * sub:(k + 1) * sub, :] + mixes[k], ln_g_ref[...], ln_b_ref[...])
                          for k in range(MIX_SUBTILES)], axis=0)
    meta = _route_and_rank(x1, rw_hi_ref[...], rw_lo_ref[...], rb_ref[...], count_ref)
    out_ref[:, :d] = x1
    out_ref[:, d:] = jnp.concatenate([meta, jnp.zeros((LANES - SUBLANES, meta.shape[1]), meta.dtype)], axis=0).T
    meta_t_ref[...] = meta
    _cast_slabs((wg32_ref, wu32_ref, wd32_ref), (wg16_ref, wu16_ref, wd16_ref))


def _mlstm_mixer_layer(x2d, seq, w_in, b_gate, norm_g, w_out, ln_g, ln_b, rw_hi, rw_lo, rb, layer,
                       expert_weights):
    t, d = x2d.shape
    ts = MIX_TILE
    nh = ML_HEADS
    cast_args, cast_in, cast_shapes, cast_out = _cast_slab_specs(layer, t // ts, expert_weights)
    assert ML_CHUNK == LANES and ts % ML_CHUNK == 0 and 2 * ML_QK_DIM == LANES and ML_V_DIM == LANES
    n_qkvo = 2 * ML_QK_W + 2 * ML_V_W
    n_qvo = ML_QK_W + 2 * ML_V_W
    w_qvo = jnp.concatenate([w_in[:, :ML_QK_W], w_in[:, 2 * ML_QK_W:n_qkvo]], axis=1).astype(jnp.bfloat16)
    w_g = w_in[:, n_qkvo:].astype(jnp.float32)
    wgt_hi, wgt_lo = _split_bf16(w_g.T)
    wkt = jnp.concatenate([w_in[:, ML_QK_W:2 * ML_QK_W].T.astype(jnp.bfloat16), wgt_hi, wgt_lo], axis=0)
    bg_row = b_gate.astype(jnp.float32)[:, None]
    return pl.pallas_call(
        functools.partial(_mlstm_mixer_kernel, tiles_per_seq=seq // ts),
        out_shape=(jax.ShapeDtypeStruct((t, ROW_W), jnp.float32),
                   jax.ShapeDtypeStruct((SUBLANES, t), jnp.float32),
                   jax.ShapeDtypeStruct((CLASS_ROWS, LANES), jnp.float32), *cast_shapes),
        grid=(t // ts,),
        in_specs=[pl.BlockSpec((ts, d), lambda i: (i, 0)),
                  _const_spec((d, n_qvo)), _const_spec((ML_QK_W + 4 * nh, d)),
                  _const_spec((2 * nh, d)),
                  _const_spec((2 * nh, 1)),
                  _const_spec((1, d)), _const_spec((d, d)),
                  _const_spec((1, d)), _const_spec((1, d)),
                  _const_spec((d, LANES)), _const_spec((d, LANES)), _const_spec((N_EXPERTS, 1)), *cast_in],
        out_specs=(pl.BlockSpec((ts, ROW_W), lambda i: (i, 0)),
                   pl.BlockSpec((SUBLANES, ts), lambda i: (0, i)),
                   _const_spec((CLASS_ROWS, LANES)), *cast_out),
        scratch_shapes=[pltpu.VMEM((ts, n_qvo), jnp.float32),
                        pltpu.VMEM((ts // ML_CHUNK, ML_QK_W, ML_CHUNK), jnp.float32),
                        pltpu.VMEM((ts // ML_CHUNK, 2 * nh, ML_CHUNK), jnp.float32),
                        pltpu.VMEM((ts, LANES), jnp.bfloat16),
                        pltpu.VMEM((ts // ML_CHUNK, LANES, ML_CHUNK), jnp.bfloat16),
                        pltpu.VMEM((ts, d), jnp.float32),
                        pltpu.VMEM((nh // 2, 2 * ML_QK_DIM, 2 * ML_V_DIM), jnp.float32),
                        pltpu.VMEM((nh, ML_CHUNK), jnp.float32)],
        compiler_params=pltpu.CompilerParams(dimension_semantics=("arbitrary",),
                                             vmem_limit_bytes=VMEM_LIMIT_BYTES),
        name="mlstm_mixer_route",
    )(x2d, w_qvo, wkt, wgt_hi, bg_row, norm_g[None, :], w_out.astype(jnp.bfloat16),
      ln_g, ln_b, rw_hi, rw_lo, rb, *cast_args)


def _expert_kernel(dest_ref, elo_ref, ehi_ref, nvalid_ref,
                   x_hbm, wg_lo_ref, wu_lo_ref, wd_lo_ref, wg_hi_ref, wu_hi_ref, wd_hi_ref,
                   ln_g_ref, ln_b_ref, out_hbm, xbuf0, xbuf1, obuf0, obuf1, tok_ref, gather_sem, scatter_sem):
    i = pl.program_id(0)
    nb = pl.num_programs(0)
    bm = obuf0.shape[0]
    d = D_MODEL

    @pl.when(i == 0)
    def _():
        def pad_block(b, carry):
            def pad_row(r, c):
                tok_ref[b * bm + r] = 0
                return c
            first = jnp.where(b < nb, nvalid_ref[jnp.minimum(b, nb - 1)], 0)
            return lax.fori_loop(first, bm, pad_row, carry)

        def place(t, c):
            tok_ref[dest_ref[t]] = t
            return c

        lax.fori_loop(0, nb + 1, pad_block, 0)
        lax.fori_loop(0, x_hbm.shape[0], place, 0, unroll=8)
    xbufs, obufs = (xbuf0, xbuf1), (obuf0, obuf1)
    nv = nvalid_ref[i]
    nv_prev = nvalid_ref[jnp.maximum(i - 1, 0)]

    def start_gather(j, s, rows=range(bm)):
        for r in rows:
            tok = tok_ref[j * bm + r]
            pltpu.make_async_copy(x_hbm.at[pl.ds(tok, 1)], xbufs[s].at[pl.ds(r, 1)], gather_sem.at[s]).start()

    def wait_gather(s):
        pltpu.make_async_copy(xbufs[s].at[pl.ds(0, bm)], xbufs[s].at[pl.ds(0, bm)], gather_sem.at[s]).wait()

    def scatter_copy(j, s, r):
        tok = tok_ref[j * bm + r]
        return pltpu.make_async_copy(obufs[s].at[pl.ds(r, 1)], out_hbm.at[pl.ds(tok, 1)], scatter_sem.at[s])

    def wait_scatter(s, n):
        @pl.when(n == bm)
        def _():
            pltpu.make_async_copy(obufs[s], obufs[s], scatter_sem.at[s]).wait()

        @pl.when(n < bm)
        def _():
            def body(r, c):
                pltpu.make_async_copy(obufs[s].at[pl.ds(0, 1)], obufs[s].at[pl.ds(0, 1)], scatter_sem.at[s]).wait()
                return c
            lax.fori_loop(0, n, body, 0)

    @pl.when(i == 0)
    def _():
        for buf in xbufs:
            buf[bm:, :] = jnp.zeros((buf.shape[0] - bm, buf.shape[1]), buf.dtype)
        start_gather(0, 0)

    def step(s):
        @pl.when((i == 0) | (nv_prev > 0))
        def _():
            wait_gather(s)

        @pl.when(nv > 0)
        def _():
            xb = xbufs[s][0:bm, :]
            x = xb[:, :d]
            g_lo = xb[:, d + META_G_LO:d + META_G_LO + 1]
            g_hi = xb[:, d + META_G_HI:d + META_G_HI + 1]
            x16 = x.astype(jnp.bfloat16)
            n_portions = 2 * GATE_CHUNKS

            def gate_proj(w_ref, k0):
                g = jnp.dot(x16, w_ref[0], preferred_element_type=jnp.float32)
                width = g.shape[1] // GATE_CHUNKS
                chunks = []
                for c in range(GATE_CHUNKS):
                    k = k0 + c
                    start_gather(i + 1, 1 - s, range(k * bm // n_portions, (k + 1) * bm // n_portions))
                    zero_row = xbufs[1 - s][bm:bm + 1, 0:LANES]
                    chunks.append(g[:, c * width:(c + 1) * width]
                                  + jnp.concatenate([zero_row] * (width // LANES), axis=1))
                return jnp.concatenate(chunks, axis=1)

            def ffn(k0, wg_ref, wu_ref, wd_ref):
                g = gate_proj(wg_ref, k0)
                u = jnp.dot(x16, wu_ref[0], preferred_element_type=jnp.float32)
                h = (g * jax.nn.sigmoid(g)) * u
                return jnp.dot(h.astype(jnp.bfloat16), wd_ref[0], preferred_element_type=jnp.float32)

            y = (g_lo * ffn(0, wg_lo_ref, wu_lo_ref, wd_lo_ref)
                 + g_hi * ffn(GATE_CHUNKS, wg_hi_ref, wu_hi_ref, wd_hi_ref))
            obufs[s][...] = _layer_norm(ALPHA * x + y, ln_g_ref[...], ln_b_ref[...])

            @pl.when(nv == bm)
            def _():
                for r in range(bm):
                    scatter_copy(i, s, r).start()

            @pl.when(nv < bm)
            def _():
                def body(r, c):
                    scatter_copy(i, s, r).start()
                    return c
                lax.fori_loop(0, nv, body, 0)

        @pl.when(i > 0)
        def _():
            wait_scatter(1 - s, nv_prev)

        @pl.when(i == nb - 1)
        def _():
            wait_scatter(s, nv)

            @pl.when(nv > 0)
            def _():
                wait_gather(1 - s)

    for s in range(2):
        pl.when(i % 2 == s)(functools.partial(step, s))


def _moe_layer(xext, meta_t, counts, w_gate, w_up, w_down, ln_g, ln_b):
    t = xext.shape[0]
    d, f, bm = D_MODEL, D_EXPERT, EXPERT_BLOCK
    n_blocks = t // bm + N_CLASSES
    n_rows = n_blocks * bm

    cls = meta_t[META_CLASS].astype(jnp.int32)
    rank = meta_t[META_RANK].astype(jnp.int32)
    cnt = counts[:N_CLASSES, 0].astype(jnp.int32)
    cls_blocks = (cnt + bm - 1) // bm
    blk_end = jnp.cumsum(cls_blocks)
    blk_start = blk_end - cls_blocks
    total_blocks = blk_end[-1]
    row_start = jnp.sum(jnp.where(cls[None, :] == jnp.arange(N_CLASSES, dtype=jnp.int32)[:, None],
                                  blk_start[:, None] * bm, 0), axis=0)
    dest = (row_start + rank).astype(jnp.int32)
    blk = jnp.arange(n_blocks, dtype=jnp.int32)
    blk_cls = jnp.sum(jnp.minimum(blk, total_blocks - 1)[:, None] >= blk_end[None, :], axis=1).astype(jnp.int32)
    blk_cls = jnp.minimum(blk_cls, N_CLASSES - 1)
    nvalid = jnp.clip(cnt[blk_cls] - (blk - blk_start[blk_cls]) * bm, 0, bm)
    nvalid = jnp.where(blk < total_blocks, nvalid, 0).astype(jnp.int32)
    e_lo = jnp.asarray(CLASS_E_LO)[blk_cls]
    e_hi = jnp.asarray(CLASS_E_HI)[blk_cls]

    w_gate, w_up, w_down = (w_gate.reshape(N_EXPERTS, d, f), w_up.reshape(N_EXPERTS, d, f),
                            w_down.reshape(N_EXPERTS, f, d))
    w_lo = lambda shape: pl.BlockSpec((1,) + shape, lambda i, tok, elo, ehi, nv: (elo[i], 0, 0))
    w_hi = lambda shape: pl.BlockSpec((1,) + shape, lambda i, tok, elo, ehi, nv: (ehi[i], 0, 0))
    vec = pl.BlockSpec((1, d), lambda i, tok, elo, ehi, nv: (0, 0))
    grid_spec = pltpu.PrefetchScalarGridSpec(
        num_scalar_prefetch=4,
        grid=(n_blocks,),
        in_specs=[pl.BlockSpec(memory_space=pl.ANY),
                  w_lo((d, f)), w_lo((d, f)), w_lo((f, d)),
                  w_hi((d, f)), w_hi((d, f)), w_hi((f, d)),
                  vec, vec],
        out_specs=pl.BlockSpec(memory_space=pl.ANY),
        scratch_shapes=[pltpu.VMEM((bm + SUBLANES, ROW_W), jnp.float32), pltpu.VMEM((bm + SUBLANES, ROW_W), jnp.float32),
                        pltpu.VMEM((bm, d), jnp.float32), pltpu.VMEM((bm, d), jnp.float32),
                        pltpu.SMEM((n_rows + bm,), jnp.int32),
                        pltpu.SemaphoreType.DMA((2,)),
                        pltpu.SemaphoreType.DMA((2,))],
    )
    return pl.pallas_call(
        _expert_kernel,
        out_shape=jax.ShapeDtypeStruct((t, d), jnp.float32),
        grid_spec=grid_spec,
        compiler_params=pltpu.CompilerParams(dimension_semantics=("arbitrary",),
                                             vmem_limit_bytes=VMEM_LIMIT_BYTES),
        name="expert_pair_ffn",
    )(dest, e_lo, e_hi, nvalid, xext, w_gate, w_up, w_down, w_gate, w_up, w_down, ln_g, ln_b)


def kernel(x, conv_w_in, conv_w, conv_w_out, ml_w_in, ml_b_gate, ml_norm_g, ml_w_out, ln_mix_g, ln_mix_b,
           ln_ffn_g, ln_ffn_b, router_w, router_b, exp_w_gate, exp_w_up, exp_w_down):
    bsz, seq, d = x.shape
    assert d == D_MODEL and seq % MIX_TILE == 0 and (bsz * seq) % EXPERT_BLOCK == 0
    bf16 = jnp.bfloat16
    vec = lambda a: a.astype(jnp.float32)[None, :]
    (rw_hi, rw_lo), rb = _prep_router(router_w, router_b)
    x2d = x.reshape(bsz * seq, d).astype(jnp.float32)
    expert_weights = (exp_w_gate.astype(jnp.float32), exp_w_up.astype(jnp.float32),
                      exp_w_down.astype(jnp.float32))
    for i in range(DEPTH):
        j = i // 2
        if i % 2 == 0:
            xext, meta_t, counts, *w16 = _conv_mixer_layer(
                x2d, seq, conv_w_in[j].astype(bf16), _pad_rows(conv_w[j].astype(jnp.float32)),
                conv_w_out[j].astype(bf16), vec(ln_mix_g[i]), vec(ln_mix_b[i]), rw_hi, rw_lo, rb,
                i, expert_weights)
        else:
            xext, meta_t, counts, *w16 = _mlstm_mixer_layer(
                x2d, seq, ml_w_in[j], ml_b_gate[j], ml_norm_g[j].astype(jnp.float32), ml_w_out[j],
                vec(ln_mix_g[i]), vec(ln_mix_b[i]), rw_hi, rw_lo, rb, i, expert_weights)
        x2d = _moe_layer(xext, meta_t, counts, *w16, vec(ln_ffn_g[i]), vec(ln_ffn_b[i]))
    return x2d.reshape(bsz, seq, d).astype(x.dtype)
```

```python
import functools
import itertools

import numpy as np
import jax
import jax.numpy as jnp
from jax import lax
from jax.experimental import pallas as pl
from jax.experimental.pallas import tpu as pltpu

D_MODEL = 1024
DEPTH = 2
CONV_WIDTH = 3
ML_HEADS = 8
ML_QK_DIM = D_MODEL // (2 * ML_HEADS)
ML_V_DIM = D_MODEL // ML_HEADS
ML_QK_W = ML_HEADS * ML_QK_DIM
ML_V_W = ML_HEADS * ML_V_DIM
N_EXPERTS = 16
N_GROUPS = 4
EXPERTS_PER_GROUP = N_EXPERTS // N_GROUPS
D_EXPERT = 3 * D_MODEL // 2
ALPHA = (2 * DEPTH) ** 0.25
LN_EPS = 1e-5
HEAD_NORM_EPS = 1e-6

LANES = 128
SUBLANES = 8
VMEM_LIMIT_BYTES = 56 * 1024 * 1024

PAIRS = tuple(itertools.combinations(range(EXPERTS_PER_GROUP), 2))
N_PAIRS = len(PAIRS)
N_CLASSES = N_GROUPS * N_PAIRS
CLASS_ROWS = -(-N_CLASSES // 16) * 16
CLASS_E_LO = np.array([g * EXPERTS_PER_GROUP + a for g in range(N_GROUPS) for a, _ in PAIRS], np.int32)
CLASS_E_HI = np.array([g * EXPERTS_PER_GROUP + b for g in range(N_GROUPS) for _, b in PAIRS], np.int32)

META_G_LO, META_G_HI, META_CLASS, META_RANK = 0, 1, 2, 3
ROW_W = D_MODEL + LANES

MIX_TILE = 512
MIX_SUBTILES = 2
EXPERT_BLOCK = 256
GATE_CHUNKS = 6
ML_CHUNK = 128


def _layer_norm(z, g, b):
    mu = jnp.mean(z, axis=-1, keepdims=True)
    zc = z - mu
    var = jnp.mean(zc * zc, axis=-1, keepdims=True)
    return zc * lax.rsqrt(var + LN_EPS) * g + b


def _split_bf16(a):
    hi = a.astype(jnp.bfloat16)
    lo = (a - hi.astype(jnp.float32)).astype(jnp.bfloat16)
    return hi, lo


def _route_and_rank(x1, rw_hi, rw_lo, rb_col, count_ref):
    ts = x1.shape[0]
    f32, bf16 = jnp.float32, jnp.bfloat16
    n_g, n_m = N_GROUPS, EXPERTS_PER_GROUP

    x_hi, x_lo = _split_bf16(x1)
    hi_terms = jnp.dot(x_hi, jnp.concatenate([rw_hi, rw_lo], axis=1), preferred_element_type=f32)
    logits = (hi_terms[:, :LANES] + hi_terms[:, LANES:]
              + jnp.dot(x_lo, rw_hi, preferred_element_type=f32))
    lt = logits.T[:N_EXPERTS, :]
    ex = jnp.exp(lt - jnp.max(lt, axis=0, keepdims=True))
    probs = ex / jnp.sum(ex, axis=0, keepdims=True)
    sel = probs + rb_col
    member = lambda a, j: a[j * n_g:(j + 1) * n_g, :]

    in_top2 = []
    for j in range(n_m):
        beaten_by = jnp.zeros((n_g, ts), jnp.int32)
        for k in range(n_m):
            if k != j:
                wins = member(sel, k) > member(sel, j)
                if k < j:
                    wins = wins | (member(sel, k) == member(sel, j))
                beaten_by = beaten_by + wins.astype(jnp.int32)
        in_top2.append(beaten_by < 2)
    score = sum(jnp.where(in_top2[j], member(sel, j), 0.0) for j in range(n_m))
    group = lax.broadcasted_iota(jnp.int32, (n_g, ts), 0)
    best = jnp.max(score, axis=0, keepdims=True)
    g_best = jnp.min(jnp.where(score == best, group, n_g), axis=0, keepdims=True)
    in_best = group == g_best
    picked = [jnp.max(jnp.where(in_top2[j] & in_best, 1, 0), axis=0, keepdims=True) for j in range(n_m)]
    p_pick = [jnp.sum(jnp.where(in_top2[j] & in_best, member(probs, j), 0.0), axis=0, keepdims=True)
              for j in range(n_m)]
    a = functools.reduce(jnp.minimum, [jnp.where(picked[j] > 0, j, n_m) for j in range(n_m)])
    b = functools.reduce(jnp.maximum, [jnp.where(picked[j] > 0, j, -1) for j in range(n_m)])
    p_lo = sum(jnp.where(a == j, p_pick[j], 0.0) for j in range(n_m))
    p_hi = sum(jnp.where(b == j, p_pick[j], 0.0) for j in range(n_m))
    g_lo = p_lo / (p_lo + p_hi)
    g_hi = p_hi / (p_lo + p_hi)
    pair = ((a * (2 * n_m - 1 - a)) >> 1) + (b - a - 1)
    cls = g_best * N_PAIRS + pair

    n_rows = count_ref.shape[0]
    onehot = jnp.where(lax.broadcasted_iota(jnp.int32, (n_rows, ts), 0) == cls, 1.0, 0.0)
    earlier = jnp.where(lax.broadcasted_iota(jnp.int32, (ts, ts), 0)
                        < lax.broadcasted_iota(jnp.int32, (ts, ts), 1), 1.0, 0.0).astype(bf16)
    before = jnp.dot(onehot.astype(bf16), earlier, preferred_element_type=f32)
    running = count_ref[...]
    running_ts = jnp.concatenate([running] * (ts // LANES), axis=1)
    rank = jnp.sum(onehot * (before + running_ts), axis=0, keepdims=True)
    count_ref[...] = running + jnp.dot(onehot.astype(bf16), jnp.ones((ts, LANES), bf16),
                                       preferred_element_type=f32)
    return jnp.concatenate([g_lo, g_hi, cls.astype(f32), rank, jnp.zeros((SUBLANES - 4, ts), f32)], axis=0)


def _cast_slabs(w32_refs, w16_refs):
    for w32_ref, w16_ref in zip(w32_refs, w16_refs):
        w16_ref[...] = w32_ref[0].astype(w16_ref.dtype)


def _cast_slab_specs(layer, n_steps, weights):
    args, in_specs, out_shapes, out_specs = [], [], [], []
    for w in weights:
        n_layers, n_exp, rows, cols = w.shape
        slab, rem = divmod(n_exp * rows, n_steps)
        assert rem == 0 and slab % (2 * SUBLANES) == 0
        args.append(w.reshape(n_layers, n_exp * rows, cols))
        in_specs.append(pl.BlockSpec((1, slab, cols), lambda i: (layer, i, 0)))
        out_shapes.append(jax.ShapeDtypeStruct((n_exp * rows, cols), jnp.bfloat16))
        out_specs.append(pl.BlockSpec((slab, cols), lambda i: (i, 0)))
    return args, in_specs, out_shapes, out_specs


def _conv_mixer_kernel(x_ref, w_in_ref, w_conv_ref, w_out_ref, ln_g_ref, ln_b_ref,
                       rw_hi_ref, rw_lo_ref, rb_ref, wg32_ref, wu32_ref, wd32_ref,
                       out_ref, meta_t_ref, count_ref, wg16_ref, wu16_ref, wd16_ref, carry_ref, *, tiles_per_seq):
    i = pl.program_id(0)

    @pl.when(i == 0)
    def _():
        count_ref[...] = jnp.zeros_like(count_ref)

    @pl.when(i % tiles_per_seq == 0)
    def _():
        carry_ref[...] = jnp.zeros_like(carry_ref)

    f32 = jnp.float32
    d = D_MODEL
    ts = x_ref.shape[0]
    sub = ts // MIX_SUBTILES
    xs = [x_ref[k * sub:(k + 1) * sub, :] for k in range(MIX_SUBTILES)]
    projs = [jnp.dot(x.astype(jnp.bfloat16), w_in_ref[...], preferred_element_type=f32) for x in xs]
    row = lax.broadcasted_iota(jnp.int32, (sub, d), 0)
    wc = w_conv_ref[...]
    prev = carry_ref[...]
    gated = []
    for proj in projs:
        gate_b, gate_c, h = proj[:, :d], proj[:, d:2 * d], proj[:, 2 * d:]
        u = gate_c * h
        u1 = jnp.where(row == 0, prev[SUBLANES - 1:SUBLANES, :], pltpu.roll(u, 1, axis=0))
        u2 = jnp.where(row == 0, prev[SUBLANES - 2:SUBLANES - 1, :],
                       jnp.where(row == 1, prev[SUBLANES - 1:SUBLANES, :], pltpu.roll(u, 2, axis=0)))
        prev = u[sub - SUBLANES:, :]
        conv = wc[0:1, :] * u2 + wc[1:2, :] * u1 + wc[2:3, :] * u
        gated.append((gate_b * conv).astype(jnp.bfloat16))
    carry_ref[...] = prev
    mixes = [jnp.dot(v, w_out_ref[...], preferred_element_type=f32) for v in gated]
    x1 = jnp.concatenate([_layer_norm(ALPHA * x + mix, ln_g_ref[...], ln_b_ref[...])
                          for x, mix in zip(xs, mixes)], axis=0)
    meta = _route_and_rank(x1, rw_hi_ref[...], rw_lo_ref[...], rb_ref[...], count_ref)
    out_ref[:, :d] = x1
    out_ref[:, d:] = jnp.concatenate([meta, jnp.zeros((LANES - SUBLANES, meta.shape[1]), meta.dtype)], axis=0).T
    meta_t_ref[...] = meta
    _cast_slabs((wg32_ref, wu32_ref, wd32_ref), (wg16_ref, wu16_ref, wd16_ref))


def _const_spec(shape):
    return pl.BlockSpec(shape, lambda i: (0,) * len(shape))


def _pad_rows(a, rows=SUBLANES):
    return jnp.pad(a, ((0, rows - a.shape[0]), (0, 0)))


def _prep_router(router_w, router_b):
    order = np.array([g * EXPERTS_PER_GROUP + j for j in range(EXPERTS_PER_GROUP) for g in range(N_GROUPS)])
    rw = jnp.pad(router_w.astype(jnp.float32)[:, order], ((0, 0), (0, LANES - N_EXPERTS)))
    rb = router_b.astype(jnp.float32)[order][:, None]
    return _split_bf16(rw), rb


def _conv_mixer_layer(x2d, seq, w_in, w_conv, w_out, ln_g, ln_b, rw_hi, rw_lo, rb, layer, expert_weights):
    t, d = x2d.shape
    ts = MIX_TILE
    grid = (t // ts,)
    cast_args, cast_in, cast_shapes, cast_out = _cast_slab_specs(layer, grid[0], expert_weights)
    return pl.pallas_call(
        functools.partial(_conv_mixer_kernel, tiles_per_seq=seq // ts),
        out_shape=(jax.ShapeDtypeStruct((t, ROW_W), jnp.float32),
                   jax.ShapeDtypeStruct((SUBLANES, t), jnp.float32),
                   jax.ShapeDtypeStruct((CLASS_ROWS, LANES), jnp.float32), *cast_shapes),
        grid=grid,
        in_specs=[pl.BlockSpec((ts, d), lambda i: (i, 0)),
                  _const_spec((d, 3 * d)), _const_spec((SUBLANES, d)), _const_spec((d, d)),
                  _const_spec((1, d)), _const_spec((1, d)),
                  _const_spec((d, LANES)), _const_spec((d, LANES)), _const_spec((N_EXPERTS, 1)), *cast_in],
        out_specs=(pl.BlockSpec((ts, ROW_W), lambda i: (i, 0)),
                   pl.BlockSpec((SUBLANES, ts), lambda i: (0, i)),
                   _const_spec((CLASS_ROWS, LANES)), *cast_out),
        scratch_shapes=[pltpu.VMEM((SUBLANES, d), jnp.float32)],
        compiler_params=pltpu.CompilerParams(dimension_semantics=("arbitrary",),
                                             vmem_limit_bytes=VMEM_LIMIT_BYTES),
        name="conv_mixer_route",
    )(x2d, w_in, w_conv, w_out, ln_g, ln_b, rw_hi, rw_lo, rb, *cast_args)


def _log_sigmoid(z):
    return jnp.minimum(z, 0.0) - jnp.log1p(jnp.exp(-jnp.abs(z)))


def _split3_f32(a):
    f32, bf16 = jnp.float32, jnp.bfloat16
    hi = a.astype(bf16).astype(f32)
    r1 = a - hi
    mid = r1.astype(bf16).astype(f32)
    lo = (r1 - mid).astype(bf16).astype(f32)
    return hi, mid, lo


_COL_U, _COL_ONE, _COL_W, _COL_EMT, _COL_END = 0, 24, 48, 64, 80


def _mlstm_mixer_kernel(x_ref, w_qvo_ref, wkt_ref, wgt_hi_ref, bg_row_ref, norm_g_ref, w_out_ref,
                        ln_g_ref, ln_b_ref, rw_hi_ref, rw_lo_ref, rb_ref, wg32_ref, wu32_ref, wd32_ref,
                        out_ref, meta_t_ref, count_ref, wg16_ref, wu16_ref, wd16_ref,
                        proj_ref, kt_ref, grow_ref, colmat_ref, bexp_ref, h_ref, c_ref, m_ref, *, tiles_per_seq):
    i = pl.program_id(0)
    f32, bf16 = jnp.float32, jnp.bfloat16
    nh, dk, dv, d = ML_HEADS, ML_QK_DIM, ML_V_DIM, D_MODEL
    L = ML_CHUNK
    ts = x_ref.shape[0]
    n_chunks = ts // L
    o_base = ML_QK_W + ML_V_W

    @pl.when(i == 0)
    def _():
        count_ref[...] = jnp.zeros_like(count_ref)

    @pl.when(i % tiles_per_seq == 0)
    def _():
        c_ref[...] = jnp.zeros_like(c_ref)
        m_ref[...] = jnp.zeros_like(m_ref)

    x = x_ref[...]
    x_hi, x_lo = _split_bf16(x)
    nt = (((1,), (1,)), ((), ()))
    kg_t = lax.dot_general(wkt_ref[...], x_hi, nt, preferred_element_type=f32)
    k_t, hi_terms = kg_t[:ML_QK_W, :], kg_t[ML_QK_W:, :]
    g_row = (hi_terms[:2 * nh, :] + hi_terms[2 * nh:, :]
             + lax.dot_general(wgt_hi_ref[...], x_lo, nt, preferred_element_type=f32)) + bg_row_ref[...]
    r_i = lax.broadcasted_iota(jnp.int32, (ts, ts), 0)
    c_i = lax.broadcasted_iota(jnp.int32, (ts, ts), 1)
    tri_row = jnp.where(((r_i // L) == (c_i // L)) & (r_i <= c_i), 1.0, 0.0).astype(bf16)
    lf_hi, lf_lo = _split_bf16(_log_sigmoid(g_row[nh:, :]))
    cum = jnp.dot(jnp.concatenate([lf_hi, lf_lo], axis=0), tri_row, preferred_element_type=f32)
    i_row, b_row = g_row[:nh, :], cum[:nh, :] + cum[nh:, :]
    proj_ref[...] = jnp.dot(x_hi, w_qvo_ref[...], preferred_element_type=f32)

    g = i_row - b_row
    lane_in_chunk = lax.broadcasted_iota(jnp.int32, (nh, ts), 1) & (L - 1)
    cm = g
    shift = 1
    while shift < L:
        cm = jnp.maximum(cm, jnp.where(lane_in_chunk >= shift, pltpu.roll(cm, shift, axis=1), -jnp.inf))
        shift *= 2
    m_prev = m_ref[:, 0:1]
    u_parts, w_parts, gr_parts, gs_parts = [], [], [], []
    for c in range(n_chunks):
        sl = slice(c * L, (c + 1) * L)
        u_c = jnp.maximum(m_prev, cm[:, sl])
        b_last = b_row[:, (c + 1) * L - 1:(c + 1) * L]
        m_new = b_last + u_c[:, L - 1:L]
        u_parts.append(u_c)
        w_parts.append(jnp.exp(m_prev - u_c))
        gr_parts.append(jnp.exp(b_last - b_row[:, sl] + i_row[:, sl] - m_new))
        gs_parts.append(jnp.broadcast_to(jnp.exp(b_last + m_prev - m_new), (nh, L)))
        m_prev = m_new
    m_ref[...] = jnp.broadcast_to(m_prev, m_ref.shape)
    u = jnp.concatenate(u_parts, axis=1)
    w_inter = jnp.concatenate(w_parts, axis=1)
    emt = jnp.exp(-(b_row + u))
    w_hi = w_inter.astype(bf16).astype(f32)
    e_hi = emt.astype(bf16).astype(f32)
    col_rows = jnp.concatenate(
        [*_split3_f32(u), jnp.ones((_COL_W - _COL_ONE, ts), f32), w_hi, w_inter - w_hi, e_hi, emt - e_hi,
         jnp.zeros((LANES - _COL_END, ts), f32)], axis=0)
    colmat_ref[...] = col_rows.T.astype(bf16)
    b_exp = jnp.concatenate([jnp.full((_COL_ONE - _COL_U, ts), -1.0, f32), *_split3_f32(g),
                             jnp.zeros((LANES - _COL_W, ts), f32)], axis=0).astype(bf16)
    for c in range(n_chunks):
        sl = slice(c * L, (c + 1) * L)
        bexp_ref[c] = b_exp[:, sl]
        grow_ref[c] = jnp.concatenate([gr_parts[c], gs_parts[c]], axis=0)
        kt_ref[c] = k_t[:, sl]

    causal = lax.broadcasted_iota(jnp.int32, (L, L), 0) >= lax.broadcasted_iota(jnp.int32, (L, L), 1)
    ones_col = jnp.where(lax.broadcasted_iota(jnp.int32, (L, dv), 1) == 0, 1.0, 0.0).astype(bf16)
    lane_ll = lax.broadcasted_iota(jnp.int32, (L, LANES), 1)
    sel_r = lax.broadcasted_iota(jnp.int32, (LANES, 2 * LANES), 0)
    sel_c = lax.broadcasted_iota(jnp.int32, (LANES, 2 * LANES), 1)
    is_w_row = (sel_r >= _COL_W) & (sel_r < _COL_EMT)
    is_e_row = (sel_r >= _COL_EMT) & (sel_r < _COL_END)
    b_sel = jnp.where((is_w_row & (sel_c < LANES)) | (is_e_row & (sel_c >= LANES)), 1.0, 0.0).astype(bf16)
    r2_r = lax.broadcasted_iota(jnp.int32, (2 * dv, 2 * dv), 0)
    r2_c = lax.broadcasted_iota(jnp.int32, (2 * dv, 2 * dv), 1)
    rhs2 = jnp.where((r2_r < dv) & (r2_c < dv), 1.0 / dv,
                     jnp.where((r2_r == dv) & (r2_c >= dv), 1.0, 0.0)).astype(bf16)

    def chunk_body(c, carry):
        r0 = pl.multiple_of(c * L, L)
        gr = grow_ref[c]
        colmat = colmat_ref[pl.ds(r0, L), :]
        rhs1 = jnp.concatenate([bexp_ref[c], b_sel], axis=1)
        heads = range(nh)
        c_pairs = [c_ref[p] for p in range(nh // 2)]
        kt_pairs = [kt_ref[c, p * LANES:(p + 1) * LANES, :] for p in range(nh // 2)]

        ew, q_m, s_mat, v_ext = [], [], [], []
        for h in heads:
            p, hh = divmod(h, 2)
            lhs = jnp.where((lane_ll & (nh - 1)) == h, colmat, 0)
            ew.append(jnp.dot(lhs, rhs1, preferred_element_type=f32))
            q2 = proj_ref[pl.ds(r0, L), p * LANES:(p + 1) * LANES] * (dk ** -0.5)
            q_m.append(jnp.where((lane_ll >= hh * dk) & (lane_ll < (hh + 1) * dk), q2, 0.0))
            s_mat.append(jnp.dot(q_m[h].astype(bf16), kt_pairs[p].astype(bf16), preferred_element_type=f32))
            v = proj_ref[pl.ds(r0, L), ML_QK_W + h * dv:ML_QK_W + (h + 1) * dv].astype(bf16)
            v_ext.append(jnp.concatenate([v, ones_col], axis=1))
        tot = []
        for h in heads:
            p, hh = divmod(h, 2)
            a = jnp.exp(jnp.where(causal, ew[h][:, :L], -jnp.inf)) * s_mat[h]
            qw = q_m[h] * ew[h][:, L:L + LANES]
            lhs = jnp.concatenate([a.astype(bf16), qw.astype(bf16)], axis=1)
            rhs = jnp.concatenate([v_ext[h], c_pairs[p].astype(bf16)], axis=0)
            tot.append(jnp.dot(lhs, rhs, preferred_element_type=f32))
        for h in heads:
            num = tot[h][:, :dv]
            lhs2 = jnp.concatenate([(num * num).astype(bf16), tot[h][:, dv:].astype(bf16)], axis=1)
            r2 = jnp.dot(lhs2, rhs2, preferred_element_type=f32)
            inv = 1.0 / jnp.maximum(jnp.abs(r2[:, dv:]), ew[h][:, L + LANES:])
            hn = (num * inv) * lax.rsqrt(r2[:, :dv] * inv * inv + HEAD_NORM_EPS)
            o_pre = proj_ref[pl.ds(r0, L), o_base + h * dv:o_base + (h + 1) * dv]
            h_ref[pl.ds(r0, L), h * dv:(h + 1) * dv] = (
                jax.nn.sigmoid(o_pre) * hn * norm_g_ref[:, h * dv:(h + 1) * dv])
        for h in heads:
            p, hh = divmod(h, 2)
            rows = slice(hh * dk, (hh + 1) * dk)
            kg = (kt_pairs[p][rows, :] * gr[h:h + 1, :]).astype(bf16)
            gs = gr[nh + h:nh + h + 1, :]
            c_ref[p, rows, :] = (jnp.concatenate([gs, gs], axis=1) * c_pairs[p][rows, :]
                                 + jnp.dot(kg, v_ext[h], preferred_element_type=f32))
        return carry

    lax.fori_loop(0, n_chunks, chunk_body, 0, unroll=True)

    sub = ts // MIX_SUBTILES
    mixes = [jnp.dot(h_ref[k * sub:(k + 1) * sub, :].astype(bf16), w_out_ref[...], preferred_element_type=f32)
             for k in range(MIX_SUBTILES)]
    x1 = jnp.concatenate([_layer_norm(ALPHA * x[k * sub:(k + 1) * sub, :] + mixes[k], ln_g_ref[...], ln_b_ref[...])
                          for k in range(MIX_SUBTILES)], axis=0)
    meta = _route_and_rank(x1, rw_hi_ref[...], rw_lo_ref[...], rb_ref[...], count_ref)
    out_ref[:, :d] = x1
    out_ref[:, d:] = jnp.concatenate([meta, jnp.zeros((LANES - SUBLANES, meta.shape[1]), meta.dtype)], axis=0).T
    meta_t_ref[...] = meta
    _cast_slabs((wg32_ref, wu32_ref, wd32_ref), (wg16_ref, wu16_ref, wd16_ref))


def _mlstm_mixer_layer(x2d, seq, w_in, b_gate, norm_g, w_out, ln_g, ln_b, rw_hi, rw_lo, rb, layer,
                       expert_weights):
    t, d = x2d.shape
    ts = MIX_TILE
    nh = ML_HEADS
    cast_args, cast_in, cast_shapes, cast_out = _cast_slab_specs(layer, t // ts, expert_weights)
    assert ML_CHUNK == LANES and ts % ML_CHUNK == 0 and 2 * ML_QK_DIM == LANES and ML_V_DIM == LANES
    n_qkvo = 2 * ML_QK_W + 2 * ML_V_W
    n_qvo = ML_QK_W + 2 * ML_V_W
    w_qvo = jnp.concatenate([w_in[:, :ML_QK_W], w_in[:, 2 * ML_QK_W:n_qkvo]], axis=1).astype(jnp.bfloat16)
    w_g = w_in[:, n_qkvo:].astype(jnp.float32)
    wgt_hi, wgt_lo = _split_bf16(w_g.T)
    wkt = jnp.concatenate([w_in[:, ML_QK_W:2 * ML_QK_W].T.astype(jnp.bfloat16), wgt_hi, wgt_lo], axis=0)
    bg_row = b_gate.astype(jnp.float32)[:, None]
    return pl.pallas_call(
        functools.partial(_mlstm_mixer_kernel, tiles_per_seq=seq // ts),
        out_shape=(jax.ShapeDtypeStruct((t, ROW_W), jnp.float32),
                   jax.ShapeDtypeStruct((SUBLANES, t), jnp.float32),
                   jax.ShapeDtypeStruct((CLASS_ROWS, LANES), jnp.float32), *cast_shapes),
        grid=(t // ts,),
        in_specs=[pl.BlockSpec((ts, d), lambda i: (i, 0)),
                  _const_spec((d, n_qvo)), _const_spec((ML_QK_W + 4 * nh, d)),
                  _const_spec((2 * nh, d)),
                  _const_spec((2 * nh, 1)),
                  _const_spec((1, d)), _const_spec((d, d)),
                  _const_spec((1, d)), _const_spec((1, d)),
                  _const_spec((d, LANES)), _const_spec((d, LANES)), _const_spec((N_EXPERTS, 1)), *cast_in],
        out_specs=(pl.BlockSpec((ts, ROW_W), lambda i: (i, 0)),
                   pl.BlockSpec((SUBLANES, ts), lambda i: (0, i)),
                   _const_spec((CLASS_ROWS, LANES)), *cast_out),
        scratch_shapes=[pltpu.VMEM((ts, n_qvo), jnp.float32),
                        pltpu.VMEM((ts // ML_CHUNK, ML_QK_W, ML_CHUNK), jnp.float32),
                        pltpu.VMEM((ts // ML_CHUNK, 2 * nh, ML_CHUNK), jnp.float32),
                        pltpu.VMEM((ts, LANES), jnp.bfloat16),
                        pltpu.VMEM((ts // ML_CHUNK, LANES, ML_CHUNK), jnp.bfloat16),
                        pltpu.VMEM((ts, d), jnp.float32),
                        pltpu.VMEM((nh // 2, 2 * ML_QK_DIM, 2 * ML_V_DIM), jnp.float32),
                        pltpu.VMEM((nh, ML_CHUNK), jnp.float32)],
        compiler_params=pltpu.CompilerParams(dimension_semantics=("arbitrary",),
                                             vmem_limit_bytes=VMEM_LIMIT_BYTES),
        name="mlstm_mixer_route",
    )(x2d, w_qvo, wkt, wgt_hi, bg_row, norm_g[None, :], w_out.astype(jnp.bfloat16),
      ln_g, ln_b, rw_hi, rw_lo, rb, *cast_args)


def _expert_kernel(dest_ref, elo_ref, ehi_ref, nvalid_ref,
                   x_hbm, wg_lo_ref, wu_lo_ref, wd_lo_ref, wg_hi_ref, wu_hi_ref, wd_hi_ref,
                   ln_g_ref, ln_b_ref, out_hbm, xbuf0, xbuf1, obuf0, obuf1, tok_ref, gather_sem, scatter_sem):
    i = pl.program_id(0)
    nb = pl.num_programs(0)
    bm = obuf0.shape[0]
    d = D_MODEL

    @pl.when(i == 0)
    def _():
        def pad_block(b, carry):
            def pad_row(r, c):
                tok_ref[b * bm + r] = 0
                return c
            first = jnp.where(b < nb, nvalid_ref[jnp.minimum(b, nb - 1)], 0)
            return lax.fori_loop(first, bm, pad_row, carry)

        def place(t, c):
            tok_ref[dest_ref[t]] = t
            return c

        lax.fori_loop(0, nb + 1, pad_block, 0)
        lax.fori_loop(0, x_hbm.shape[0], place, 0, unroll=8)
    xbufs, obufs = (xbuf0, xbuf1), (obuf0, obuf1)
    nv = nvalid_ref[i]
    nv_prev = nvalid_ref[jnp.maximum(i - 1, 0)]

    def start_gather(j, s, rows=range(bm)):
        for r in rows:
            tok = tok_ref[j * bm + r]
            pltpu.make_async_copy(x_hbm.at[pl.ds(tok, 1)], xbufs[s].at[pl.ds(r, 1)], gather_sem.at[s]).start()

    def wait_gather(s):
        pltpu.make_async_copy(xbufs[s].at[pl.ds(0, bm)], xbufs[s].at[pl.ds(0, bm)], gather_sem.at[s]).wait()

    def scatter_copy(j, s, r):
        tok = tok_ref[j * bm + r]
        return pltpu.make_async_copy(obufs[s].at[pl.ds(r, 1)], out_hbm.at[pl.ds(tok, 1)], scatter_sem.at[s])

    def wait_scatter(s, n):
        @pl.when(n == bm)
        def _():
            pltpu.make_async_copy(obufs[s], obufs[s], scatter_sem.at[s]).wait()

        @pl.when(n < bm)
        def _():
            def body(r, c):
                pltpu.make_async_copy(obufs[s].at[pl.ds(0, 1)], obufs[s].at[pl.ds(0, 1)], scatter_sem.at[s]).wait()
                return c
            lax.fori_loop(0, n, body, 0)

    @pl.when(i == 0)
    def _():
        for buf in xbufs:
            buf[bm:, :] = jnp.zeros((buf.shape[0] - bm, buf.shape[1]), buf.dtype)
        start_gather(0, 0)

    def step(s):
        @pl.when((i == 0) | (nv_prev > 0))
        def _():
            wait_gather(s)

        @pl.when(nv > 0)
        def _():
            xb = xbufs[s][0:bm, :]
            x = xb[:, :d]
            g_lo = xb[:, d + META_G_LO:d + META_G_LO + 1]
            g_hi = xb[:, d + META_G_HI:d + META_G_HI + 1]
            x16 = x.astype(jnp.bfloat16)
            n_portions = 2 * GATE_CHUNKS

            def gate_proj(w_ref, k0):
                g = jnp.dot(x16, w_ref[0], preferred_element_type=jnp.float32)
                width = g.shape[1] // GATE_CHUNKS
                chunks = []
                for c in range(GATE_CHUNKS):
                    k = k0 + c
                    start_gather(i + 1, 1 - s, range(k * bm // n_portions, (k + 1) * bm // n_portions))
                    zero_row = xbufs[1 - s][bm:bm + 1, 0:LANES]
                    chunks.append(g[:, c * width:(c + 1) * width]
                                  + jnp.concatenate([zero_row] * (width // LANES), axis=1))
                return jnp.concatenate(chunks, axis=1)

            def ffn(k0, wg_ref, wu_ref, wd_ref):
                g = gate_proj(wg_ref, k0)
                u = jnp.dot(x16, wu_ref[0], preferred_element_type=jnp.float32)
                h = (g * jax.nn.sigmoid(g)) * u
                return jnp.dot(h.astype(jnp.bfloat16), wd_ref[0], preferred_element_type=jnp.float32)

            y = (g_lo * ffn(0, wg_lo_ref, wu_lo_ref, wd_lo_ref)
                 + g_hi * ffn(GATE_CHUNKS, wg_hi_ref, wu_hi_ref, wd_hi_ref))
            obufs[s][...] = _layer_norm(ALPHA * x + y, ln_g_ref[...], ln_b_ref[...])

            @pl.when(nv == bm)
            def _():
                for r in range(bm):
                    scatter_copy(i, s, r).start()

            @pl.when(nv < bm)
            def _():
                def body(r, c):
                    scatter_copy(i, s, r).start()
                    return c
                lax.fori_loop(0, nv, body, 0)

        @pl.when(i > 0)
        def _():
            wait_scatter(1 - s, nv_prev)

        @pl.when(i == nb - 1)
        def _():
            wait_scatter(s, nv)

            @pl.when(nv > 0)
            def _():
                wait_gather(1 - s)

    for s in range(2):
        pl.when(i % 2 == s)(functools.partial(step, s))


def _moe_layer(xext, meta_t, counts, w_gate, w_up, w_down, ln_g, ln_b):
    t = xext.shape[0]
    d, f, bm = D_MODEL, D_EXPERT, EXPERT_BLOCK
    n_blocks = t // bm + N_CLASSES
    n_rows = n_blocks * bm

    cls = meta_t[META_CLASS].astype(jnp.int32)
    rank = meta_t[META_RANK].astype(jnp.int32)
    cnt = counts[:N_CLASSES, 0].astype(jnp.int32)
    cls_blocks = (cnt + bm - 1) // bm
    blk_end = jnp.cumsum(cls_blocks)
    blk_start = blk_end - cls_blocks
    total_blocks = blk_end[-1]
    row_start = jnp.sum(jnp.where(cls[None, :] == jnp.arange(N_CLASSES, dtype=jnp.int32)[:, None],
                                  blk_start[:, None] * bm, 0), axis=0)
    dest = (row_start + rank).astype(jnp.int32)
    blk = jnp.arange(n_blocks, dtype=jnp.int32)
    blk_cls = jnp.sum(jnp.minimum(blk, total_blocks - 1)[:, None] >= blk_end[None, :], axis=1).astype(jnp.int32)
    blk_cls = jnp.minimum(blk_cls, N_CLASSES - 1)
    nvalid = jnp.clip(cnt[blk_cls] - (blk - blk_start[blk_cls]) * bm, 0, bm)
    nvalid = jnp.where(blk < total_blocks, nvalid, 0).astype(jnp.int32)
    e_lo = jnp.asarray(CLASS_E_LO)[blk_cls]
    e_hi = jnp.asarray(CLASS_E_HI)[blk_cls]

    w_gate, w_up, w_down = (w_gate.reshape(N_EXPERTS, d, f), w_up.reshape(N_EXPERTS, d, f),
                            w_down.reshape(N_EXPERTS, f, d))
    w_lo = lambda shape: pl.BlockSpec((1,) + shape, lambda i, tok, elo, ehi, nv: (elo[i], 0, 0))
    w_hi = lambda shape: pl.BlockSpec((1,) + shape, lambda i, tok, elo, ehi, nv: (ehi[i], 0, 0))
    vec = pl.BlockSpec((1, d), lambda i, tok, elo, ehi, nv: (0, 0))
    grid_spec = pltpu.PrefetchScalarGridSpec(
        num_scalar_prefetch=4,
        grid=(n_blocks,),
        in_specs=[pl.BlockSpec(memory_space=pl.ANY),
                  w_lo((d, f)), w_lo((d, f)), w_lo((f, d)),
                  w_hi((d, f)), w_hi((d, f)), w_hi((f, d)),
                  vec, vec],
        out_specs=pl.BlockSpec(memory_space=pl.ANY),
        scratch_shapes=[pltpu.VMEM((bm + SUBLANES, ROW_W), jnp.float32), pltpu.VMEM((bm + SUBLANES, ROW_W), jnp.float32),
                        pltpu.VMEM((bm, d), jnp.float32), pltpu.VMEM((bm, d), jnp.float32),
                        pltpu.SMEM((n_rows + bm,), jnp.int32),
                        pltpu.SemaphoreType.DMA((2,)),
                        pltpu.SemaphoreType.DMA((2,))],
    )
    return pl.pallas_call(
        _expert_kernel,
        out_shape=jax.ShapeDtypeStruct((t, d), jnp.float32),
        grid_spec=grid_spec,
        compiler_params=pltpu.CompilerParams(dimension_semantics=("arbitrary",),
                                             vmem_limit_bytes=VMEM_LIMIT_BYTES),
        name="expert_pair_ffn",
    )(dest, e_lo, e_hi, nvalid, xext, w_gate, w_up, w_down, w_gate, w_up, w_down, ln_g, ln_b)


def kernel(x, conv_w_in, conv_w, conv_w_out, ml_w_in, ml_b_gate, ml_norm_g, ml_w_out, ln_mix_g, ln_mix_b,
           ln_ffn_g, ln_ffn_b, router_w, router_b, exp_w_gate, exp_w_up, exp_w_down):
    bsz, seq, d = x.shape
    assert d == D_MODEL and seq % MIX_TILE == 0 and (bsz * seq) % EXPERT_BLOCK == 0
    bf16 = jnp.bfloat16
    vec = lambda a: a.astype(jnp.float32)[None, :]
    (rw_hi, rw_lo), rb = _prep_router(router_w, router_b)
    x2d = x.reshape(bsz * seq, d).astype(jnp.float32)
    expert_weights = (exp_w_gate.astype(jnp.float32), exp_w_up.astype(jnp.float32),
                      exp_w_down.astype(jnp.float32))
    for i in range(DEPTH):
        j = i // 2
        if i % 2 == 0:
            xext, meta_t, counts, *w16 = _conv_mixer_layer(
                x2d, seq, conv_w_in[j].astype(bf16), _pad_rows(conv_w[j].astype(jnp.float32)),
                conv_w_out[j].astype(bf16), vec(ln_mix_g[i]), vec(ln_mix_b[i]), rw_hi, rw_lo, rb,
                i, expert_weights)
        else:
            xext, meta_t, counts, *w16 = _mlstm_mixer_layer(
                x2d, seq, ml_w_in[j], ml_b_gate[j], ml_norm_g[j].astype(jnp.float32), ml_w_out[j],
                vec(ln_mix_g[i]), vec(ln_mix_b[i]), rw_hi, rw_lo, rb, i, expert_weights)
        x2d = _moe_layer(xext, meta_t, counts, *w16, vec(ln_ffn_g[i]), vec(ln_ffn_b[i]))
    return x2d.reshape(bsz, seq, d).astype(x.dtype)
```

```python
import functools
import itertools

import numpy as np
import jax
import jax.numpy as jnp
from jax import lax
from jax.experimental import pallas as pl
from jax.experimental.pallas import tpu as pltpu

D_MODEL = 1024
DEPTH = 2
CONV_WIDTH = 3
ML_HEADS = 8
ML_QK_DIM = D_MODEL // (2 * ML_HEADS)
ML_V_DIM = D_MODEL // ML_HEADS
ML_QK_W = ML_HEADS * ML_QK_DIM
ML_V_W = ML_HEADS * ML_V_DIM
N_EXPERTS = 16
N_GROUPS = 4
EXPERTS_PER_GROUP = N_EXPERTS // N_GROUPS
D_EXPERT = 3 * D_MODEL // 2
ALPHA = (2 * DEPTH) ** 0.25
LN_EPS = 1e-5
HEAD_NORM_EPS = 1e-6

LANES = 128
SUBLANES = 8
VMEM_LIMIT_BYTES = 56 * 1024 * 1024

PAIRS = tuple(itertools.combinations(range(EXPERTS_PER_GROUP), 2))
N_PAIRS = len(PAIRS)
N_CLASSES = N_GROUPS * N_PAIRS
CLASS_ROWS = -(-N_CLASSES // 16) * 16
CLASS_E_LO = np.array([g * EXPERTS_PER_GROUP + a for g in range(N_GROUPS) for a, _ in PAIRS], np.int32)
CLASS_E_HI = np.array([g * EXPERTS_PER_GROUP + b for g in range(N_GROUPS) for _, b in PAIRS], np.int32)

META_G_LO, META_G_HI, META_CLASS, META_RANK = 0, 1, 2, 3
ROW_W = D_MODEL + LANES

MIX_TILE = 512
MIX_SUBTILES = 2
EXPERT_BLOCK = 256
GATE_CHUNKS = 6
ML_CHUNK = 128


def _layer_norm(z, g, b):
    mu = jnp.mean(z, axis=-1, keepdims=True)
    zc = z - mu
    var = jnp.mean(zc * zc, axis=-1, keepdims=True)
    return zc * lax.rsqrt(var + LN_EPS) * g + b


def _split_bf16(a):
    hi = a.astype(jnp.bfloat16)
    lo = (a - hi.astype(jnp.float32)).astype(jnp.bfloat16)
    return hi, lo


def _route_and_rank(x1, rw_hi, rw_lo, rb_col, count_ref):
    ts = x1.shape[0]
    f32, bf16 = jnp.float32, jnp.bfloat16
    n_g, n_m = N_GROUPS, EXPERTS_PER_GROUP

    x_hi, x_lo = _split_bf16(x1)
    hi_terms = jnp.dot(x_hi, jnp.concatenate([rw_hi, rw_lo], axis=1), preferred_element_type=f32)
    logits = (hi_terms[:, :LANES] + hi_terms[:, LANES:]
              + jnp.dot(x_lo, rw_hi, preferred_element_type=f32))
    lt = logits.T[:N_EXPERTS, :]
    ex = jnp.exp(lt - jnp.max(lt, axis=0, keepdims=True))
    probs = ex / jnp.sum(ex, axis=0, keepdims=True)
    sel = probs + rb_col
    member = lambda a, j: a[j * n_g:(j + 1) * n_g, :]

    in_top2 = []
    for j in range(n_m):
        beaten_by = jnp.zeros((n_g, ts), jnp.int32)
        for k in range(n_m):
            if k != j:
                wins = member(sel, k) > member(sel, j)
                if k < j:
                    wins = wins | (member(sel, k) == member(sel, j))
                beaten_by = beaten_by + wins.astype(jnp.int32)
        in_top2.append(beaten_by < 2)
    score = sum(jnp.where(in_top2[j], member(sel, j), 0.0) for j in range(n_m))
    group = lax.broadcasted_iota(jnp.int32, (n_g, ts), 0)
    best = jnp.max(score, axis=0, keepdims=True)
    g_best = jnp.min(jnp.where(score == best, group, n_g), axis=0, keepdims=True)
    in_best = group == g_best
    picked = [jnp.max(jnp.where(in_top2[j] & in_best, 1, 0), axis=0, keepdims=True) for j in range(n_m)]
    p_pick = [jnp.sum(jnp.where(in_top2[j] & in_best, member(probs, j), 0.0), axis=0, keepdims=True)
              for j in range(n_m)]
    a = functools.reduce(jnp.minimum, [jnp.where(picked[j] > 0, j, n_m) for j in range(n_m)])
    b = functools.reduce(jnp.maximum, [jnp.where(picked[j] > 0, j, -1) for j in range(n_m)])
    p_lo = sum(jnp.where(a == j, p_pick[j], 0.0) for j in range(n_m))
    p_hi = sum(jnp.where(b == j, p_pick[j], 0.0) for j in range(n_m))
    g_lo = p_lo / (p_lo + p_hi)
    g_hi = p_hi / (p_lo + p_hi)
    pair = ((a * (2 * n_m - 1 - a)) >> 1) + (b - a - 1)
    cls = g_best * N_PAIRS + pair

    n_rows = count_ref.shape[0]
    onehot = jnp.where(lax.broadcasted_iota(jnp.int32, (n_rows, ts), 0) == cls, 1.0, 0.0)
    earlier = jnp.where(lax.broadcasted_iota(jnp.int32, (ts, ts), 0)
                        < lax.broadcasted_iota(jnp.int32, (ts, ts), 1), 1.0, 0.0).astype(bf16)
    before = jnp.dot(onehot.astype(bf16), earlier, preferred_element_type=f32)
    running = count_ref[...]
    running_ts = jnp.concatenate([running] * (ts // LANES), axis=1)
    rank = jnp.sum(onehot * (before + running_ts), axis=0, keepdims=True)
    count_ref[...] = running + jnp.dot(onehot.astype(bf16), jnp.ones((ts, LANES), bf16),
                                       preferred_element_type=f32)
    return jnp.concatenate([g_lo, g_hi, cls.astype(f32), rank, jnp.zeros((SUBLANES - 4, ts), f32)], axis=0)


def _cast_slabs(w32_refs, w16_refs):
    for w32_ref, w16_ref in zip(w32_refs, w16_refs):
        w16_ref[...] = w32_ref[0].astype(w16_ref.dtype)


def _cast_slab_specs(layer, n_steps, weights):
    args, in_specs, out_shapes, out_specs = [], [], [], []
    for w in weights:
        n_layers, n_exp, rows, cols = w.shape
        slab, rem = divmod(n_exp * rows, n_steps)
        assert rem == 0 and slab % (2 * SUBLANES) == 0
        args.append(w.reshape(n_layers, n_exp * rows, cols))
        in_specs.append(pl.BlockSpec((1, slab, cols), lambda i: (layer, i, 0)))
        out_shapes.append(jax.ShapeDtypeStruct((n_exp * rows, cols), jnp.bfloat16))
        out_specs.append(pl.BlockSpec((slab, cols), lambda i: (i, 0)))
    return args, in_specs, out_shapes, out_specs


def _conv_mixer_kernel(x_ref, w_in_ref, w_conv_ref, w_out_ref, ln_g_ref, ln_b_ref,
                       rw_hi_ref, rw_lo_ref, rb_ref, wg32_ref, wu32_ref, wd32_ref,
                       out_ref, meta_t_ref, count_ref, wg16_ref, wu16_ref, wd16_ref, carry_ref, *, tiles_per_seq):
    i = pl.program_id(0)

    @pl.when(i == 0)
    def _():
        count_ref[...] = jnp.zeros_like(count_ref)

    @pl.when(i % tiles_per_seq == 0)
    def _():
        carry_ref[...] = jnp.zeros_like(carry_ref)

    f32 = jnp.float32
    d = D_MODEL
    ts = x_ref.shape[0]
    sub = ts // MIX_SUBTILES
    xs = [x_ref[k * sub:(k + 1) * sub, :] for k in range(MIX_SUBTILES)]
    projs = [jnp.dot(x.astype(jnp.bfloat16), w_in_ref[...], preferred_element_type=f32) for x in xs]
    row = lax.broadcasted_iota(jnp.int32, (sub, d), 0)
    wc = w_conv_ref[...]
    prev = carry_ref[...]
    gated = []
    for proj in projs:
        gate_b, gate_c, h = proj[:, :d], proj[:, d:2 * d], proj[:, 2 * d:]
        u = gate_c * h
        u1 = jnp.where(row == 0, prev[SUBLANES - 1:SUBLANES, :], pltpu.roll(u, 1, axis=0))
        u2 = jnp.where(row == 0, prev[SUBLANES - 2:SUBLANES - 1, :],
                       jnp.where(row == 1, prev[SUBLANES - 1:SUBLANES, :], pltpu.roll(u, 2, axis=0)))
        prev = u[sub - SUBLANES:, :]
        conv = wc[0:1, :] * u2 + wc[1:2, :] * u1 + wc[2:3, :] * u
        gated.append((gate_b * conv).astype(jnp.bfloat16))
    carry_ref[...] = prev
    mixes = [jnp.dot(v, w_out_ref[...], preferred_element_type=f32) for v in gated]
    x1 = jnp.concatenate([_layer_norm(ALPHA * x + mix, ln_g_ref[...], ln_b_ref[...])
                          for x, mix in zip(xs, mixes)], axis=0)
    meta = _route_and_rank(x1, rw_hi_ref[...], rw_lo_ref[...], rb_ref[...], count_ref)
    out_ref[:, :d] = x1
    out_ref[:, d:] = jnp.concatenate([meta, jnp.zeros((LANES - SUBLANES, meta.shape[1]), meta.dtype)], axis=0).T
    meta_t_ref[...] = meta
    _cast_slabs((wg32_ref, wu32_ref, wd32_ref), (wg16_ref, wu16_ref, wd16_ref))


def _const_spec(shape):
    return pl.BlockSpec(shape, lambda i: (0,) * len(shape))


def _pad_rows(a, rows=SUBLANES):
    return jnp.pad(a, ((0, rows - a.shape[0]), (0, 0)))


def _prep_router(router_w, router_b):
    order = np.array([g * EXPERTS_PER_GROUP + j for j in range(EXPERTS_PER_GROUP) for g in range(N_GROUPS)])
    rw = jnp.pad(router_w.astype(jnp.float32)[:, order], ((0, 0), (0, LANES - N_EXPERTS)))
    rb = router_b.astype(jnp.float32)[order][:, None]
    return _split_bf16(rw), rb


def _conv_mixer_layer(x2d, seq, w_in, w_conv, w_out, ln_g, ln_b, rw_hi, rw_lo, rb, layer, expert_weights):
    t, d = x2d.shape
    ts = MIX_TILE
    grid = (t // ts,)
    cast_args, cast_in, cast_shapes, cast_out = _cast_slab_specs(layer, grid[0], expert_weights)
    return pl.pallas_call(
        functools.partial(_conv_mixer_kernel, tiles_per_seq=seq // ts),
        out_shape=(jax.ShapeDtypeStruct((t, ROW_W), jnp.float32),
                   jax.ShapeDtypeStruct((SUBLANES, t), jnp.float32),
                   jax.ShapeDtypeStruct((CLASS_ROWS, LANES), jnp.float32), *cast_shapes),
        grid=grid,
        in_specs=[pl.BlockSpec((ts, d), lambda i: (i, 0)),
                  _const_spec((d, 3 * d)), _const_spec((SUBLANES, d)), _const_spec((d, d)),
                  _const_spec((1, d)), _const_spec((1, d)),
                  _const_spec((d, LANES)), _const_spec((d, LANES)), _const_spec((N_EXPERTS, 1)), *cast_in],
        out_specs=(pl.BlockSpec((ts, ROW_W), lambda i: (i, 0)),
                   pl.BlockSpec((SUBLANES, ts), lambda i: (0, i)),
                   _const_spec((CLASS_ROWS, LANES)), *cast_out),
        scratch_shapes=[pltpu.VMEM((SUBLANES, d), jnp.float32)],
        compiler_params=pltpu.CompilerParams(dimension_semantics=("arbitrary",),
                                             vmem_limit_bytes=VMEM_LIMIT_BYTES),
        name="conv_mixer_route",
    )(x2d, w_in, w_conv, w_out, ln_g, ln_b, rw_hi, rw_lo, rb, *cast_args)


def _log_sigmoid(z):
    return jnp.minimum(z, 0.0) - jnp.log1p(jnp.exp(-jnp.abs(z)))


def _split3_f32(a):
    f32, bf16 = jnp.float32, jnp.bfloat16
    hi = a.astype(bf16).astype(f32)
    r1 = a - hi
    mid = r1.astype(bf16).astype(f32)
    lo = (r1 - mid).astype(bf16).astype(f32)
    return hi, mid, lo


_COL_U, _COL_ONE, _COL_W, _COL_EMT, _COL_END = 0, 24, 48, 64, 80


def _mlstm_mixer_kernel(x_ref, w_qvo_ref, wkt_ref, wgt_hi_ref, bg_row_ref, norm_g_ref, w_out_ref,
                        ln_g_ref, ln_b_ref, rw_hi_ref, rw_lo_ref, rb_ref, wg32_ref, wu32_ref, wd32_ref,
                        out_ref, meta_t_ref, count_ref, wg16_ref, wu16_ref, wd16_ref,
                        proj_ref, kt_ref, grow_ref, colmat_ref, bexp_ref, h_ref, c_ref, m_ref, *, tiles_per_seq):
    i = pl.program_id(0)
    f32, bf16 = jnp.float32, jnp.bfloat16
    nh, dk, dv, d = ML_HEADS, ML_QK_DIM, ML_V_DIM, D_MODEL
    L = ML_CHUNK
    ts = x_ref.shape[0]
    n_chunks = ts // L
    o_base = ML_QK_W + ML_V_W

    @pl.when(i == 0)
    def _():
        count_ref[...] = jnp.zeros_like(count_ref)

    @pl.when(i % tiles_per_seq == 0)
    def _():
        c_ref[...] = jnp.zeros_like(c_ref)
        m_ref[...] = jnp.zeros_like(m_ref)

    x = x_ref[...]
    x_hi, x_lo = _split_bf16(x)
    nt = (((1,), (1,)), ((), ()))
    kg_t = lax.dot_general(wkt_ref[...], x_hi, nt, preferred_element_type=f32)
    k_t, hi_terms = kg_t[:ML_QK_W, :], kg_t[ML_QK_W:, :]
    g_row = (hi_terms[:2 * nh, :] + hi_terms[2 * nh:, :]
             + lax.dot_general(wgt_hi_ref[...], x_lo, nt, preferred_element_type=f32)) + bg_row_ref[...]
    r_i = lax.broadcasted_iota(jnp.int32, (ts, ts), 0)
    c_i = lax.broadcasted_iota(jnp.int32, (ts, ts), 1)
    tri_row = jnp.where(((r_i // L) == (c_i // L)) & (r_i <= c_i), 1.0, 0.0).astype(bf16)
    lf_hi, lf_lo = _split_bf16(_log_sigmoid(g_row[nh:, :]))
    cum = jnp.dot(jnp.concatenate([lf_hi, lf_lo], axis=0), tri_row, preferred_element_type=f32)
    i_row, b_row = g_row[:nh, :], cum[:nh, :] + cum[nh:, :]
    proj_ref[...] = jnp.dot(x_hi, w_qvo_ref[...], preferred_element_type=f32)

    g = i_row - b_row
    lane_in_chunk = lax.broadcasted_iota(jnp.int32, (nh, ts), 1) & (L - 1)
    cm = g
    shift = 1
    while shift < L:
        cm = jnp.maximum(cm, jnp.where(lane_in_chunk >= shift, pltpu.roll(cm, shift, axis=1), -jnp.inf))
        shift *= 2
    m_prev = m_ref[:, 0:1]
    u_parts, w_parts, gr_parts, gs_parts = [], [], [], []
    for c in range(n_chunks):
        sl = slice(c * L, (c + 1) * L)
        u_c = jnp.maximum(m_prev, cm[:, sl])
        b_last = b_row[:, (c + 1) * L - 1:(c + 1) * L]
        m_new = b_last + u_c[:, L - 1:L]
        u_parts.append(u_c)
        w_parts.append(jnp.exp(m_prev - u_c))
        gr_parts.append(jnp.exp(b_last - b_row[:, sl] + i_row[:, sl] - m_new))
        gs_parts.append(jnp.broadcast_to(jnp.exp(b_last + m_prev - m_new), (nh, L)))
        m_prev = m_new
    m_ref[...] = jnp.broadcast_to(m_prev, m_ref.shape)
    u = jnp.concatenate(u_parts, axis=1)
    w_inter = jnp.concatenate(w_parts, axis=1)
    emt = jnp.exp(-(b_row + u))
    w_hi = w_inter.astype(bf16).astype(f32)
    e_hi = emt.astype(bf16).astype(f32)
    col_rows = jnp.concatenate(
        [*_split3_f32(u), jnp.ones((_COL_W - _COL_ONE, ts), f32), w_hi, w_inter - w_hi, e_hi, emt - e_hi,
         jnp.zeros((LANES - _COL_END, ts), f32)], axis=0)
    colmat_ref[...] = col_rows.T.astype(bf16)
    b_exp = jnp.concatenate([jnp.full((_COL_ONE - _COL_U, ts), -1.0, f32), *_split3_f32(g),
                             jnp.zeros((LANES - _COL_W, ts), f32)], axis=0).astype(bf16)
    for c in range(n_chunks):
        sl = slice(c * L, (c + 1) * L)
        bexp_ref[c] = b_exp[:, sl]
        grow_ref[c] = jnp.concatenate([gr_parts[c], gs_parts[c]], axis=0)
        kt_ref[c] = k_t[:, sl]

    causal = lax.broadcasted_iota(jnp.int32, (L, L), 0) >= lax.broadcasted_iota(jnp.int32, (L, L), 1)
    ones_col = jnp.where(lax.broadcasted_iota(jnp.int32, (L, dv), 1) == 0, 1.0, 0.0).astype(bf16)
    lane_ll = lax.broadcasted_iota(jnp.int32, (L, LANES), 1)
    sel_r = lax.broadcasted_iota(jnp.int32, (LANES, 2 * LANES), 0)
    sel_c = lax.broadcasted_iota(jnp.int32, (LANES, 2 * LANES), 1)
    is_w_row = (sel_r >= _COL_W) & (sel_r < _COL_EMT)
    is_e_row = (sel_r >= _COL_EMT) & (sel_r < _COL_END)
    b_sel = jnp.where((is_w_row & (sel_c < LANES)) | (is_e_row & (sel_c >= LANES)), 1.0, 0.0).astype(bf16)
    r2_r = lax.broadcasted_iota(jnp.int32, (2 * dv, 2 * dv), 0)
    r2_c = lax.broadcasted_iota(jnp.int32, (2 * dv, 2 * dv), 1)
    rhs2 = jnp.where((r2_r < dv) & (r2_c < dv), 1.0 / dv,
                     jnp.where((r2_r == dv) & (r2_c >= dv), 1.0, 0.0)).astype(bf16)

    def chunk_body(c, carry):
        r0 = pl.multiple_of(c * L, L)
        gr = grow_ref[c]
        colmat = colmat_ref[pl.ds(r0, L), :]
        rhs1 = jnp.concatenate([bexp_ref[c], b_sel], axis=1)
        heads = range(nh)
        c_pairs = [c_ref[p] for p in range(nh // 2)]
        kt_pairs = [kt_ref[c, p * LANES:(p + 1) * LANES, :] for p in range(nh // 2)]

        ew, q_m, s_mat, v_ext = [], [], [], []
        for h in heads:
            p, hh = divmod(h, 2)
            lhs = jnp.where((lane_ll & (nh - 1)) == h, colmat, 0)
            ew.append(jnp.dot(lhs, rhs1, preferred_element_type=f32))
            q2 = proj_ref[pl.ds(r0, L), p * LANES:(p + 1) * LANES] * (dk ** -0.5)
            q_m.append(jnp.where((lane_ll >= hh * dk) & (lane_ll < (hh + 1) * dk), q2, 0.0))
            s_mat.append(jnp.dot(q_m[h].astype(bf16), kt_pairs[p].astype(bf16), preferred_element_type=f32))
            v = proj_ref[pl.ds(r0, L), ML_QK_W + h * dv:ML_QK_W + (h + 1) * dv].astype(bf16)
            v_ext.append(jnp.concatenate([v, ones_col], axis=1))
        tot = []
        for h in heads:
            p, hh = divmod(h, 2)
            a = jnp.exp(jnp.where(causal, ew[h][:, :L], -jnp.inf)) * s_mat[h]
            qw = q_m[h] * ew[h][:, L:L + LANES]
            lhs = jnp.concatenate([a.astype(bf16), qw.astype(bf16)], axis=1)
            rhs = jnp.concatenate([v_ext[h], c_pairs[p].astype(bf16)], axis=0)
            tot.append(jnp.dot(lhs, rhs, preferred_element_type=f32))
        for h in heads:
            num = tot[h][:, :dv]
            lhs2 = jnp.concatenate([(num * num).astype(bf16), tot[h][:, dv:].astype(bf16)], axis=1)
            r2 = jnp.dot(lhs2, rhs2, preferred_element_type=f32)
            inv = 1.0 / jnp.maximum(jnp.abs(r2[:, dv:]), ew[h][:, L + LANES:])
            hn = (num * inv) * lax.rsqrt(r2[:, :dv] * inv * inv + HEAD_NORM_EPS)
            o_pre = proj_ref[pl.ds(r0, L), o_base + h * dv:o_base + (h + 1) * dv]
            h_ref[pl.ds(r0, L), h * dv:(h + 1) * dv] = (
                jax.nn.sigmoid(o_pre) * hn * norm_g_ref[:, h * dv:(h + 1) * dv])
        for h in heads:
            p, hh = divmod(h, 2)
            rows = slice(hh * dk, (hh + 1) * dk)
            kg = (kt_pairs[p][rows, :] * gr[h:h + 1, :]).astype(bf16)
            gs = gr[nh + h:nh + h + 1, :]
            c_ref[p, rows, :] = (jnp.concatenate([gs, gs], axis=1) * c_pairs[p][rows, :]
                                 + jnp.dot(kg, v_ext[h], preferred_element_type=f32))
        return carry

    lax.fori_loop(0, n_chunks, chunk_body, 0, unroll=True)

    sub = ts // MIX_SUBTILES
    mixes = [jnp.dot(h_ref[k * sub:(k + 1) * sub, :].astype(bf16), w_out_ref[...], preferred_element_type=f32)
             for k in range(MIX_SUBTILES)]
    x1 = jnp.concatenate([_layer_norm(ALPHA * x[k * sub:(k + 1) * sub, :] + mixes[k], ln_g_ref[...], ln_b_ref[...])
                          for k in range(MIX_SUBTILES)], axis=0)
    meta = _route_and_rank(x1, rw_hi_ref[...], rw_lo_ref[...], rb_ref[...], count_ref)
    out_ref[:, :d] = x1
    out_ref[:, d:] = jnp.concatenate([meta, jnp.zeros((LANES - SUBLANES, meta.shape[1]), meta.dtype)], axis=0).T
    meta_t_ref[...] = meta
    _cast_slabs((wg32_ref, wu32_ref, wd32_ref), (wg16_ref, wu16_ref, wd16_ref))


def _mlstm_mixer_layer(x2d, seq, w_in, b_gate, norm_g, w_out, ln_g, ln_b, rw_hi, rw_lo, rb, layer,
                       expert_weights):
    t, d = x2d.shape
    ts = MIX_TILE
    nh = ML_HEADS
    cast_args, cast_in, cast_shapes, cast_out = _cast_slab_specs(layer, t // ts, expert_weights)
    assert ML_CHUNK == LANES and ts % ML_CHUNK == 0 and 2 * ML_QK_DIM == LANES and ML_V_DIM == LANES
    n_qkvo = 2 * ML_QK_W + 2 * ML_V_W
    n_qvo = ML_QK_W + 2 * ML_V_W
    w_qvo = jnp.concatenate([w_in[:, :ML_QK_W], w_in[:, 2 * ML_QK_W:n_qkvo]], axis=1).astype(jnp.bfloat16)
    w_g = w_in[:, n_qkvo:].astype(jnp.float32)
    wgt_hi, wgt_lo = _split_bf16(w_g.T)
    wkt = jnp.concatenate([w_in[:, ML_QK_W:2 * ML_QK_W].T.astype(jnp.bfloat16), wgt_hi, wgt_lo], axis=0)
    bg_row = b_gate.astype(jnp.float32)[:, None]
    return pl.pallas_call(
        functools.partial(_mlstm_mixer_kernel, tiles_per_seq=seq // ts),
        out_shape=(jax.ShapeDtypeStruct((t, ROW_W), jnp.float32),
                   jax.ShapeDtypeStruct((SUBLANES, t), jnp.float32),
                   jax.ShapeDtypeStruct((CLASS_ROWS, LANES), jnp.float32), *cast_shapes),
        grid=(t // ts,),
        in_specs=[pl.BlockSpec((ts, d), lambda i: (i, 0)),
                  _const_spec((d, n_qvo)), _const_spec((ML_QK_W + 4 * nh, d)),
                  _const_spec((2 * nh, d)),
                  _const_spec((2 * nh, 1)),
                  _const_spec((1, d)), _const_spec((d, d)),
                  _const_spec((1, d)), _const_spec((1, d)),
                  _const_spec((d, LANES)), _const_spec((d, LANES)), _const_spec((N_EXPERTS, 1)), *cast_in],
        out_specs=(pl.BlockSpec((ts, ROW_W), lambda i: (i, 0)),
                   pl.BlockSpec((SUBLANES, ts), lambda i: (0, i)),
                   _const_spec((CLASS_ROWS, LANES)), *cast_out),
        scratch_shapes=[pltpu.VMEM((ts, n_qvo), jnp.float32),
                        pltpu.VMEM((ts // ML_CHUNK, ML_QK_W, ML_CHUNK), jnp.float32),
                        pltpu.VMEM((ts // ML_CHUNK, 2 * nh, ML_CHUNK), jnp.float32),
                        pltpu.VMEM((ts, LANES), jnp.bfloat16),
                        pltpu.VMEM((ts // ML_CHUNK, LANES, ML_CHUNK), jnp.bfloat16),
                        pltpu.VMEM((ts, d), jnp.float32),
                        pltpu.VMEM((nh // 2, 2 * ML_QK_DIM, 2 * ML_V_DIM), jnp.float32),
                        pltpu.VMEM((nh, ML_CHUNK), jnp.float32)],
        compiler_params=pltpu.CompilerParams(dimension_semantics=("arbitrary",),
                                             vmem_limit_bytes=VMEM_LIMIT_BYTES),
        name="mlstm_mixer_route",
    )(x2d, w_qvo, wkt, wgt_hi, bg_row, norm_g[None, :], w_out.astype(jnp.bfloat16),
      ln_g, ln_b, rw_hi, rw_lo, rb, *cast_args)


def _expert_kernel(dest_ref, elo_ref, ehi_ref, nvalid_ref,
                   x_hbm, wg_lo_ref, wu_lo_ref, wd_lo_ref, wg_hi_ref, wu_hi_ref, wd_hi_ref,
                   ln_g_ref, ln_b_ref, out_hbm, xbuf0, xbuf1, obuf0, obuf1, tok_ref, gather_sem, scatter_sem):
    i = pl.program_id(0)
    nb = pl.num_programs(0)
    bm = obuf0.shape[0]
    d = D_MODEL

    @pl.when(i == 0)
    def _():
        def pad_block(b, carry):
            def pad_row(r, c):
                tok_ref[b * bm + r] = 0
                return c
            first = jnp.where(b < nb, nvalid_ref[jnp.minimum(b, nb - 1)], 0)
            is_gathered = (b == 0) | (nvalid_ref[jnp.maximum(b - 1, 0)] > 0)
            return lax.fori_loop(jnp.where(is_gathered, first, bm), bm, pad_row, carry)

        def place(t, c):
            tok_ref[dest_ref[t]] = t
            return c

        lax.fori_loop(0, nb + 1, pad_block, 0)
        lax.fori_loop(0, x_hbm.shape[0], place, 0, unroll=8)
    xbufs, obufs = (xbuf0, xbuf1), (obuf0, obuf1)
    nv = nvalid_ref[i]
    nv_prev = nvalid_ref[jnp.maximum(i - 1, 0)]

    def start_gather(j, s, rows=range(bm)):
        for r in rows:
            tok = tok_ref[j * bm + r]
            pltpu.make_async_copy(x_hbm.at[pl.ds(tok, 1)], xbufs[s].at[pl.ds(r, 1)], gather_sem.at[s]).start()

    def wait_gather(s):
        pltpu.make_async_copy(xbufs[s].at[pl.ds(0, bm)], xbufs[s].at[pl.ds(0, bm)], gather_sem.at[s]).wait()

    def scatter_copy(j, s, r):
        tok = tok_ref[j * bm + r]
        return pltpu.make_async_copy(obufs[s].at[pl.ds(r, 1)], out_hbm.at[pl.ds(tok, 1)], scatter_sem.at[s])

    def wait_scatter(s, n):
        @pl.when(n == bm)
        def _():
            pltpu.make_async_copy(obufs[s], obufs[s], scatter_sem.at[s]).wait()

        @pl.when(n < bm)
        def _():
            def body(r, c):
                pltpu.make_async_copy(obufs[s].at[pl.ds(0, 1)], obufs[s].at[pl.ds(0, 1)], scatter_sem.at[s]).wait()
                return c
            lax.fori_loop(0, n, body, 0)

    @pl.when(i == 0)
    def _():
        for buf in xbufs:
            buf[bm:, :] = jnp.zeros((buf.shape[0] - bm, buf.shape[1]), buf.dtype)
        start_gather(0, 0)

    def step(s):
        @pl.when((i == 0) | (nv_prev > 0))
        def _():
            wait_gather(s)

        @pl.when(nv > 0)
        def _():
            xb = xbufs[s][0:bm, :]
            x = xb[:, :d]
            g_lo = xb[:, d + META_G_LO:d + META_G_LO + 1]
            g_hi = xb[:, d + META_G_HI:d + META_G_HI + 1]
            x16 = x.astype(jnp.bfloat16)
            n_portions = 2 * GATE_CHUNKS

            def gate_proj(w_ref, k0):
                g = jnp.dot(x16, w_ref[0], preferred_element_type=jnp.float32)
                width = g.shape[1] // GATE_CHUNKS
                chunks = []
                for c in range(GATE_CHUNKS):
                    k = k0 + c
                    start_gather(i + 1, 1 - s, range(k * bm // n_portions, (k + 1) * bm // n_portions))
                    zero_row = xbufs[1 - s][bm:bm + 1, 0:LANES]
                    chunks.append(g[:, c * width:(c + 1) * width]
                                  + jnp.concatenate([zero_row] * (width // LANES), axis=1))
                return jnp.concatenate(chunks, axis=1)

            def ffn(k0, wg_ref, wu_ref, wd_ref):
                g = gate_proj(wg_ref, k0)
                u = jnp.dot(x16, wu_ref[0], preferred_element_type=jnp.float32)
                h = (g * jax.nn.sigmoid(g)) * u
                return jnp.dot(h.astype(jnp.bfloat16), wd_ref[0], preferred_element_type=jnp.float32)

            y = (g_lo * ffn(0, wg_lo_ref, wu_lo_ref, wd_lo_ref)
                 + g_hi * ffn(GATE_CHUNKS, wg_hi_ref, wu_hi_ref, wd_hi_ref))
            obufs[s][...] = _layer_norm(ALPHA * x + y, ln_g_ref[...], ln_b_ref[...])

            @pl.when(nv == bm)
            def _():
                for r in range(bm):
                    scatter_copy(i, s, r).start()

            @pl.when(nv < bm)
            def _():
                def body(r, c):
                    scatter_copy(i, s, r).start()
                    return c
                lax.fori_loop(0, nv, body, 0)

        @pl.when(i > 0)
        def _():
            wait_scatter(1 - s, nv_prev)

        @pl.when(i == nb - 1)
        def _():
            wait_scatter(s, nv)

            @pl.when(nv > 0)
            def _():
                wait_gather(1 - s)

    for s in range(2):
        pl.when(i % 2 == s)(functools.partial(step, s))


def _moe_layer(xext, meta_t, counts, w_gate, w_up, w_down, ln_g, ln_b):
    t = xext.shape[0]
    d, f, bm = D_MODEL, D_EXPERT, EXPERT_BLOCK
    n_blocks = t // bm + N_CLASSES
    n_rows = n_blocks * bm

    cls = meta_t[META_CLASS].astype(jnp.int32)
    rank = meta_t[META_RANK].astype(jnp.int32)
    cnt = counts[:N_CLASSES, 0].astype(jnp.int32)
    cls_blocks = (cnt + bm - 1) // bm
    blk_end = jnp.cumsum(cls_blocks)
    blk_start = blk_end - cls_blocks
    total_blocks = blk_end[-1]
    row_start = jnp.sum(jnp.where(cls[None, :] == jnp.arange(N_CLASSES, dtype=jnp.int32)[:, None],
                                  blk_start[:, None] * bm, 0), axis=0)
    dest = (row_start + rank).astype(jnp.int32)
    blk = jnp.arange(n_blocks, dtype=jnp.int32)
    b = jnp.minimum(blk, total_blocks - 1)[None, :]
    in_cls = (b >= blk_start[:, None]) & (b < blk_end[:, None])
    pick = lambda table: jnp.sum(jnp.where(in_cls, table[:, None], 0), axis=0).astype(jnp.int32)
    nvalid = jnp.clip(pick(cnt) - (blk - pick(blk_start)) * bm, 0, bm)
    nvalid = jnp.where(blk < total_blocks, nvalid, 0).astype(jnp.int32)
    e_lo = pick(jnp.asarray(CLASS_E_LO))
    e_hi = pick(jnp.asarray(CLASS_E_HI))

    w_gate, w_up, w_down = (w_gate.reshape(N_EXPERTS, d, f), w_up.reshape(N_EXPERTS, d, f),
                            w_down.reshape(N_EXPERTS, f, d))
    w_lo = lambda shape: pl.BlockSpec((1,) + shape, lambda i, tok, elo, ehi, nv: (elo[i], 0, 0))
    w_hi = lambda shape: pl.BlockSpec((1,) + shape, lambda i, tok, elo, ehi, nv: (ehi[i], 0, 0))
    vec = pl.BlockSpec((1, d), lambda i, tok, elo, ehi, nv: (0, 0))
    grid_spec = pltpu.PrefetchScalarGridSpec(
        num_scalar_prefetch=4,
        grid=(n_blocks,),
        in_specs=[pl.BlockSpec(memory_space=pl.ANY),
                  w_lo((d, f)), w_lo((d, f)), w_lo((f, d)),
                  w_hi((d, f)), w_hi((d, f)), w_hi((f, d)),
                  vec, vec],
        out_specs=pl.BlockSpec(memory_space=pl.ANY),
        scratch_shapes=[pltpu.VMEM((bm + SUBLANES, ROW_W), jnp.float32), pltpu.VMEM((bm + SUBLANES, ROW_W), jnp.float32),
                        pltpu.VMEM((bm, d), jnp.float32), pltpu.VMEM((bm, d), jnp.float32),
                        pltpu.SMEM((n_rows + bm,), jnp.int32),
                        pltpu.SemaphoreType.DMA((2,)),
                        pltpu.SemaphoreType.DMA((2,))],
    )
    return pl.pallas_call(
        _expert_kernel,
        out_shape=jax.ShapeDtypeStruct((t, d), jnp.float32),
        grid_spec=grid_spec,
        compiler_params=pltpu.CompilerParams(dimension_semantics=("arbitrary",),
                                             vmem_limit_bytes=VMEM_LIMIT_BYTES),
        name="expert_pair_ffn",
    )(dest, e_lo, e_hi, nvalid, xext, w_gate, w_up, w_down, w_gate, w_up, w_down, ln_g, ln_b)


def kernel(x, conv_w_in, conv_w, conv_w_out, ml_w_in, ml_b_gate, ml_norm_g, ml_w_out, ln_mix_g, ln_mix_b,
           ln_ffn_g, ln_ffn_b, router_w, router_b, exp_w_gate, exp_w_up, exp_w_down):
    bsz, seq, d = x.shape
    assert d == D_MODEL and seq % MIX_TILE == 0 and (bsz * seq) % EXPERT_BLOCK == 0
    bf16 = jnp.bfloat16
    vec = lambda a: a.astype(jnp.float32)[None, :]
    (rw_hi, rw_lo), rb = _prep_router(router_w, router_b)
    x2d = x.reshape(bsz * seq, d).astype(jnp.float32)
    expert_weights = (exp_w_gate.astype(jnp.float32), exp_w_up.astype(jnp.float32),
                      exp_w_down.astype(jnp.float32))
    for i in range(DEPTH):
        j = i // 2
        if i % 2 == 0:
            xext, meta_t, counts, *w16 = _conv_mixer_layer(
                x2d, seq, conv_w_in[j].astype(bf16), _pad_rows(conv_w[j].astype(jnp.float32)),
                conv_w_out[j].astype(bf16), vec(ln_mix_g[i]), vec(ln_mix_b[i]), rw_hi, rw_lo, rb,
                i, expert_weights)
        else:
            xext, meta_t, counts, *w16 = _mlstm_mixer_layer(
                x2d, seq, ml_w_in[j], ml_b_gate[j], ml_norm_g[j].astype(jnp.float32), ml_w_out[j],
                vec(ln_mix_g[i]), vec(ln_mix_b[i]), rw_hi, rw_lo, rb, i, expert_weights)
        x2d = _moe_layer(xext, meta_t, counts, *w16, vec(ln_ffn_g[i]), vec(ln_ffn_b[i]))
    return x2d.reshape(bsz, seq, d).astype(x.dtype)
```

```python
import functools
import itertools

import numpy as np
import jax
import jax.numpy as jnp
from jax import lax
from jax.experimental import pallas as pl
from jax.experimental.pallas import tpu as pltpu

D_MODEL = 1024
DEPTH = 2
CONV_WIDTH = 3
ML_HEADS = 8
ML_QK_DIM = D_MODEL // (2 * ML_HEADS)
ML_V_DIM = D_MODEL // ML_HEADS
ML_QK_W = ML_HEADS * ML_QK_DIM
ML_V_W = ML_HEADS * ML_V_DIM
N_EXPERTS = 16
N_GROUPS = 4
EXPERTS_PER_GROUP = N_EXPERTS // N_GROUPS
D_EXPERT = 3 * D_MODEL // 2
ALPHA = (2 * DEPTH) ** 0.25
LN_EPS = 1e-5
HEAD_NORM_EPS = 1e-6

LANES = 128
SUBLANES = 8
VMEM_LIMIT_BYTES = 56 * 1024 * 1024

PAIRS = tuple(itertools.combinations(range(EXPERTS_PER_GROUP), 2))
N_PAIRS = len(PAIRS)
N_CLASSES = N_GROUPS * N_PAIRS
CLASS_ROWS = -(-N_CLASSES // 16) * 16
CLASS_E_LO = np.array([g * EXPERTS_PER_GROUP + a for g in range(N_GROUPS) for a, _ in PAIRS], np.int32)
CLASS_E_HI = np.array([g * EXPERTS_PER_GROUP + b for g in range(N_GROUPS) for _, b in PAIRS], np.int32)

META_G_LO, META_G_HI, META_CLASS, META_RANK = 0, 1, 2, 3
ROW_W = D_MODEL + LANES

MIX_TILE = 512
MIX_SUBTILES = 2
EXPERT_BLOCK = 256
GATE_CHUNKS = 6
ML_CHUNK = 128


def _layer_norm(z, g, b):
    mu = jnp.mean(z, axis=-1, keepdims=True)
    zc = z - mu
    var = jnp.mean(zc * zc, axis=-1, keepdims=True)
    return zc * lax.rsqrt(var + LN_EPS) * g + b


def _split_bf16(a):
    hi = a.astype(jnp.bfloat16)
    lo = (a - hi.astype(jnp.float32)).astype(jnp.bfloat16)
    return hi, lo


def _route_and_rank(x1, rw_hi, rw_lo, rb_col, count_ref):
    ts = x1.shape[0]
    f32, bf16 = jnp.float32, jnp.bfloat16
    n_g, n_m = N_GROUPS, EXPERTS_PER_GROUP

    x_hi, x_lo = _split_bf16(x1)
    hi_terms = jnp.dot(x_hi, jnp.concatenate([rw_hi, rw_lo], axis=1), preferred_element_type=f32)
    logits = (hi_terms[:, :LANES] + hi_terms[:, LANES:]
              + jnp.dot(x_lo, rw_hi, preferred_element_type=f32))
    lt = logits.T[:N_EXPERTS, :]
    ex = jnp.exp(lt - jnp.max(lt, axis=0, keepdims=True))
    probs = ex / jnp.sum(ex, axis=0, keepdims=True)
    sel = probs + rb_col
    member = lambda a, j: a[j * n_g:(j + 1) * n_g, :]

    in_top2 = []
    for j in range(n_m):
        beaten_by = jnp.zeros((n_g, ts), jnp.int32)
        for k in range(n_m):
            if k != j:
                wins = member(sel, k) > member(sel, j)
                if k < j:
                    wins = wins | (member(sel, k) == member(sel, j))
                beaten_by = beaten_by + wins.astype(jnp.int32)
        in_top2.append(beaten_by < 2)
    score = sum(jnp.where(in_top2[j], member(sel, j), 0.0) for j in range(n_m))
    group = lax.broadcasted_iota(jnp.int32, (n_g, ts), 0)
    best = jnp.max(score, axis=0, keepdims=True)
    g_best = jnp.min(jnp.where(score == best, group, n_g), axis=0, keepdims=True)
    in_best = group == g_best
    picked = [jnp.max(jnp.where(in_top2[j] & in_best, 1, 0), axis=0, keepdims=True) for j in range(n_m)]
    p_pick = [jnp.sum(jnp.where(in_top2[j] & in_best, member(probs, j), 0.0), axis=0, keepdims=True)
              for j in range(n_m)]
    a = functools.reduce(jnp.minimum, [jnp.where(picked[j] > 0, j, n_m) for j in range(n_m)])
    b = functools.reduce(jnp.maximum, [jnp.where(picked[j] > 0, j, -1) for j in range(n_m)])
    p_lo = sum(jnp.where(a == j, p_pick[j], 0.0) for j in range(n_m))
    p_hi = sum(jnp.where(b == j, p_pick[j], 0.0) for j in range(n_m))
    g_lo = p_lo / (p_lo + p_hi)
    g_hi = p_hi / (p_lo + p_hi)
    pair = ((a * (2 * n_m - 1 - a)) >> 1) + (b - a - 1)
    cls = g_best * N_PAIRS + pair

    n_rows = count_ref.shape[0]
    onehot = jnp.where(lax.broadcasted_iota(jnp.int32, (n_rows, ts), 0) == cls, 1.0, 0.0)
    earlier = jnp.where(lax.broadcasted_iota(jnp.int32, (ts, ts), 0)
                        < lax.broadcasted_iota(jnp.int32, (ts, ts), 1), 1.0, 0.0).astype(bf16)
    before = jnp.dot(onehot.astype(bf16), earlier, preferred_element_type=f32)
    running = count_ref[...]
    running_ts = jnp.concatenate([running] * (ts // LANES), axis=1)
    rank = jnp.sum(onehot * (before + running_ts), axis=0, keepdims=True)
    count_ref[...] = running + jnp.dot(onehot.astype(bf16), jnp.ones((ts, LANES), bf16),
                                       preferred_element_type=f32)
    return jnp.concatenate([g_lo, g_hi, cls.astype(f32), rank, jnp.zeros((SUBLANES - 4, ts), f32)], axis=0)


def _cast_slabs(w32_refs, w16_refs):
    for w32_ref, w16_ref in zip(w32_refs, w16_refs):
        w16_ref[...] = w32_ref[0].astype(w16_ref.dtype)


def _cast_slab_specs(layer, n_steps, weights):
    args, in_specs, out_shapes, out_specs = [], [], [], []
    for w in weights:
        n_layers, n_exp, rows, cols = w.shape
        slab, rem = divmod(n_exp * rows, n_steps)
        assert rem == 0 and slab % (2 * SUBLANES) == 0
        args.append(w.reshape(n_layers, n_exp * rows, cols))
        in_specs.append(pl.BlockSpec((1, slab, cols), lambda i: (layer, i, 0)))
        out_shapes.append(jax.ShapeDtypeStruct((n_exp * rows, cols), jnp.bfloat16))
        out_specs.append(pl.BlockSpec((slab, cols), lambda i: (i, 0)))
    return args, in_specs, out_shapes, out_specs


def _conv_mixer_kernel(x_ref, w_in_ref, w_conv_ref, w_out_ref, ln_g_ref, ln_b_ref,
                       rw_hi_ref, rw_lo_ref, rb_ref, wg32_ref, wu32_ref, wd32_ref,
                       out_ref, meta_t_ref, count_ref, wg16_ref, wu16_ref, wd16_ref, carry_ref, *, tiles_per_seq):
    i = pl.program_id(0)

    @pl.when(i == 0)
    def _():
        count_ref[...] = jnp.zeros_like(count_ref)

    @pl.when(i % tiles_per_seq == 0)
    def _():
        carry_ref[...] = jnp.zeros_like(carry_ref)

    f32 = jnp.float32
    d = D_MODEL
    ts = x_ref.shape[0]
    sub = ts // MIX_SUBTILES
    xs = [x_ref[k * sub:(k + 1) * sub, :] for k in range(MIX_SUBTILES)]
    projs = [jnp.dot(x.astype(jnp.bfloat16), w_in_ref[...], preferred_element_type=f32) for x in xs]
    row = lax.broadcasted_iota(jnp.int32, (sub, d), 0)
    wc = w_conv_ref[...]
    prev = carry_ref[...]
    gated = []
    for proj in projs:
        gate_b, gate_c, h = proj[:, :d], proj[:, d:2 * d], proj[:, 2 * d:]
        u = gate_c * h
        u1 = jnp.where(row == 0, prev[SUBLANES - 1:SUBLANES, :], pltpu.roll(u, 1, axis=0))
        u2 = jnp.where(row == 0, prev[SUBLANES - 2:SUBLANES - 1, :],
                       jnp.where(row == 1, prev[SUBLANES - 1:SUBLANES, :], pltpu.roll(u, 2, axis=0)))
        prev = u[sub - SUBLANES:, :]
        conv = wc[0:1, :] * u2 + wc[1:2, :] * u1 + wc[2:3, :] * u
        gated.append((gate_b * conv).astype(jnp.bfloat16))
    carry_ref[...] = prev
    mixes = [jnp.dot(v, w_out_ref[...], preferred_element_type=f32) for v in gated]
    x1 = jnp.concatenate([_layer_norm(ALPHA * x + mix, ln_g_ref[...], ln_b_ref[...])
                          for x, mix in zip(xs, mixes)], axis=0)
    meta = _route_and_rank(x1, rw_hi_ref[...], rw_lo_ref[...], rb_ref[...], count_ref)
    out_ref[:, :d] = x1
    out_ref[:, d:] = jnp.concatenate([meta, jnp.zeros((LANES - SUBLANES, meta.shape[1]), meta.dtype)], axis=0).T
    meta_t_ref[...] = meta
    _cast_slabs((wg32_ref, wu32_ref, wd32_ref), (wg16_ref, wu16_ref, wd16_ref))


def _const_spec(shape):
    return pl.BlockSpec(shape, lambda i: (0,) * len(shape))


def _pad_rows(a, rows=SUBLANES):
    return jnp.pad(a, ((0, rows - a.shape[0]), (0, 0)))


def _prep_router(router_w, router_b):
    order = np.array([g * EXPERTS_PER_GROUP + j for j in range(EXPERTS_PER_GROUP) for g in range(N_GROUPS)])
    rw = jnp.pad(router_w.astype(jnp.float32)[:, order], ((0, 0), (0, LANES - N_EXPERTS)))
    rb = router_b.astype(jnp.float32)[order][:, None]
    return _split_bf16(rw), rb


def _conv_mixer_layer(x2d, seq, w_in, w_conv, w_out, ln_g, ln_b, rw_hi, rw_lo, rb, layer, expert_weights):
    t, d = x2d.shape
    ts = MIX_TILE
    grid = (t // ts,)
    cast_args, cast_in, cast_shapes, cast_out = _cast_slab_specs(layer, grid[0], expert_weights)
    return pl.pallas_call(
        functools.partial(_conv_mixer_kernel, tiles_per_seq=seq // ts),
        out_shape=(jax.ShapeDtypeStruct((t, ROW_W), jnp.float32),
                   jax.ShapeDtypeStruct((SUBLANES, t), jnp.float32),
                   jax.ShapeDtypeStruct((CLASS_ROWS, LANES), jnp.float32), *cast_shapes),
        grid=grid,
        in_specs=[pl.BlockSpec((ts, d), lambda i: (i, 0)),
                  _const_spec((d, 3 * d)), _const_spec((SUBLANES, d)), _const_spec((d, d)),
                  _const_spec((1, d)), _const_spec((1, d)),
                  _const_spec((d, LANES)), _const_spec((d, LANES)), _const_spec((N_EXPERTS, 1)), *cast_in],
        out_specs=(pl.BlockSpec((ts, ROW_W), lambda i: (i, 0)),
                   pl.BlockSpec((SUBLANES, ts), lambda i: (0, i)),
                   _const_spec((CLASS_ROWS, LANES)), *cast_out),
        scratch_shapes=[pltpu.VMEM((SUBLANES, d), jnp.float32)],
        compiler_params=pltpu.CompilerParams(dimension_semantics=("arbitrary",),
                                             vmem_limit_bytes=VMEM_LIMIT_BYTES),
        name="conv_mixer_route",
    )(x2d, w_in, w_conv, w_out, ln_g, ln_b, rw_hi, rw_lo, rb, *cast_args)


def _log_sigmoid(z):
    return jnp.minimum(z, 0.0) - jnp.log1p(jnp.exp(-jnp.abs(z)))


def _split3_f32(a):
    f32, bf16 = jnp.float32, jnp.bfloat16
    hi = a.astype(bf16).astype(f32)
    r1 = a - hi
    mid = r1.astype(bf16).astype(f32)
    lo = (r1 - mid).astype(bf16).astype(f32)
    return hi, mid, lo


_COL_U, _COL_ONE, _COL_W, _COL_EMT, _COL_END = 0, 24, 48, 64, 80


def _mlstm_mixer_kernel(x_ref, w_qvo_ref, wkt_ref, wgt_hi_ref, bg_row_ref, norm_g_ref, w_out_ref,
                        ln_g_ref, ln_b_ref, rw_hi_ref, rw_lo_ref, rb_ref, wg32_ref, wu32_ref, wd32_ref,
                        out_ref, meta_t_ref, count_ref, wg16_ref, wu16_ref, wd16_ref,
                        proj_ref, kt_ref, grow_ref, colmat_ref, bexp_ref, h_ref, c_ref, m_ref, *, tiles_per_seq):
    i = pl.program_id(0)
    f32, bf16 = jnp.float32, jnp.bfloat16
    nh, dk, dv, d = ML_HEADS, ML_QK_DIM, ML_V_DIM, D_MODEL
    L = ML_CHUNK
    ts = x_ref.shape[0]
    n_chunks = ts // L
    o_base = ML_QK_W + ML_V_W

    @pl.when(i == 0)
    def _():
        count_ref[...] = jnp.zeros_like(count_ref)

    @pl.when(i % tiles_per_seq == 0)
    def _():
        c_ref[...] = jnp.zeros_like(c_ref)
        m_ref[...] = jnp.zeros_like(m_ref)

    x = x_ref[...]
    x_hi, x_lo = _split_bf16(x)
    nt = (((1,), (1,)), ((), ()))
    kg_t = lax.dot_general(wkt_ref[...], x_hi, nt, preferred_element_type=f32)
    k_t, hi_terms = kg_t[:ML_QK_W, :], kg_t[ML_QK_W:, :]
    g_row = (hi_terms[:2 * nh, :] + hi_terms[2 * nh:, :]
             + lax.dot_general(wgt_hi_ref[...], x_lo, nt, preferred_element_type=f32)) + bg_row_ref[...]
    r_i = lax.broadcasted_iota(jnp.int32, (ts, ts), 0)
    c_i = lax.broadcasted_iota(jnp.int32, (ts, ts), 1)
    tri_row = jnp.where(((r_i // L) == (c_i // L)) & (r_i <= c_i), 1.0, 0.0).astype(bf16)
    lf_hi, lf_lo = _split_bf16(_log_sigmoid(g_row[nh:, :]))
    cum = jnp.dot(jnp.concatenate([lf_hi, lf_lo], axis=0), tri_row, preferred_element_type=f32)
    i_row, b_row = g_row[:nh, :], cum[:nh, :] + cum[nh:, :]
    proj_ref[...] = jnp.dot(x_hi, w_qvo_ref[...], preferred_element_type=f32)

    g = i_row - b_row
    lane_in_chunk = lax.broadcasted_iota(jnp.int32, (nh, ts), 1) & (L - 1)
    cm = g
    shift = 1
    while shift < L:
        cm = jnp.maximum(cm, jnp.where(lane_in_chunk >= shift, pltpu.roll(cm, shift, axis=1), -jnp.inf))
        shift *= 2
    m_prev = m_ref[:, 0:1]
    u_parts, w_parts, gr_parts, gs_parts = [], [], [], []
    for c in range(n_chunks):
        sl = slice(c * L, (c + 1) * L)
        u_c = jnp.maximum(m_prev, cm[:, sl])
        b_last = b_row[:, (c + 1) * L - 1:(c + 1) * L]
        m_new = b_last + u_c[:, L - 1:L]
        u_parts.append(u_c)
        w_parts.append(jnp.exp(m_prev - u_c))
        gr_parts.append(jnp.exp(b_last - b_row[:, sl] + i_row[:, sl] - m_new))
        gs_parts.append(jnp.broadcast_to(jnp.exp(b_last + m_prev - m_new), (nh, L)))
        m_prev = m_new
    m_ref[...] = jnp.broadcast_to(m_prev, m_ref.shape)
    u = jnp.concatenate(u_parts, axis=1)
    w_inter = jnp.concatenate(w_parts, axis=1)
    emt = jnp.exp(-(b_row + u))
    w_hi = w_inter.astype(bf16).astype(f32)
    e_hi = emt.astype(bf16).astype(f32)
    col_rows = jnp.concatenate(
        [*_split3_f32(u), jnp.ones((_COL_W - _COL_ONE, ts), f32), w_hi, w_inter - w_hi, e_hi, emt - e_hi,
         jnp.zeros((LANES - _COL_END, ts), f32)], axis=0)
    colmat_ref[...] = col_rows.T.astype(bf16)
    b_exp = jnp.concatenate([jnp.full((_COL_ONE - _COL_U, ts), -1.0, f32), *_split3_f32(g),
                             jnp.zeros((LANES - _COL_W, ts), f32)], axis=0).astype(bf16)
    for c in range(n_chunks):
        sl = slice(c * L, (c + 1) * L)
        bexp_ref[c] = b_exp[:, sl]
        grow_ref[c] = jnp.concatenate([gr_parts[c], gs_parts[c]], axis=0)
        kt_ref[c] = k_t[:, sl]

    causal = lax.broadcasted_iota(jnp.int32, (L, L), 0) >= lax.broadcasted_iota(jnp.int32, (L, L), 1)
    ones_col = jnp.where(lax.broadcasted_iota(jnp.int32, (L, dv), 1) == 0, 1.0, 0.0).astype(bf16)
    lane_ll = lax.broadcasted_iota(jnp.int32, (L, LANES), 1)
    sel_r = lax.broadcasted_iota(jnp.int32, (LANES, 2 * LANES), 0)
    sel_c = lax.broadcasted_iota(jnp.int32, (LANES, 2 * LANES), 1)
    is_w_row = (sel_r >= _COL_W) & (sel_r < _COL_EMT)
    is_e_row = (sel_r >= _COL_EMT) & (sel_r < _COL_END)
    b_sel = jnp.where((is_w_row & (sel_c < LANES)) | (is_e_row & (sel_c >= LANES)), 1.0, 0.0).astype(bf16)
    r2_r = lax.broadcasted_iota(jnp.int32, (2 * dv, 2 * dv), 0)
    r2_c = lax.broadcasted_iota(jnp.int32, (2 * dv, 2 * dv), 1)
    rhs2 = jnp.where((r2_r < dv) & (r2_c < dv), 1.0 / dv,
                     jnp.where((r2_r == dv) & (r2_c >= dv), 1.0, 0.0)).astype(bf16)

    def chunk_body(c, carry):
        r0 = pl.multiple_of(c * L, L)
        gr = grow_ref[c]
        colmat = colmat_ref[pl.ds(r0, L), :]
        rhs1 = jnp.concatenate([bexp_ref[c], b_sel], axis=1)
        heads = range(nh)
        c_pairs = [c_ref[p] for p in range(nh // 2)]
        kt_pairs = [kt_ref[c, p * LANES:(p + 1) * LANES, :] for p in range(nh // 2)]

        ew, q_m, s_mat, v_ext = [], [], [], []
        for h in heads:
            p, hh = divmod(h, 2)
            lhs = jnp.where((lane_ll & (nh - 1)) == h, colmat, 0)
            ew.append(jnp.dot(lhs, rhs1, preferred_element_type=f32))
            q2 = proj_ref[pl.ds(r0, L), p * LANES:(p + 1) * LANES] * (dk ** -0.5)
            q_m.append(jnp.where((lane_ll >= hh * dk) & (lane_ll < (hh + 1) * dk), q2, 0.0))
            s_mat.append(jnp.dot(q_m[h].astype(bf16), kt_pairs[p].astype(bf16), preferred_element_type=f32))
            v = proj_ref[pl.ds(r0, L), ML_QK_W + h * dv:ML_QK_W + (h + 1) * dv].astype(bf16)
            v_ext.append(jnp.concatenate([v, ones_col], axis=1))
        tot = []
        for h in heads:
            p, hh = divmod(h, 2)
            a = jnp.exp(jnp.where(causal, ew[h][:, :L], -jnp.inf)) * s_mat[h]
            qw = q_m[h] * ew[h][:, L:L + LANES]
            lhs = jnp.concatenate([a.astype(bf16), qw.astype(bf16)], axis=1)
            rhs = jnp.concatenate([v_ext[h], c_pairs[p].astype(bf16)], axis=0)
            tot.append(jnp.dot(lhs, rhs, preferred_element_type=f32))
        for h in heads:
            num = tot[h][:, :dv]
            lhs2 = jnp.concatenate([(num * num).astype(bf16), tot[h][:, dv:].astype(bf16)], axis=1)
            r2 = jnp.dot(lhs2, rhs2, preferred_element_type=f32)
            inv = 1.0 / jnp.maximum(jnp.abs(r2[:, dv:]), ew[h][:, L + LANES:])
            hn = (num * inv) * lax.rsqrt(r2[:, :dv] * inv * inv + HEAD_NORM_EPS)
            o_pre = proj_ref[pl.ds(r0, L), o_base + h * dv:o_base + (h + 1) * dv]
            h_ref[pl.ds(r0, L), h * dv:(h + 1) * dv] = (
                jax.nn.sigmoid(o_pre) * hn * norm_g_ref[:, h * dv:(h + 1) * dv])
        for h in heads:
            p, hh = divmod(h, 2)
            rows = slice(hh * dk, (hh + 1) * dk)
            kg = (kt_pairs[p][rows, :] * gr[h:h + 1, :]).astype(bf16)
            gs = gr[nh + h:nh + h + 1, :]
            c_ref[p, rows, :] = (jnp.concatenate([gs, gs], axis=1) * c_pairs[p][rows, :]
                                 + jnp.dot(kg, v_ext[h], preferred_element_type=f32))
        return carry

    lax.fori_loop(0, n_chunks, chunk_body, 0, unroll=True)

    sub = ts // MIX_SUBTILES
    mixes = [jnp.dot(h_ref[k * sub:(k + 1) * sub, :].astype(bf16), w_out_ref[...], preferred_element_type=f32)
             for k in range(MIX_SUBTILES)]
    x1 = jnp.concatenate([_layer_norm(ALPHA * x[k * sub:(k + 1) * sub, :] + mixes[k], ln_g_ref[...], ln_b_ref[...])
                          for k in range(MIX_SUBTILES)], axis=0)
    meta = _route_and_rank(x1, rw_hi_ref[...], rw_lo_ref[...], rb_ref[...], count_ref)
    out_ref[:, :d] = x1
    out_ref[:, d:] = jnp.concatenate([meta, jnp.zeros((LANES - SUBLANES, meta.shape[1]), meta.dtype)], axis=0).T
    meta_t_ref[...] = meta
    _cast_slabs((wg32_ref, wu32_ref, wd32_ref), (wg16_ref, wu16_ref, wd16_ref))


def _mlstm_mixer_layer(x2d, seq, w_in, b_gate, norm_g, w_out, ln_g, ln_b, rw_hi, rw_lo, rb, layer,
                       expert_weights):
    t, d = x2d.shape
    ts = MIX_TILE
    nh = ML_HEADS
    cast_args, cast_in, cast_shapes, cast_out = _cast_slab_specs(layer, t // ts, expert_weights)
    assert ML_CHUNK == LANES and ts % ML_CHUNK == 0 and 2 * ML_QK_DIM == LANES and ML_V_DIM == LANES
    n_qkvo = 2 * ML_QK_W + 2 * ML_V_W
    n_qvo = ML_QK_W + 2 * ML_V_W
    w_qvo = jnp.concatenate([w_in[:, :ML_QK_W], w_in[:, 2 * ML_QK_W:n_qkvo]], axis=1).astype(jnp.bfloat16)
    w_g = w_in[:, n_qkvo:].astype(jnp.float32)
    wgt_hi, wgt_lo = _split_bf16(w_g.T)
    wkt = jnp.concatenate([w_in[:, ML_QK_W:2 * ML_QK_W].T.astype(jnp.bfloat16), wgt_hi, wgt_lo], axis=0)
    bg_row = b_gate.astype(jnp.float32)[:, None]
    return pl.pallas_call(
        functools.partial(_mlstm_mixer_kernel, tiles_per_seq=seq // ts),
        out_shape=(jax.ShapeDtypeStruct((t, ROW_W), jnp.float32),
                   jax.ShapeDtypeStruct((SUBLANES, t), jnp.float32),
                   jax.ShapeDtypeStruct((CLASS_ROWS, LANES), jnp.float32), *cast_shapes),
        grid=(t // ts,),
        in_specs=[pl.BlockSpec((ts, d), lambda i: (i, 0)),
                  _const_spec((d, n_qvo)), _const_spec((ML_QK_W + 4 * nh, d)),
                  _const_spec((2 * nh, d)),
                  _const_spec((2 * nh, 1)),
                  _const_spec((1, d)), _const_spec((d, d)),
                  _const_spec((1, d)), _const_spec((1, d)),
                  _const_spec((d, LANES)), _const_spec((d, LANES)), _const_spec((N_EXPERTS, 1)), *cast_in],
        out_specs=(pl.BlockSpec((ts, ROW_W), lambda i: (i, 0)),
                   pl.BlockSpec((SUBLANES, ts), lambda i: (0, i)),
                   _const_spec((CLASS_ROWS, LANES)), *cast_out),
        scratch_shapes=[pltpu.VMEM((ts, n_qvo), jnp.float32),
                        pltpu.VMEM((ts // ML_CHUNK, ML_QK_W, ML_CHUNK), jnp.float32),
                        pltpu.VMEM((ts // ML_CHUNK, 2 * nh, ML_CHUNK), jnp.float32),
                        pltpu.VMEM((ts, LANES), jnp.bfloat16),
                        pltpu.VMEM((ts // ML_CHUNK, LANES, ML_CHUNK), jnp.bfloat16),
                        pltpu.VMEM((ts, d), jnp.float32),
                        pltpu.VMEM((nh // 2, 2 * ML_QK_DIM, 2 * ML_V_DIM), jnp.float32),
                        pltpu.VMEM((nh, ML_CHUNK), jnp.float32)],
        compiler_params=pltpu.CompilerParams(dimension_semantics=("arbitrary",),
                                             vmem_limit_bytes=VMEM_LIMIT_BYTES),
        name="mlstm_mixer_route",
    )(x2d, w_qvo, wkt, wgt_hi, bg_row, norm_g[None, :], w_out.astype(jnp.bfloat16),
      ln_g, ln_b, rw_hi, rw_lo, rb, *cast_args)


def _expert_kernel(dest_ref, elo_ref, ehi_ref, nvalid_ref,
                   x_hbm, wg_lo_ref, wu_lo_ref, wd_lo_ref, wg_hi_ref, wu_hi_ref, wd_hi_ref,
                   ln_g_ref, ln_b_ref, out_hbm, xbuf0, xbuf1, obuf0, obuf1, tok_ref, gather_sem, scatter_sem):
    i = pl.program_id(0)
    nb = pl.num_programs(0)
    bm = obuf0.shape[0]
    d = D_MODEL

    @pl.when(i == 0)
    def _():
        def pad_block(b, carry):
            def pad_row(r, c):
                tok_ref[b * bm + r] = 0
                return c
            first = jnp.where(b < nb, nvalid_ref[jnp.minimum(b, nb - 1)], 0)
            is_gathered = (b == 0) | (nvalid_ref[jnp.maximum(b - 1, 0)] > 0)
            return lax.fori_loop(jnp.where(is_gathered, first, bm), bm, pad_row, carry)

        def place(t, c):
            tok_ref[dest_ref[t]] = t
            return c

        lax.fori_loop(0, nb + 1, pad_block, 0)
        lax.fori_loop(0, x_hbm.shape[0], place, 0, unroll=8)
    xbufs, obufs = (xbuf0, xbuf1), (obuf0, obuf1)
    nv = nvalid_ref[i]
    nv_prev = nvalid_ref[jnp.maximum(i - 1, 0)]

    def start_gather(j, s, rows=range(bm)):
        for r in rows:
            tok = tok_ref[j * bm + r]
            pltpu.make_async_copy(x_hbm.at[pl.ds(tok, 1)], xbufs[s].at[pl.ds(r, 1)], gather_sem.at[s]).start()

    def wait_gather(s):
        pltpu.make_async_copy(xbufs[s].at[pl.ds(0, bm)], xbufs[s].at[pl.ds(0, bm)], gather_sem.at[s]).wait()

    def scatter_copy(j, s, r):
        tok = tok_ref[j * bm + r]
        return pltpu.make_async_copy(obufs[s].at[pl.ds(r, 1)], out_hbm.at[pl.ds(tok, 1)], scatter_sem.at[s])

    def wait_scatter(s, n):
        @pl.when(n == bm)
        def _():
            pltpu.make_async_copy(obufs[s], obufs[s], scatter_sem.at[s]).wait()

        @pl.when(n < bm)
        def _():
            def body(r, c):
                pltpu.make_async_copy(obufs[s].at[pl.ds(0, 1)], obufs[s].at[pl.ds(0, 1)], scatter_sem.at[s]).wait()
                return c
            lax.fori_loop(0, n, body, 0)

    @pl.when(i == 0)
    def _():
        for buf in xbufs:
            buf[bm:, :] = jnp.zeros((buf.shape[0] - bm, buf.shape[1]), buf.dtype)
        start_gather(0, 0)

    def step(s):
        @pl.when((i == 0) | (nv_prev > 0))
        def _():
            wait_gather(s)

        @pl.when(nv > 0)
        def _():
            xb = xbufs[s][0:bm, :]
            x = xb[:, :d]
            g_lo = xb[:, d + META_G_LO:d + META_G_LO + 1]
            g_hi = xb[:, d + META_G_HI:d + META_G_HI + 1]
            x16 = x.astype(jnp.bfloat16)
            n_portions = 2 * GATE_CHUNKS

            def gate_proj(w_ref, k0):
                g = jnp.dot(x16, w_ref[0], preferred_element_type=jnp.float32)
                width = g.shape[1] // GATE_CHUNKS
                chunks = []
                for c in range(GATE_CHUNKS):
                    k = k0 + c
                    start_gather(i + 1, 1 - s, range(k * bm // n_portions, (k + 1) * bm // n_portions))
                    zero_row = xbufs[1 - s][bm:bm + 1, 0:LANES]
                    chunks.append(g[:, c * width:(c + 1) * width]
                                  + jnp.concatenate([zero_row] * (width // LANES), axis=1))
                return jnp.concatenate(chunks, axis=1)

            def ffn(k0, wg_ref, wu_ref, wd_ref):
                g = gate_proj(wg_ref, k0)
                u = jnp.dot(x16, wu_ref[0], preferred_element_type=jnp.float32)
                h = (g * jax.nn.sigmoid(g)) * u
                return jnp.dot(h.astype(jnp.bfloat16), wd_ref[0], preferred_element_type=jnp.float32)

            y = (g_lo * ffn(0, wg_lo_ref, wu_lo_ref, wd_lo_ref)
                 + g_hi * ffn(GATE_CHUNKS, wg_hi_ref, wu_hi_ref, wd_hi_ref))
            obufs[s][...] = _layer_norm(ALPHA * x + y, ln_g_ref[...], ln_b_ref[...])

            @pl.when(nv == bm)
            def _():
                for r in range(bm):
                    scatter_copy(i, s, r).start(priority=r % 2)

            @pl.when(nv < bm)
            def _():
                def body(r, c):
                    scatter_copy(i, s, r).start()
                    return c
                lax.fori_loop(0, nv, body, 0)

        @pl.when(i > 0)
        def _():
            wait_scatter(1 - s, nv_prev)

        @pl.when(i == nb - 1)
        def _():
            wait_scatter(s, nv)

            @pl.when(nv > 0)
            def _():
                wait_gather(1 - s)

    for s in range(2):
        pl.when(i % 2 == s)(functools.partial(step, s))


def _moe_layer(xext, meta_t, counts, w_gate, w_up, w_down, ln_g, ln_b):
    t = xext.shape[0]
    d, f, bm = D_MODEL, D_EXPERT, EXPERT_BLOCK
    n_blocks = t // bm + N_CLASSES
    n_rows = n_blocks * bm

    cls = meta_t[META_CLASS].astype(jnp.int32)
    rank = meta_t[META_RANK].astype(jnp.int32)
    cnt = counts[:N_CLASSES, 0].astype(jnp.int32)
    cls_blocks = (cnt + bm - 1) // bm
    blk_end = jnp.cumsum(cls_blocks)
    blk_start = blk_end - cls_blocks
    total_blocks = blk_end[-1]
    row_start = jnp.sum(jnp.where(cls[None, :] == jnp.arange(N_CLASSES, dtype=jnp.int32)[:, None],
                                  blk_start[:, None] * bm, 0), axis=0)
    dest = (row_start + rank).astype(jnp.int32)
    blk = jnp.arange(n_blocks, dtype=jnp.int32)
    b = jnp.minimum(blk, total_blocks - 1)[None, :]
    in_cls = (b >= blk_start[:, None]) & (b < blk_end[:, None])
    pick = lambda table: jnp.sum(jnp.where(in_cls, table[:, None], 0), axis=0).astype(jnp.int32)
    nvalid = jnp.clip(pick(cnt) - (blk - pick(blk_start)) * bm, 0, bm)
    nvalid = jnp.where(blk < total_blocks, nvalid, 0).astype(jnp.int32)
    e_lo = pick(jnp.asarray(CLASS_E_LO))
    e_hi = pick(jnp.asarray(CLASS_E_HI))

    w_gate, w_up, w_down = (w_gate.reshape(N_EXPERTS, d, f), w_up.reshape(N_EXPERTS, d, f),
                            w_down.reshape(N_EXPERTS, f, d))
    w_lo = lambda shape: pl.BlockSpec((1,) + shape, lambda i, tok, elo, ehi, nv: (elo[i], 0, 0))
    w_hi = lambda shape: pl.BlockSpec((1,) + shape, lambda i, tok, elo, ehi, nv: (ehi[i], 0, 0))
    vec = pl.BlockSpec((1, d), lambda i, tok, elo, ehi, nv: (0, 0))
    grid_spec = pltpu.PrefetchScalarGridSpec(
        num_scalar_prefetch=4,
        grid=(n_blocks,),
        in_specs=[pl.BlockSpec(memory_space=pl.ANY),
                  w_lo((d, f)), w_lo((d, f)), w_lo((f, d)),
                  w_hi((d, f)), w_hi((d, f)), w_hi((f, d)),
                  vec, vec],
        out_specs=pl.BlockSpec(memory_space=pl.ANY),
        scratch_shapes=[pltpu.VMEM((bm + SUBLANES, ROW_W), jnp.float32), pltpu.VMEM((bm + SUBLANES, ROW_W), jnp.float32),
                        pltpu.VMEM((bm, d), jnp.float32), pltpu.VMEM((bm, d), jnp.float32),
                        pltpu.SMEM((n_rows + bm,), jnp.int32),
                        pltpu.SemaphoreType.DMA((2,)),
                        pltpu.SemaphoreType.DMA((2,))],
    )
    return pl.pallas_call(
        _expert_kernel,
        out_shape=jax.ShapeDtypeStruct((t, d), jnp.float32),
        grid_spec=grid_spec,
        compiler_params=pltpu.CompilerParams(dimension_semantics=("arbitrary",),
                                             vmem_limit_bytes=VMEM_LIMIT_BYTES),
        name="expert_pair_ffn",
    )(dest, e_lo, e_hi, nvalid, xext, w_gate, w_up, w_down, w_gate, w_up, w_down, ln_g, ln_b)


def kernel(x, conv_w_in, conv_w, conv_w_out, ml_w_in, ml_b_gate, ml_norm_g, ml_w_out, ln_mix_g, ln_mix_b,
           ln_ffn_g, ln_ffn_b, router_w, router_b, exp_w_gate, exp_w_up, exp_w_down):
    bsz, seq, d = x.shape
    assert d == D_MODEL and seq % MIX_TILE == 0 and (bsz * seq) % EXPERT_BLOCK == 0
    bf16 = jnp.bfloat16
    vec = lambda a: a.astype(jnp.float32)[None, :]
    (rw_hi, rw_lo), rb = _prep_router(router_w, router_b)
    x2d = x.reshape(bsz * seq, d).astype(jnp.float32)
    expert_weights = (exp_w_gate.astype(jnp.float32), exp_w_up.astype(jnp.float32),
                      exp_w_down.astype(jnp.float32))
    for i in range(DEPTH):
        j = i // 2
        if i % 2 == 0:
            xext, meta_t, counts, *w16 = _conv_mixer_layer(
                x2d, seq, conv_w_in[j].astype(bf16), _pad_rows(conv_w[j].astype(jnp.float32)),
                conv_w_out[j].astype(bf16), vec(ln_mix_g[i]), vec(ln_mix_b[i]), rw_hi, rw_lo, rb,
                i, expert_weights)
        else:
            xext, meta_t, counts, *w16 = _mlstm_mixer_layer(
                x2d, seq, ml_w_in[j], ml_b_gate[j], ml_norm_g[j].astype(jnp.float32), ml_w_out[j],
                vec(ln_mix_g[i]), vec(ln_mix_b[i]), rw_hi, rw_lo, rb, i, expert_weights)
        x2d = _moe_layer(xext, meta_t, counts, *w16, vec(ln_ffn_g[i]), vec(ln_ffn_b[i]))
    return x2d.reshape(bsz, seq, d).astype(x.dtype)
```

```python
import functools
import itertools

import numpy as np
import jax
import jax.numpy as jnp
from jax import lax
from jax.experimental import pallas as pl
from jax.experimental.pallas import tpu as pltpu

D_MODEL = 1024
DEPTH = 2
CONV_WIDTH = 3
ML_HEADS = 8
ML_QK_DIM = D_MODEL // (2 * ML_HEADS)
ML_V_DIM = D_MODEL // ML_HEADS
ML_QK_W = ML_HEADS * ML_QK_DIM
ML_V_W = ML_HEADS * ML_V_DIM
N_EXPERTS = 16
N_GROUPS = 4
EXPERTS_PER_GROUP = N_EXPERTS // N_GROUPS
D_EXPERT = 3 * D_MODEL // 2
ALPHA = (2 * DEPTH) ** 0.25
LN_EPS = 1e-5
HEAD_NORM_EPS = 1e-6

LANES = 128
SUBLANES = 8
V7X_VMEM_BYTES = 64 * 1024 * 1024
VMEM_LIMIT_BYTES = V7X_VMEM_BYTES * 7 // 8

PAIRS = tuple(itertools.combinations(range(EXPERTS_PER_GROUP), 2))
N_PAIRS = len(PAIRS)
N_CLASSES = N_GROUPS * N_PAIRS
CLASS_ROWS = -(-N_CLASSES // 16) * 16
CLASS_E_LO = np.array([g * EXPERTS_PER_GROUP + a for g in range(N_GROUPS) for a, _ in PAIRS], np.int32)
CLASS_E_HI = np.array([g * EXPERTS_PER_GROUP + b for g in range(N_GROUPS) for _, b in PAIRS], np.int32)

META_G_LO, META_G_HI, META_CLASS, META_RANK = 0, 1, 2, 3
ROW_W = D_MODEL + LANES

MIX_TILE = 512
MIX_SUBTILES = 2
EXPERT_BLOCK = 256
GATE_CHUNKS = 6
ML_CHUNK = 128


def _layer_norm(z, g, b):
    mu = jnp.mean(z, axis=-1, keepdims=True)
    zc = z - mu
    var = jnp.mean(zc * zc, axis=-1, keepdims=True)
    return zc * lax.rsqrt(var + LN_EPS) * g + b


def _split_bf16(a):
    hi = a.astype(jnp.bfloat16)
    lo = (a - hi.astype(jnp.float32)).astype(jnp.bfloat16)
    return hi, lo


def _route_and_rank(x1, rw_hi, rw_lo, rb_col, count_ref):
    ts = x1.shape[0]
    f32, bf16 = jnp.float32, jnp.bfloat16
    n_g, n_m = N_GROUPS, EXPERTS_PER_GROUP

    x_hi, x_lo = _split_bf16(x1)
    hi_terms = jnp.dot(x_hi, jnp.concatenate([rw_hi, rw_lo], axis=1), preferred_element_type=f32)
    logits = (hi_terms[:, :LANES] + hi_terms[:, LANES:]
              + jnp.dot(x_lo, rw_hi, preferred_element_type=f32))
    lt = logits.T[:N_EXPERTS, :]
    ex = jnp.exp(lt - jnp.max(lt, axis=0, keepdims=True))
    probs = ex / jnp.sum(ex, axis=0, keepdims=True)
    sel = probs + rb_col
    member = lambda a, j: a[j * n_g:(j + 1) * n_g, :]

    in_top2 = []
    for j in range(n_m):
        beaten_by = jnp.zeros((n_g, ts), jnp.int32)
        for k in range(n_m):
            if k != j:
                wins = member(sel, k) > member(sel, j)
                if k < j:
                    wins = wins | (member(sel, k) == member(sel, j))
                beaten_by = beaten_by + wins.astype(jnp.int32)
        in_top2.append(beaten_by < 2)
    score = sum(jnp.where(in_top2[j], member(sel, j), 0.0) for j in range(n_m))
    group = lax.broadcasted_iota(jnp.int32, (n_g, ts), 0)
    best = jnp.max(score, axis=0, keepdims=True)
    g_best = jnp.min(jnp.where(score == best, group, n_g), axis=0, keepdims=True)
    in_best = group == g_best
    picked = [jnp.max(jnp.where(in_top2[j] & in_best, 1, 0), axis=0, keepdims=True) for j in range(n_m)]
    p_pick = [jnp.sum(jnp.where(in_top2[j] & in_best, member(probs, j), 0.0), axis=0, keepdims=True)
              for j in range(n_m)]
    a = functools.reduce(jnp.minimum, [jnp.where(picked[j] > 0, j, n_m) for j in range(n_m)])
    b = functools.reduce(jnp.maximum, [jnp.where(picked[j] > 0, j, -1) for j in range(n_m)])
    p_lo = sum(jnp.where(a == j, p_pick[j], 0.0) for j in range(n_m))
    p_hi = sum(jnp.where(b == j, p_pick[j], 0.0) for j in range(n_m))
    g_lo = p_lo / (p_lo + p_hi)
    g_hi = p_hi / (p_lo + p_hi)
    pair = ((a * (2 * n_m - 1 - a)) >> 1) + (b - a - 1)
    cls = g_best * N_PAIRS + pair

    n_rows = count_ref.shape[0]
    onehot = jnp.where(lax.broadcasted_iota(jnp.int32, (n_rows, ts), 0) == cls, 1.0, 0.0)
    earlier = jnp.where(lax.broadcasted_iota(jnp.int32, (ts, ts), 0)
                        < lax.broadcasted_iota(jnp.int32, (ts, ts), 1), 1.0, 0.0).astype(bf16)
    before = jnp.dot(onehot.astype(bf16), earlier, preferred_element_type=f32)
    running = count_ref[...]
    running_ts = jnp.concatenate([running] * (ts // LANES), axis=1)
    rank = jnp.sum(onehot * (before + running_ts), axis=0, keepdims=True)
    count_ref[...] = running + jnp.dot(onehot.astype(bf16), jnp.ones((ts, LANES), bf16),
                                       preferred_element_type=f32)
    return jnp.concatenate([g_lo, g_hi, cls.astype(f32), rank, jnp.zeros((SUBLANES - 4, ts), f32)], axis=0)


def _cast_slabs(w32_refs, w16_refs):
    for w32_ref, w16_ref in zip(w32_refs, w16_refs):
        w16_ref[...] = w32_ref[0].astype(w16_ref.dtype)


def _cast_slab_specs(layer, n_steps, weights):
    args, in_specs, out_shapes, out_specs = [], [], [], []
    for w in weights:
        n_layers, n_exp, rows, cols = w.shape
        slab, rem = divmod(n_exp * rows, n_steps)
        assert rem == 0 and slab % (2 * SUBLANES) == 0
        args.append(w.reshape(n_layers, n_exp * rows, cols))
        in_specs.append(pl.BlockSpec((1, slab, cols), lambda i: (layer, i, 0)))
        out_shapes.append(jax.ShapeDtypeStruct((n_exp * rows, cols), jnp.bfloat16))
        out_specs.append(pl.BlockSpec((slab, cols), lambda i: (i, 0)))
    return args, in_specs, out_shapes, out_specs


def _conv_mixer_kernel(x_ref, w_in_ref, w_conv_ref, w_out_ref, ln_g_ref, ln_b_ref,
                       rw_hi_ref, rw_lo_ref, rb_ref, wg32_ref, wu32_ref, wd32_ref,
                       out_ref, meta_t_ref, count_ref, wg16_ref, wu16_ref, wd16_ref, carry_ref, *, tiles_per_seq):
    i = pl.program_id(0)

    @pl.when(i == 0)
    def _():
        count_ref[...] = jnp.zeros_like(count_ref)

    @pl.when(i % tiles_per_seq == 0)
    def _():
        carry_ref[...] = jnp.zeros_like(carry_ref)

    f32 = jnp.float32
    d = D_MODEL
    ts = x_ref.shape[0]
    sub = ts // MIX_SUBTILES
    xs = [x_ref[k * sub:(k + 1) * sub, :] for k in range(MIX_SUBTILES)]
    projs = [jnp.dot(x.astype(jnp.bfloat16), w_in_ref[...], preferred_element_type=f32) for x in xs]
    row = lax.broadcasted_iota(jnp.int32, (sub, d), 0)
    wc = w_conv_ref[...]
    prev = carry_ref[...]
    gated = []
    for proj in projs:
        gate_b, gate_c, h = proj[:, :d], proj[:, d:2 * d], proj[:, 2 * d:]
        u = gate_c * h
        u1 = jnp.where(row == 0, prev[SUBLANES - 1:SUBLANES, :], pltpu.roll(u, 1, axis=0))
        u2 = jnp.where(row == 0, prev[SUBLANES - 2:SUBLANES - 1, :],
                       jnp.where(row == 1, prev[SUBLANES - 1:SUBLANES, :], pltpu.roll(u, 2, axis=0)))
        prev = u[sub - SUBLANES:, :]
        conv = wc[0:1, :] * u2 + wc[1:2, :] * u1 + wc[2:3, :] * u
        gated.append((gate_b * conv).astype(jnp.bfloat16))
    carry_ref[...] = prev
    mixes = [jnp.dot(v, w_out_ref[...], preferred_element_type=f32) for v in gated]
    x1 = jnp.concatenate([_layer_norm(ALPHA * x + mix, ln_g_ref[...], ln_b_ref[...])
                          for x, mix in zip(xs, mixes)], axis=0)
    meta = _route_and_rank(x1, rw_hi_ref[...], rw_lo_ref[...], rb_ref[...], count_ref)
    out_ref[:, :d] = x1
    out_ref[:, d:] = jnp.concatenate([meta, jnp.zeros((LANES - SUBLANES, meta.shape[1]), meta.dtype)], axis=0).T
    meta_t_ref[...] = meta
    _cast_slabs((wg32_ref, wu32_ref, wd32_ref), (wg16_ref, wu16_ref, wd16_ref))


def _const_spec(shape):
    return pl.BlockSpec(shape, lambda i: (0,) * len(shape))


def _pad_rows(a, rows=SUBLANES):
    return jnp.pad(a, ((0, rows - a.shape[0]), (0, 0)))


def _prep_router(router_w, router_b):
    order = np.array([g * EXPERTS_PER_GROUP + j for j in range(EXPERTS_PER_GROUP) for g in range(N_GROUPS)])
    rw = jnp.pad(router_w.astype(jnp.float32)[:, order], ((0, 0), (0, LANES - N_EXPERTS)))
    rb = router_b.astype(jnp.float32)[order][:, None]
    return _split_bf16(rw), rb


def _conv_mixer_layer(x2d, seq, w_in, w_conv, w_out, ln_g, ln_b, rw_hi, rw_lo, rb, layer, expert_weights):
    t, d = x2d.shape
    ts = MIX_TILE
    grid = (t // ts,)
    cast_args, cast_in, cast_shapes, cast_out = _cast_slab_specs(layer, grid[0], expert_weights)
    return pl.pallas_call(
        functools.partial(_conv_mixer_kernel, tiles_per_seq=seq // ts),
        out_shape=(jax.ShapeDtypeStruct((t, ROW_W), jnp.float32),
                   jax.ShapeDtypeStruct((SUBLANES, t), jnp.float32),
                   jax.ShapeDtypeStruct((CLASS_ROWS, LANES), jnp.float32), *cast_shapes),
        grid=grid,
        in_specs=[pl.BlockSpec((ts, d), lambda i: (i, 0)),
                  _const_spec((d, 3 * d)), _const_spec((SUBLANES, d)), _const_spec((d, d)),
                  _const_spec((1, d)), _const_spec((1, d)),
                  _const_spec((d, LANES)), _const_spec((d, LANES)), _const_spec((N_EXPERTS, 1)), *cast_in],
        out_specs=(pl.BlockSpec((ts, ROW_W), lambda i: (i, 0)),
                   pl.BlockSpec((SUBLANES, ts), lambda i: (0, i)),
                   _const_spec((CLASS_ROWS, LANES)), *cast_out),
        scratch_shapes=[pltpu.VMEM((SUBLANES, d), jnp.float32)],
        compiler_params=pltpu.CompilerParams(dimension_semantics=("arbitrary",),
                                             vmem_limit_bytes=VMEM_LIMIT_BYTES),
        name="conv_mixer_route",
    )(x2d, w_in, w_conv, w_out, ln_g, ln_b, rw_hi, rw_lo, rb, *cast_args)


def _log_sigmoid(z):
    return jnp.minimum(z, 0.0) - jnp.log1p(jnp.exp(-jnp.abs(z)))


def _split3_f32(a):
    f32, bf16 = jnp.float32, jnp.bfloat16
    hi = a.astype(bf16).astype(f32)
    r1 = a - hi
    mid = r1.astype(bf16).astype(f32)
    lo = (r1 - mid).astype(bf16).astype(f32)
    return hi, mid, lo


_COL_U, _COL_ONE, _COL_W, _COL_EMT, _COL_END = 0, 24, 48, 64, 80


def _mlstm_mixer_kernel(x_ref, w_qvo_ref, wkt_ref, wgt_hi_ref, bg_row_ref, norm_g_ref, w_out_ref,
                        ln_g_ref, ln_b_ref, rw_hi_ref, rw_lo_ref, rb_ref, wg32_ref, wu32_ref, wd32_ref,
                        out_ref, meta_t_ref, count_ref, wg16_ref, wu16_ref, wd16_ref,
                        proj_ref, kt_ref, grow_ref, colmat_ref, bexp_ref, h_ref, c_ref, m_ref, *, tiles_per_seq):
    i = pl.program_id(0)
    f32, bf16 = jnp.float32, jnp.bfloat16
    nh, dk, dv, d = ML_HEADS, ML_QK_DIM, ML_V_DIM, D_MODEL
    L = ML_CHUNK
    ts = x_ref.shape[0]
    n_chunks = ts // L
    o_base = ML_QK_W + ML_V_W

    @pl.when(i == 0)
    def _():
        count_ref[...] = jnp.zeros_like(count_ref)

    @pl.when(i % tiles_per_seq == 0)
    def _():
        c_ref[...] = jnp.zeros_like(c_ref)
        m_ref[...] = jnp.zeros_like(m_ref)

    x = x_ref[...]
    x_hi, x_lo = _split_bf16(x)
    nt = (((1,), (1,)), ((), ()))
    kg_t = lax.dot_general(wkt_ref[...], x_hi, nt, preferred_element_type=f32)
    k_t, hi_terms = kg_t[:ML_QK_W, :], kg_t[ML_QK_W:, :]
    g_row = (hi_terms[:2 * nh, :] + hi_terms[2 * nh:, :]
             + lax.dot_general(wgt_hi_ref[...], x_lo, nt, preferred_element_type=f32)) + bg_row_ref[...]
    r_i = lax.broadcasted_iota(jnp.int32, (ts, ts), 0)
    c_i = lax.broadcasted_iota(jnp.int32, (ts, ts), 1)
    tri_row = jnp.where(((r_i // L) == (c_i // L)) & (r_i <= c_i), 1.0, 0.0).astype(bf16)
    lf_hi, lf_lo = _split_bf16(_log_sigmoid(g_row[nh:, :]))
    cum = jnp.dot(jnp.concatenate([lf_hi, lf_lo], axis=0), tri_row, preferred_element_type=f32)
    i_row, b_row = g_row[:nh, :], cum[:nh, :] + cum[nh:, :]
    proj_ref[...] = jnp.dot(x_hi, w_qvo_ref[...], preferred_element_type=f32)

    g = i_row - b_row
    lane_in_chunk = lax.broadcasted_iota(jnp.int32, (nh, ts), 1) & (L - 1)
    cm = g
    shift = 1
    while shift < L:
        cm = jnp.maximum(cm, jnp.where(lane_in_chunk >= shift, pltpu.roll(cm, shift, axis=1), -jnp.inf))
        shift *= 2
    m_prev = m_ref[:, 0:1]
    u_parts, w_parts, gr_parts, gs_parts = [], [], [], []
    for c in range(n_chunks):
        sl = slice(c * L, (c + 1) * L)
        u_c = jnp.maximum(m_prev, cm[:, sl])
        b_last = b_row[:, (c + 1) * L - 1:(c + 1) * L]
        m_new = b_last + u_c[:, L - 1:L]
        u_parts.append(u_c)
        w_parts.append(jnp.exp(m_prev - u_c))
        gr_parts.append(jnp.exp(b_last - b_row[:, sl] + i_row[:, sl] - m_new))
        gs_parts.append(jnp.broadcast_to(jnp.exp(b_last + m_prev - m_new), (nh, L)))
        m_prev = m_new
    m_ref[...] = jnp.broadcast_to(m_prev, m_ref.shape)
    u = jnp.concatenate(u_parts, axis=1)
    w_inter = jnp.concatenate(w_parts, axis=1)
    emt = jnp.exp(-(b_row + u))
    w_hi = w_inter.astype(bf16).astype(f32)
    e_hi = emt.astype(bf16).astype(f32)
    col_rows = jnp.concatenate(
        [*_split3_f32(u), jnp.ones((_COL_W - _COL_ONE, ts), f32), w_hi, w_inter - w_hi, e_hi, emt - e_hi,
         jnp.zeros((LANES - _COL_END, ts), f32)], axis=0)
    colmat_ref[...] = col_rows.T.astype(bf16)
    b_exp = jnp.concatenate([jnp.full((_COL_ONE - _COL_U, ts), -1.0, f32), *_split3_f32(g),
                             jnp.zeros((LANES - _COL_W, ts), f32)], axis=0).astype(bf16)
    for c in range(n_chunks):
        sl = slice(c * L, (c + 1) * L)
        bexp_ref[c] = b_exp[:, sl]
        grow_ref[c] = jnp.concatenate([gr_parts[c], gs_parts[c]], axis=0)
        kt_ref[c] = k_t[:, sl]

    causal = lax.broadcasted_iota(jnp.int32, (L, L), 0) >= lax.broadcasted_iota(jnp.int32, (L, L), 1)
    ones_col = jnp.where(lax.broadcasted_iota(jnp.int32, (L, dv), 1) == 0, 1.0, 0.0).astype(bf16)
    lane_ll = lax.broadcasted_iota(jnp.int32, (L, LANES), 1)
    sel_r = lax.broadcasted_iota(jnp.int32, (LANES, 2 * LANES), 0)
    sel_c = lax.broadcasted_iota(jnp.int32, (LANES, 2 * LANES), 1)
    is_w_row = (sel_r >= _COL_W) & (sel_r < _COL_EMT)
    is_e_row = (sel_r >= _COL_EMT) & (sel_r < _COL_END)
    b_sel = jnp.where((is_w_row & (sel_c < LANES)) | (is_e_row & (sel_c >= LANES)), 1.0, 0.0).astype(bf16)
    r2_r = lax.broadcasted_iota(jnp.int32, (2 * dv, 2 * dv), 0)
    r2_c = lax.broadcasted_iota(jnp.int32, (2 * dv, 2 * dv), 1)
    rhs2 = jnp.where((r2_r < dv) & (r2_c < dv), 1.0 / dv,
                     jnp.where((r2_r == dv) & (r2_c >= dv), 1.0, 0.0)).astype(bf16)

    def chunk_body(c, carry):
        r0 = pl.multiple_of(c * L, L)
        gr = grow_ref[c]
        colmat = colmat_ref[pl.ds(r0, L), :]
        rhs1 = jnp.concatenate([bexp_ref[c], b_sel], axis=1)
        heads = range(nh)
        c_pairs = [c_ref[p] for p in range(nh // 2)]
        kt_pairs = [kt_ref[c, p * LANES:(p + 1) * LANES, :] for p in range(nh // 2)]

        ew, q_m, s_mat, v_ext = [], [], [], []
        for h in heads:
            p, hh = divmod(h, 2)
            lhs = jnp.where((lane_ll & (nh - 1)) == h, colmat, 0)
            ew.append(jnp.dot(lhs, rhs1, preferred_element_type=f32))
            q2 = proj_ref[pl.ds(r0, L), p * LANES:(p + 1) * LANES] * (dk ** -0.5)
            q_m.append(jnp.where((lane_ll >= hh * dk) & (lane_ll < (hh + 1) * dk), q2, 0.0))
            s_mat.append(jnp.dot(q_m[h].astype(bf16), kt_pairs[p].astype(bf16), preferred_element_type=f32))
            v = proj_ref[pl.ds(r0, L), ML_QK_W + h * dv:ML_QK_W + (h + 1) * dv].astype(bf16)
            v_ext.append(jnp.concatenate([v, ones_col], axis=1))
        tot = []
        for h in heads:
            p, hh = divmod(h, 2)
            a = jnp.exp(jnp.where(causal, ew[h][:, :L], -jnp.inf)) * s_mat[h]
            qw = q_m[h] * ew[h][:, L:L + LANES]
            lhs = jnp.concatenate([a.astype(bf16), qw.astype(bf16)], axis=1)
            rhs = jnp.concatenate([v_ext[h], c_pairs[p].astype(bf16)], axis=0)
            tot.append(jnp.dot(lhs, rhs, preferred_element_type=f32))
        for h in heads:
            num = tot[h][:, :dv]
            lhs2 = jnp.concatenate([(num * num).astype(bf16), tot[h][:, dv:].astype(bf16)], axis=1)
            r2 = jnp.dot(lhs2, rhs2, preferred_element_type=f32)
            inv = 1.0 / jnp.maximum(jnp.abs(r2[:, dv:]), ew[h][:, L + LANES:])
            hn = (num * inv) * lax.rsqrt(r2[:, :dv] * inv * inv + HEAD_NORM_EPS)
            o_pre = proj_ref[pl.ds(r0, L), o_base + h * dv:o_base + (h + 1) * dv]
            h_ref[pl.ds(r0, L), h * dv:(h + 1) * dv] = (
                jax.nn.sigmoid(o_pre) * hn * norm_g_ref[:, h * dv:(h + 1) * dv])
        for h in heads:
            p, hh = divmod(h, 2)
            rows = slice(hh * dk, (hh + 1) * dk)
            kg = (kt_pairs[p][rows, :] * gr[h:h + 1, :]).astype(bf16)
            gs = gr[nh + h:nh + h + 1, :]
            c_ref[p, rows, :] = (jnp.concatenate([gs, gs], axis=1) * c_pairs[p][rows, :]
                                 + jnp.dot(kg, v_ext[h], preferred_element_type=f32))
        return carry

    lax.fori_loop(0, n_chunks, chunk_body, 0, unroll=True)

    sub = ts // MIX_SUBTILES
    mixes = [jnp.dot(h_ref[k * sub:(k + 1) * sub, :].astype(bf16), w_out_ref[...], preferred_element_type=f32)
             for k in range(MIX_SUBTILES)]
    x1 = jnp.concatenate([_layer_norm(ALPHA * x[k * sub:(k + 1) * sub, :] + mixes[k], ln_g_ref[...], ln_b_ref[...])
                          for k in range(MIX_SUBTILES)], axis=0)
    meta = _route_and_rank(x1, rw_hi_ref[...], rw_lo_ref[...], rb_ref[...], count_ref)
    out_ref[:, :d] = x1
    out_ref[:, d:] = jnp.concatenate([meta, jnp.zeros((LANES - SUBLANES, meta.shape[1]), meta.dtype)], axis=0).T
    meta_t_ref[...] = meta
    _cast_slabs((wg32_ref, wu32_ref, wd32_ref), (wg16_ref, wu16_ref, wd16_ref))


def _mlstm_mixer_layer(x2d, seq, w_in, b_gate, norm_g, w_out, ln_g, ln_b, rw_hi, rw_lo, rb, layer,
                       expert_weights):
    t, d = x2d.shape
    ts = MIX_TILE
    nh = ML_HEADS
    cast_args, cast_in, cast_shapes, cast_out = _cast_slab_specs(layer, t // ts, expert_weights)
    assert ML_CHUNK == LANES and ts % ML_CHUNK == 0 and 2 * ML_QK_DIM == LANES and ML_V_DIM == LANES
    n_qkvo = 2 * ML_QK_W + 2 * ML_V_W
    n_qvo = ML_QK_W + 2 * ML_V_W
    w_qvo = jnp.concatenate([w_in[:, :ML_QK_W], w_in[:, 2 * ML_QK_W:n_qkvo]], axis=1).astype(jnp.bfloat16)
    w_g = w_in[:, n_qkvo:].astype(jnp.float32)
    wgt_hi, wgt_lo = _split_bf16(w_g.T)
    wkt = jnp.concatenate([w_in[:, ML_QK_W:2 * ML_QK_W].T.astype(jnp.bfloat16), wgt_hi, wgt_lo], axis=0)
    bg_row = b_gate.astype(jnp.float32)[:, None]
    return pl.pallas_call(
        functools.partial(_mlstm_mixer_kernel, tiles_per_seq=seq // ts),
        out_shape=(jax.ShapeDtypeStruct((t, ROW_W), jnp.float32),
                   jax.ShapeDtypeStruct((SUBLANES, t), jnp.float32),
                   jax.ShapeDtypeStruct((CLASS_ROWS, LANES), jnp.float32), *cast_shapes),
        grid=(t // ts,),
        in_specs=[pl.BlockSpec((ts, d), lambda i: (i, 0)),
                  _const_spec((d, n_qvo)), _const_spec((ML_QK_W + 4 * nh, d)),
                  _const_spec((2 * nh, d)),
                  _const_spec((2 * nh, 1)),
                  _const_spec((1, d)), _const_spec((d, d)),
                  _const_spec((1, d)), _const_spec((1, d)),
                  _const_spec((d, LANES)), _const_spec((d, LANES)), _const_spec((N_EXPERTS, 1)), *cast_in],
        out_specs=(pl.BlockSpec((ts, ROW_W), lambda i: (i, 0)),
                   pl.BlockSpec((SUBLANES, ts), lambda i: (0, i)),
                   _const_spec((CLASS_ROWS, LANES)), *cast_out),
        scratch_shapes=[pltpu.VMEM((ts, n_qvo), jnp.float32),
                        pltpu.VMEM((ts // ML_CHUNK, ML_QK_W, ML_CHUNK), jnp.float32),
                        pltpu.VMEM((ts // ML_CHUNK, 2 * nh, ML_CHUNK), jnp.float32),
                        pltpu.VMEM((ts, LANES), jnp.bfloat16),
                        pltpu.VMEM((ts // ML_CHUNK, LANES, ML_CHUNK), jnp.bfloat16),
                        pltpu.VMEM((ts, d), jnp.float32),
                        pltpu.VMEM((nh // 2, 2 * ML_QK_DIM, 2 * ML_V_DIM), jnp.float32),
                        pltpu.VMEM((nh, ML_CHUNK), jnp.float32)],
        compiler_params=pltpu.CompilerParams(dimension_semantics=("arbitrary",),
                                             vmem_limit_bytes=VMEM_LIMIT_BYTES),
        name="mlstm_mixer_route",
    )(x2d, w_qvo, wkt, wgt_hi, bg_row, norm_g[None, :], w_out.astype(jnp.bfloat16),
      ln_g, ln_b, rw_hi, rw_lo, rb, *cast_args)


def _expert_kernel(dest_ref, elo_ref, ehi_ref, nvalid_ref,
                   x_hbm, wg_lo_ref, wu_lo_ref, wd_lo_ref, wg_hi_ref, wu_hi_ref, wd_hi_ref,
                   ln_g_ref, ln_b_ref, out_hbm, xbuf0, xbuf1, obuf0, obuf1, tok_ref, gather_sem, scatter_sem):
    i = pl.program_id(0)
    nb = pl.num_programs(0)
    bm = obuf0.shape[0]
    d = D_MODEL

    @pl.when(i == 0)
    def _():
        def pad_block(b, carry):
            def pad_row(r, c):
                tok_ref[b * bm + r] = 0
                return c
            first = jnp.where(b < nb, nvalid_ref[jnp.minimum(b, nb - 1)], 0)
            is_gathered = (b == 0) | (nvalid_ref[jnp.maximum(b - 1, 0)] > 0)
            return lax.fori_loop(jnp.where(is_gathered, first, bm), bm, pad_row, carry)

        def place(t, c):
            tok_ref[dest_ref[t]] = t
            return c

        lax.fori_loop(0, nb + 1, pad_block, 0)
        lax.fori_loop(0, x_hbm.shape[0], place, 0, unroll=16)
    xbufs, obufs = (xbuf0, xbuf1), (obuf0, obuf1)
    nv = nvalid_ref[i]
    nv_prev = nvalid_ref[jnp.maximum(i - 1, 0)]

    def start_gather(j, s, rows=range(bm)):
        for r in rows:
            tok = tok_ref[j * bm + r]
            pltpu.make_async_copy(x_hbm.at[pl.ds(tok, 1)], xbufs[s].at[pl.ds(r, 1)], gather_sem.at[s]).start()

    def wait_gather(s):
        pltpu.make_async_copy(xbufs[s].at[pl.ds(0, bm)], xbufs[s].at[pl.ds(0, bm)], gather_sem.at[s]).wait()

    def scatter_copy(j, s, r):
        tok = tok_ref[j * bm + r]
        return pltpu.make_async_copy(obufs[s].at[pl.ds(r, 1)], out_hbm.at[pl.ds(tok, 1)], scatter_sem.at[s])

    def wait_scatter(s, n):
        @pl.when(n == bm)
        def _():
            pltpu.make_async_copy(obufs[s], obufs[s], scatter_sem.at[s]).wait()

        @pl.when(n < bm)
        def _():
            def body(r, c):
                pltpu.make_async_copy(obufs[s].at[pl.ds(0, 1)], obufs[s].at[pl.ds(0, 1)], scatter_sem.at[s]).wait()
                return c
            lax.fori_loop(0, n, body, 0)

    @pl.when(i == 0)
    def _():
        for buf in xbufs:
            buf[bm:, :] = jnp.zeros((buf.shape[0] - bm, buf.shape[1]), buf.dtype)
        start_gather(0, 0)

    def step(s):
        @pl.when((i == 0) | (nv_prev > 0))
        def _():
            wait_gather(s)

        @pl.when(nv > 0)
        def _():
            xb = xbufs[s][0:bm, :]
            x = xb[:, :d]
            g_lo = xb[:, d + META_G_LO:d + META_G_LO + 1]
            g_hi = xb[:, d + META_G_HI:d + META_G_HI + 1]
            x16 = x.astype(jnp.bfloat16)
            n_portions = 2 * GATE_CHUNKS

            def gate_proj(w_ref, k0):
                g = jnp.dot(x16, w_ref[0], preferred_element_type=jnp.float32)
                width = g.shape[1] // GATE_CHUNKS
                chunks = []
                for c in range(GATE_CHUNKS):
                    k = k0 + c
                    start_gather(i + 1, 1 - s, range(k * bm // n_portions, (k + 1) * bm // n_portions))
                    zero_row = xbufs[1 - s][bm:bm + 1, 0:LANES]
                    chunks.append(g[:, c * width:(c + 1) * width]
                                  + jnp.concatenate([zero_row] * (width // LANES), axis=1))
                return jnp.concatenate(chunks, axis=1)

            def ffn(k0, wg_ref, wu_ref, wd_ref):
                g = gate_proj(wg_ref, k0)
                u = jnp.dot(x16, wu_ref[0], preferred_element_type=jnp.float32)
                h = (g * jax.nn.sigmoid(g)) * u
                return jnp.dot(h.astype(jnp.bfloat16), wd_ref[0], preferred_element_type=jnp.float32)

            y = (g_lo * ffn(0, wg_lo_ref, wu_lo_ref, wd_lo_ref)
                 + g_hi * ffn(GATE_CHUNKS, wg_hi_ref, wu_hi_ref, wd_hi_ref))
            obufs[s][...] = _layer_norm(ALPHA * x + y, ln_g_ref[...], ln_b_ref[...])

            @pl.when(nv == bm)
            def _():
                for r in range(bm):
                    scatter_copy(i, s, r).start()

            @pl.when(nv < bm)
            def _():
                def body(r, c):
                    scatter_copy(i, s, r).start()
                    return c
                lax.fori_loop(0, nv, body, 0)

        @pl.when(i > 0)
        def _():
            wait_scatter(1 - s, nv_prev)

        @pl.when(i == nb - 1)
        def _():
            wait_scatter(s, nv)

            @pl.when(nv > 0)
            def _():
                wait_gather(1 - s)

    for s in range(2):
        pl.when(i % 2 == s)(functools.partial(step, s))


def _moe_layer(xext, meta_t, counts, w_gate, w_up, w_down, ln_g, ln_b):
    t = xext.shape[0]
    d, f, bm = D_MODEL, D_EXPERT, EXPERT_BLOCK
    n_blocks = t // bm + N_CLASSES
    n_rows = n_blocks * bm

    cls = meta_t[META_CLASS].astype(jnp.int32)
    rank = meta_t[META_RANK].astype(jnp.int32)
    cnt = counts[:N_CLASSES, 0].astype(jnp.int32)
    cls_blocks = (cnt + bm - 1) // bm
    blk_end = jnp.cumsum(cls_blocks)
    blk_start = blk_end - cls_blocks
    total_blocks = blk_end[-1]
    row_start = jnp.sum(jnp.where(cls[None, :] == jnp.arange(N_CLASSES, dtype=jnp.int32)[:, None],
                                  blk_start[:, None] * bm, 0), axis=0)
    dest = (row_start + rank).astype(jnp.int32)
    blk = jnp.arange(n_blocks, dtype=jnp.int32)
    b = jnp.minimum(blk, total_blocks - 1)[None, :]
    in_cls = (b >= blk_start[:, None]) & (b < blk_end[:, None])
    pick = lambda table: jnp.sum(jnp.where(in_cls, table[:, None], 0), axis=0).astype(jnp.int32)
    nvalid = jnp.clip(pick(cnt) - (blk - pick(blk_start)) * bm, 0, bm)
    nvalid = jnp.where(blk < total_blocks, nvalid, 0).astype(jnp.int32)
    e_lo = pick(jnp.asarray(CLASS_E_LO))
    e_hi = pick(jnp.asarray(CLASS_E_HI))

    w_gate, w_up, w_down = (w_gate.reshape(N_EXPERTS, d, f), w_up.reshape(N_EXPERTS, d, f),
                            w_down.reshape(N_EXPERTS, f, d))
    w_lo = lambda shape: pl.BlockSpec((1,) + shape, lambda i, tok, elo, ehi, nv: (elo[i], 0, 0))
    w_hi = lambda shape: pl.BlockSpec((1,) + shape, lambda i, tok, elo, ehi, nv: (ehi[i], 0, 0))
    vec = pl.BlockSpec((1, d), lambda i, tok, elo, ehi, nv: (0, 0))
    grid_spec = pltpu.PrefetchScalarGridSpec(
        num_scalar_prefetch=4,
        grid=(n_blocks,),
        in_specs=[pl.BlockSpec(memory_space=pl.ANY),
                  w_lo((d, f)), w_lo((d, f)), w_lo((f, d)),
                  w_hi((d, f)), w_hi((d, f)), w_hi((f, d)),
                  vec, vec],
        out_specs=pl.BlockSpec(memory_space=pl.ANY),
        scratch_shapes=[pltpu.VMEM((bm + SUBLANES, ROW_W), jnp.float32), pltpu.VMEM((bm + SUBLANES, ROW_W), jnp.float32),
                        pltpu.VMEM((bm, d), jnp.float32), pltpu.VMEM((bm, d), jnp.float32),
                        pltpu.SMEM((n_rows + bm,), jnp.int32),
                        pltpu.SemaphoreType.DMA((2,)),
                        pltpu.SemaphoreType.DMA((2,))],
    )
    return pl.pallas_call(
        _expert_kernel,
        out_shape=jax.ShapeDtypeStruct((t, d), jnp.float32),
        grid_spec=grid_spec,
        compiler_params=pltpu.CompilerParams(dimension_semantics=("arbitrary",),
                                             vmem_limit_bytes=VMEM_LIMIT_BYTES),
        name="expert_pair_ffn",
    )(dest, e_lo, e_hi, nvalid, xext, w_gate, w_up, w_down, w_gate, w_up, w_down, ln_g, ln_b)


def kernel(x, conv_w_in, conv_w, conv_w_out, ml_w_in, ml_b_gate, ml_norm_g, ml_w_out, ln_mix_g, ln_mix_b,
           ln_ffn_g, ln_ffn_b, router_w, router_b, exp_w_gate, exp_w_up, exp_w_down):
    bsz, seq, d = x.shape
    assert d == D_MODEL and seq % MIX_TILE == 0 and (bsz * seq) % EXPERT_BLOCK == 0
    bf16 = jnp.bfloat16
    vec = lambda a: a.astype(jnp.float32)[None, :]
    (rw_hi, rw_lo), rb = _prep_router(router_w, router_b)
    x2d = x.reshape(bsz * seq, d).astype(jnp.float32)
    expert_weights = (exp_w_gate.astype(jnp.float32), exp_w_up.astype(jnp.float32),
                      exp_w_down.astype(jnp.float32))
    for i in range(DEPTH):
        j = i // 2
        if i % 2 == 0:
            xext, meta_t, counts, *w16 = _conv_mixer_layer(
                x2d, seq, conv_w_in[j].astype(bf16), _pad_rows(conv_w[j].astype(jnp.float32)),
                conv_w_out[j].astype(bf16), vec(ln_mix_g[i]), vec(ln_mix_b[i]), rw_hi, rw_lo, rb,
                i, expert_weights)
        else:
            xext, meta_t, counts, *w16 = _mlstm_mixer_layer(
                x2d, seq, ml_w_in[j], ml_b_gate[j], ml_norm_g[j].astype(jnp.float32), ml_w_out[j],
                vec(ln_mix_g[i]), vec(ln_mix_b[i]), rw_hi, rw_lo, rb, i, expert_weights)
        x2d = _moe_layer(xext, meta_t, counts, *w16, vec(ln_ffn_g[i]), vec(ln_ffn_b[i]))
    return x2d.reshape(bsz, seq, d).astype(x.dtype)
```

```python
import functools
import itertools

import numpy as np
import jax
import jax.numpy as jnp
from jax import lax
from jax.experimental import pallas as pl
from jax.experimental.pallas import tpu as pltpu

D_MODEL = 1024
DEPTH = 2
CONV_WIDTH = 3
ML_HEADS = 8
ML_QK_DIM = D_MODEL // (2 * ML_HEADS)
ML_V_DIM = D_MODEL // ML_HEADS
ML_QK_W = ML_HEADS * ML_QK_DIM
ML_V_W = ML_HEADS * ML_V_DIM
N_EXPERTS = 16
N_GROUPS = 4
EXPERTS_PER_GROUP = N_EXPERTS // N_GROUPS
D_EXPERT = 3 * D_MODEL // 2
ALPHA = (2 * DEPTH) ** 0.25
LN_EPS = 1e-5
HEAD_NORM_EPS = 1e-6

LANES = 128
SUBLANES = 8
V7X_VMEM_BYTES = 64 * 1024 * 1024
VMEM_LIMIT_BYTES = V7X_VMEM_BYTES * 7 // 8

PAIRS = tuple(itertools.combinations(range(EXPERTS_PER_GROUP), 2))
N_PAIRS = len(PAIRS)
N_CLASSES = N_GROUPS * N_PAIRS
CLASS_ROWS = -(-N_CLASSES // 16) * 16
CLASS_E_LO = np.array([g * EXPERTS_PER_GROUP + a for g in range(N_GROUPS) for a, _ in PAIRS], np.int32)
CLASS_E_HI = np.array([g * EXPERTS_PER_GROUP + b for g in range(N_GROUPS) for _, b in PAIRS], np.int32)

META_G_LO, META_G_HI, META_CLASS, META_RANK = 0, 1, 2, 3
ROW_W = D_MODEL + LANES

MIX_TILE = 512
MIX_SUBTILES = 2
EXPERT_BLOCK = 256
ROW_UNROLL = 8
GATE_CHUNKS = 6
ML_CHUNK = 128


def _layer_norm(z, g, b):
    mu = jnp.mean(z, axis=-1, keepdims=True)
    zc = z - mu
    var = jnp.mean(zc * zc, axis=-1, keepdims=True)
    return zc * lax.rsqrt(var + LN_EPS) * g + b


def _split_bf16(a):
    hi = a.astype(jnp.bfloat16)
    lo = (a - hi.astype(jnp.float32)).astype(jnp.bfloat16)
    return hi, lo


def _route_and_rank(x1, rw_hi, rw_lo, rb_col, count_ref):
    ts = x1.shape[0]
    f32, bf16 = jnp.float32, jnp.bfloat16
    n_g, n_m = N_GROUPS, EXPERTS_PER_GROUP

    x_hi, x_lo = _split_bf16(x1)
    hi_terms = jnp.dot(x_hi, jnp.concatenate([rw_hi, rw_lo], axis=1), preferred_element_type=f32)
    logits = (hi_terms[:, :LANES] + hi_terms[:, LANES:]
              + jnp.dot(x_lo, rw_hi, preferred_element_type=f32))
    lt = logits.T[:N_EXPERTS, :]
    ex = jnp.exp(lt - jnp.max(lt, axis=0, keepdims=True))
    probs = ex / jnp.sum(ex, axis=0, keepdims=True)
    sel = probs + rb_col
    member = lambda a, j: a[j * n_g:(j + 1) * n_g, :]

    in_top2 = []
    for j in range(n_m):
        beaten_by = jnp.zeros((n_g, ts), jnp.int32)
        for k in range(n_m):
            if k != j:
                wins = member(sel, k) > member(sel, j)
                if k < j:
                    wins = wins | (member(sel, k) == member(sel, j))
                beaten_by = beaten_by + wins.astype(jnp.int32)
        in_top2.append(beaten_by < 2)
    score = sum(jnp.where(in_top2[j], member(sel, j), 0.0) for j in range(n_m))
    group = lax.broadcasted_iota(jnp.int32, (n_g, ts), 0)
    best = jnp.max(score, axis=0, keepdims=True)
    g_best = jnp.min(jnp.where(score == best, group, n_g), axis=0, keepdims=True)
    in_best = group == g_best
    picked = [jnp.max(jnp.where(in_top2[j] & in_best, 1, 0), axis=0, keepdims=True) for j in range(n_m)]
    p_pick = [jnp.sum(jnp.where(in_top2[j] & in_best, member(probs, j), 0.0), axis=0, keepdims=True)
              for j in range(n_m)]
    a = functools.reduce(jnp.minimum, [jnp.where(picked[j] > 0, j, n_m) for j in range(n_m)])
    b = functools.reduce(jnp.maximum, [jnp.where(picked[j] > 0, j, -1) for j in range(n_m)])
    p_lo = sum(jnp.where(a == j, p_pick[j], 0.0) for j in range(n_m))
    p_hi = sum(jnp.where(b == j, p_pick[j], 0.0) for j in range(n_m))
    g_lo = p_lo / (p_lo + p_hi)
    g_hi = p_hi / (p_lo + p_hi)
    pair = ((a * (2 * n_m - 1 - a)) >> 1) + (b - a - 1)
    cls = g_best * N_PAIRS + pair

    n_rows = count_ref.shape[0]
    onehot = jnp.where(lax.broadcasted_iota(jnp.int32, (n_rows, ts), 0) == cls, 1.0, 0.0)
    earlier = jnp.where(lax.broadcasted_iota(jnp.int32, (ts, ts), 0)
                        < lax.broadcasted_iota(jnp.int32, (ts, ts), 1), 1.0, 0.0).astype(bf16)
    before = jnp.dot(onehot.astype(bf16), earlier, preferred_element_type=f32)
    running = count_ref[...]
    running_ts = jnp.concatenate([running] * (ts // LANES), axis=1)
    rank = jnp.sum(onehot * (before + running_ts), axis=0, keepdims=True)
    count_ref[...] = running + jnp.dot(onehot.astype(bf16), jnp.ones((ts, LANES), bf16),
                                       preferred_element_type=f32)
    return jnp.concatenate([g_lo, g_hi, cls.astype(f32), rank, jnp.zeros((SUBLANES - 4, ts), f32)], axis=0)


def _cast_slabs(w32_refs, w16_refs):
    for w32_ref, w16_ref in zip(w32_refs, w16_refs):
        w16_ref[...] = w32_ref[0].astype(w16_ref.dtype)


def _cast_slab_specs(layer, n_steps, weights):
    args, in_specs, out_shapes, out_specs = [], [], [], []
    for w in weights:
        n_layers, n_exp, rows, cols = w.shape
        slab, rem = divmod(n_exp * rows, n_steps)
        assert rem == 0 and slab % (2 * SUBLANES) == 0
        args.append(w.reshape(n_layers, n_exp * rows, cols))
        in_specs.append(pl.BlockSpec((1, slab, cols), lambda i: (layer, i, 0)))
        out_shapes.append(jax.ShapeDtypeStruct((n_exp * rows, cols), jnp.bfloat16))
        out_specs.append(pl.BlockSpec((slab, cols), lambda i: (i, 0)))
    return args, in_specs, out_shapes, out_specs


def _conv_mixer_kernel(x_ref, w_in_ref, w_conv_ref, w_out_ref, ln_g_ref, ln_b_ref,
                       rw_hi_ref, rw_lo_ref, rb_ref, wg32_ref, wu32_ref, wd32_ref,
                       out_ref, meta_t_ref, count_ref, wg16_ref, wu16_ref, wd16_ref, carry_ref, *, tiles_per_seq):
    i = pl.program_id(0)

    @pl.when(i == 0)
    def _():
        count_ref[...] = jnp.zeros_like(count_ref)

    @pl.when(i % tiles_per_seq == 0)
    def _():
        carry_ref[...] = jnp.zeros_like(carry_ref)

    f32 = jnp.float32
    d = D_MODEL
    ts = x_ref.shape[0]
    sub = ts // MIX_SUBTILES
    xs = [x_ref[k * sub:(k + 1) * sub, :] for k in range(MIX_SUBTILES)]
    projs = [jnp.dot(x.astype(jnp.bfloat16), w_in_ref[...], preferred_element_type=f32) for x in xs]
    row = lax.broadcasted_iota(jnp.int32, (sub, d), 0)
    wc = w_conv_ref[...]
    prev = carry_ref[...]
    gated = []
    for proj in projs:
        gate_b, gate_c, h = proj[:, :d], proj[:, d:2 * d], proj[:, 2 * d:]
        u = gate_c * h
        u1 = jnp.where(row == 0, prev[SUBLANES - 1:SUBLANES, :], pltpu.roll(u, 1, axis=0))
        u2 = jnp.where(row == 0, prev[SUBLANES - 2:SUBLANES - 1, :],
                       jnp.where(row == 1, prev[SUBLANES - 1:SUBLANES, :], pltpu.roll(u, 2, axis=0)))
        prev = u[sub - SUBLANES:, :]
        conv = wc[0:1, :] * u2 + wc[1:2, :] * u1 + wc[2:3, :] * u
        gated.append((gate_b * conv).astype(jnp.bfloat16))
    carry_ref[...] = prev
    mixes = [jnp.dot(v, w_out_ref[...], preferred_element_type=f32) for v in gated]
    x1 = jnp.concatenate([_layer_norm(ALPHA * x + mix, ln_g_ref[...], ln_b_ref[...])
                          for x, mix in zip(xs, mixes)], axis=0)
    meta = _route_and_rank(x1, rw_hi_ref[...], rw_lo_ref[...], rb_ref[...], count_ref)
    out_ref[:, :d] = x1
    out_ref[:, d:] = jnp.concatenate([meta, jnp.zeros((LANES - SUBLANES, meta.shape[1]), meta.dtype)], axis=0).T
    meta_t_ref[...] = meta
    _cast_slabs((wg32_ref, wu32_ref, wd32_ref), (wg16_ref, wu16_ref, wd16_ref))


def _const_spec(shape):
    return pl.BlockSpec(shape, lambda i: (0,) * len(shape))


def _pad_rows(a, rows=SUBLANES):
    return jnp.pad(a, ((0, rows - a.shape[0]), (0, 0)))


def _prep_router(router_w, router_b):
    order = np.array([g * EXPERTS_PER_GROUP + j for j in range(EXPERTS_PER_GROUP) for g in range(N_GROUPS)])
    rw = jnp.pad(router_w.astype(jnp.float32)[:, order], ((0, 0), (0, LANES - N_EXPERTS)))
    rb = router_b.astype(jnp.float32)[order][:, None]
    return _split_bf16(rw), rb


def _conv_mixer_layer(x2d, seq, w_in, w_conv, w_out, ln_g, ln_b, rw_hi, rw_lo, rb, layer, expert_weights):
    t, d = x2d.shape
    ts = MIX_TILE
    grid = (t // ts,)
    cast_args, cast_in, cast_shapes, cast_out = _cast_slab_specs(layer, grid[0], expert_weights)
    return pl.pallas_call(
        functools.partial(_conv_mixer_kernel, tiles_per_seq=seq // ts),
        out_shape=(jax.ShapeDtypeStruct((t, ROW_W), jnp.float32),
                   jax.ShapeDtypeStruct((SUBLANES, t), jnp.float32),
                   jax.ShapeDtypeStruct((CLASS_ROWS, LANES), jnp.float32), *cast_shapes),
        grid=grid,
        in_specs=[pl.BlockSpec((ts, d), lambda i: (i, 0)),
                  _const_spec((d, 3 * d)), _const_spec((SUBLANES, d)), _const_spec((d, d)),
                  _const_spec((1, d)), _const_spec((1, d)),
                  _const_spec((d, LANES)), _const_spec((d, LANES)), _const_spec((N_EXPERTS, 1)), *cast_in],
        out_specs=(pl.BlockSpec((ts, ROW_W), lambda i: (i, 0)),
                   pl.BlockSpec((SUBLANES, ts), lambda i: (0, i)),
                   _const_spec((CLASS_ROWS, LANES)), *cast_out),
        scratch_shapes=[pltpu.VMEM((SUBLANES, d), jnp.float32)],
        compiler_params=pltpu.CompilerParams(dimension_semantics=("arbitrary",),
                                             vmem_limit_bytes=VMEM_LIMIT_BYTES),
        name="conv_mixer_route",
    )(x2d, w_in, w_conv, w_out, ln_g, ln_b, rw_hi, rw_lo, rb, *cast_args)


def _log_sigmoid(z):
    return jnp.minimum(z, 0.0) - jnp.log1p(jnp.exp(-jnp.abs(z)))


def _split3_f32(a):
    f32, bf16 = jnp.float32, jnp.bfloat16
    hi = a.astype(bf16).astype(f32)
    r1 = a - hi
    mid = r1.astype(bf16).astype(f32)
    lo = (r1 - mid).astype(bf16).astype(f32)
    return hi, mid, lo


_COL_U, _COL_ONE, _COL_W, _COL_EMT, _COL_END = 0, 24, 48, 64, 80


def _mlstm_mixer_kernel(x_ref, w_qvo_ref, wkt_ref, wgt_hi_ref, bg_row_ref, norm_g_ref, w_out_ref,
                        ln_g_ref, ln_b_ref, rw_hi_ref, rw_lo_ref, rb_ref, wg32_ref, wu32_ref, wd32_ref,
                        out_ref, meta_t_ref, count_ref, wg16_ref, wu16_ref, wd16_ref,
                        proj_ref, kt_ref, grow_ref, colmat_ref, bexp_ref, h_ref, c_ref, m_ref, *, tiles_per_seq):
    i = pl.program_id(0)
    f32, bf16 = jnp.float32, jnp.bfloat16
    nh, dk, dv, d = ML_HEADS, ML_QK_DIM, ML_V_DIM, D_MODEL
    L = ML_CHUNK
    ts = x_ref.shape[0]
    n_chunks = ts // L
    o_base = ML_QK_W + ML_V_W

    @pl.when(i == 0)
    def _():
        count_ref[...] = jnp.zeros_like(count_ref)

    @pl.when(i % tiles_per_seq == 0)
    def _():
        c_ref[...] = jnp.zeros_like(c_ref)
        m_ref[...] = jnp.zeros_like(m_ref)

    x = x_ref[...]
    x_hi, x_lo = _split_bf16(x)
    nt = (((1,), (1,)), ((), ()))
    kg_t = lax.dot_general(wkt_ref[...], x_hi, nt, preferred_element_type=f32)
    k_t, hi_terms = kg_t[:ML_QK_W, :], kg_t[ML_QK_W:, :]
    g_row = (hi_terms[:2 * nh, :] + hi_terms[2 * nh:, :]
             + lax.dot_general(wgt_hi_ref[...], x_lo, nt, preferred_element_type=f32)) + bg_row_ref[...]
    r_i = lax.broadcasted_iota(jnp.int32, (ts, ts), 0)
    c_i = lax.broadcasted_iota(jnp.int32, (ts, ts), 1)
    tri_row = jnp.where(((r_i // L) == (c_i // L)) & (r_i <= c_i), 1.0, 0.0).astype(bf16)
    lf_hi, lf_lo = _split_bf16(_log_sigmoid(g_row[nh:, :]))
    cum = jnp.dot(jnp.concatenate([lf_hi, lf_lo], axis=0), tri_row, preferred_element_type=f32)
    i_row, b_row = g_row[:nh, :], cum[:nh, :] + cum[nh:, :]
    proj_ref[...] = jnp.dot(x_hi, w_qvo_ref[...], preferred_element_type=f32)

    g = i_row - b_row
    lane_in_chunk = lax.broadcasted_iota(jnp.int32, (nh, ts), 1) & (L - 1)
    cm = g
    shift = 1
    while shift < L:
        cm = jnp.maximum(cm, jnp.where(lane_in_chunk >= shift, pltpu.roll(cm, shift, axis=1), -jnp.inf))
        shift *= 2
    m_prev = m_ref[:, 0:1]
    u_parts, w_parts, gr_parts, gs_parts = [], [], [], []
    for c in range(n_chunks):
        sl = slice(c * L, (c + 1) * L)
        u_c = jnp.maximum(m_prev, cm[:, sl])
        b_last = b_row[:, (c + 1) * L - 1:(c + 1) * L]
        m_new = b_last + u_c[:, L - 1:L]
        u_parts.append(u_c)
        w_parts.append(jnp.exp(m_prev - u_c))
        gr_parts.append(jnp.exp(b_last - b_row[:, sl] + i_row[:, sl] - m_new))
        gs_parts.append(jnp.broadcast_to(jnp.exp(b_last + m_prev - m_new), (nh, L)))
        m_prev = m_new
    m_ref[...] = jnp.broadcast_to(m_prev, m_ref.shape)
    u = jnp.concatenate(u_parts, axis=1)
    w_inter = jnp.concatenate(w_parts, axis=1)
    emt = jnp.exp(-(b_row + u))
    w_hi = w_inter.astype(bf16).astype(f32)
    e_hi = emt.astype(bf16).astype(f32)
    col_rows = jnp.concatenate(
        [*_split3_f32(u), jnp.ones((_COL_W - _COL_ONE, ts), f32), w_hi, w_inter - w_hi, e_hi, emt - e_hi,
         jnp.zeros((LANES - _COL_END, ts), f32)], axis=0)
    colmat_ref[...] = col_rows.T.astype(bf16)
    b_exp = jnp.concatenate([jnp.full((_COL_ONE - _COL_U, ts), -1.0, f32), *_split3_f32(g),
                             jnp.zeros((LANES - _COL_W, ts), f32)], axis=0).astype(bf16)
    for c in range(n_chunks):
        sl = slice(c * L, (c + 1) * L)
        bexp_ref[c] = b_exp[:, sl]
        grow_ref[c] = jnp.concatenate([gr_parts[c], gs_parts[c]], axis=0)
        kt_ref[c] = k_t[:, sl]

    causal = lax.broadcasted_iota(jnp.int32, (L, L), 0) >= lax.broadcasted_iota(jnp.int32, (L, L), 1)
    ones_col = jnp.where(lax.broadcasted_iota(jnp.int32, (L, dv), 1) == 0, 1.0, 0.0).astype(bf16)
    lane_ll = lax.broadcasted_iota(jnp.int32, (L, LANES), 1)
    sel_r = lax.broadcasted_iota(jnp.int32, (LANES, 2 * LANES), 0)
    sel_c = lax.broadcasted_iota(jnp.int32, (LANES, 2 * LANES), 1)
    is_w_row = (sel_r >= _COL_W) & (sel_r < _COL_EMT)
    is_e_row = (sel_r >= _COL_EMT) & (sel_r < _COL_END)
    b_sel = jnp.where((is_w_row & (sel_c < LANES)) | (is_e_row & (sel_c >= LANES)), 1.0, 0.0).astype(bf16)
    r2_r = lax.broadcasted_iota(jnp.int32, (2 * dv, 2 * dv), 0)
    r2_c = lax.broadcasted_iota(jnp.int32, (2 * dv, 2 * dv), 1)
    rhs2 = jnp.where((r2_r < dv) & (r2_c < dv), 1.0 / dv,
                     jnp.where((r2_r == dv) & (r2_c >= dv), 1.0, 0.0)).astype(bf16)

    def chunk_body(c, carry):
        r0 = pl.multiple_of(c * L, L)
        gr = grow_ref[c]
        colmat = colmat_ref[pl.ds(r0, L), :]
        rhs1 = jnp.concatenate([bexp_ref[c], b_sel], axis=1)
        heads = range(nh)
        c_pairs = [c_ref[p] for p in range(nh // 2)]
        kt_pairs = [kt_ref[c, p * LANES:(p + 1) * LANES, :] for p in range(nh // 2)]

        ew, q_m, s_mat, v_ext = [], [], [], []
        for h in heads:
            p, hh = divmod(h, 2)
            lhs = jnp.where((lane_ll & (nh - 1)) == h, colmat, 0)
            ew.append(jnp.dot(lhs, rhs1, preferred_element_type=f32))
            q2 = proj_ref[pl.ds(r0, L), p * LANES:(p + 1) * LANES] * (dk ** -0.5)
            q_m.append(jnp.where((lane_ll >= hh * dk) & (lane_ll < (hh + 1) * dk), q2, 0.0))
            s_mat.append(jnp.dot(q_m[h].astype(bf16), kt_pairs[p].astype(bf16), preferred_element_type=f32))
            v = proj_ref[pl.ds(r0, L), ML_QK_W + h * dv:ML_QK_W + (h + 1) * dv].astype(bf16)
            v_ext.append(jnp.concatenate([v, ones_col], axis=1))
        tot = []
        for h in heads:
            p, hh = divmod(h, 2)
            a = jnp.exp(jnp.where(causal, ew[h][:, :L], -jnp.inf)) * s_mat[h]
            qw = q_m[h] * ew[h][:, L:L + LANES]
            lhs = jnp.concatenate([a.astype(bf16), qw.astype(bf16)], axis=1)
            rhs = jnp.concatenate([v_ext[h], c_pairs[p].astype(bf16)], axis=0)
            tot.append(jnp.dot(lhs, rhs, preferred_element_type=f32))
        for h in heads:
            num = tot[h][:, :dv]
            lhs2 = jnp.concatenate([(num * num).astype(bf16), tot[h][:, dv:].astype(bf16)], axis=1)
            r2 = jnp.dot(lhs2, rhs2, preferred_element_type=f32)
            inv = 1.0 / jnp.maximum(jnp.abs(r2[:, dv:]), ew[h][:, L + LANES:])
            hn = (num * inv) * lax.rsqrt(r2[:, :dv] * inv * inv + HEAD_NORM_EPS)
            o_pre = proj_ref[pl.ds(r0, L), o_base + h * dv:o_base + (h + 1) * dv]
            h_ref[pl.ds(r0, L), h * dv:(h + 1) * dv] = (
                jax.nn.sigmoid(o_pre) * hn * norm_g_ref[:, h * dv:(h + 1) * dv])
        for h in heads:
            p, hh = divmod(h, 2)
            rows = slice(hh * dk, (hh + 1) * dk)
            kg = (kt_pairs[p][rows, :] * gr[h:h + 1, :]).astype(bf16)
            gs = gr[nh + h:nh + h + 1, :]
            c_ref[p, rows, :] = (jnp.concatenate([gs, gs], axis=1) * c_pairs[p][rows, :]
                                 + jnp.dot(kg, v_ext[h], preferred_element_type=f32))
        return carry

    lax.fori_loop(0, n_chunks, chunk_body, 0, unroll=True)

    sub = ts // MIX_SUBTILES
    mixes = [jnp.dot(h_ref[k * sub:(k + 1) * sub, :].astype(bf16), w_out_ref[...], preferred_element_type=f32)
             for k in range(MIX_SUBTILES)]
    x1 = jnp.concatenate([_layer_norm(ALPHA * x[k * sub:(k + 1) * sub, :] + mixes[k], ln_g_ref[...], ln_b_ref[...])
                          for k in range(MIX_SUBTILES)], axis=0)
    meta = _route_and_rank(x1, rw_hi_ref[...], rw_lo_ref[...], rb_ref[...], count_ref)
    out_ref[:, :d] = x1
    out_ref[:, d:] = jnp.concatenate([meta, jnp.zeros((LANES - SUBLANES, meta.shape[1]), meta.dtype)], axis=0).T
    meta_t_ref[...] = meta
    _cast_slabs((wg32_ref, wu32_ref, wd32_ref), (wg16_ref, wu16_ref, wd16_ref))


def _mlstm_mixer_layer(x2d, seq, w_in, b_gate, norm_g, w_out, ln_g, ln_b, rw_hi, rw_lo, rb, layer,
                       expert_weights):
    t, d = x2d.shape
    ts = MIX_TILE
    nh = ML_HEADS
    cast_args, cast_in, cast_shapes, cast_out = _cast_slab_specs(layer, t // ts, expert_weights)
    assert ML_CHUNK == LANES and ts % ML_CHUNK == 0 and 2 * ML_QK_DIM == LANES and ML_V_DIM == LANES
    n_qkvo = 2 * ML_QK_W + 2 * ML_V_W
    n_qvo = ML_QK_W + 2 * ML_V_W
    w_qvo = jnp.concatenate([w_in[:, :ML_QK_W], w_in[:, 2 * ML_QK_W:n_qkvo]], axis=1).astype(jnp.bfloat16)
    w_g = w_in[:, n_qkvo:].astype(jnp.float32)
    wgt_hi, wgt_lo = _split_bf16(w_g.T)
    wkt = jnp.concatenate([w_in[:, ML_QK_W:2 * ML_QK_W].T.astype(jnp.bfloat16), wgt_hi, wgt_lo], axis=0)
    bg_row = b_gate.astype(jnp.float32)[:, None]
    return pl.pallas_call(
        functools.partial(_mlstm_mixer_kernel, tiles_per_seq=seq // ts),
        out_shape=(jax.ShapeDtypeStruct((t, ROW_W), jnp.float32),
                   jax.ShapeDtypeStruct((SUBLANES, t), jnp.float32),
                   jax.ShapeDtypeStruct((CLASS_ROWS, LANES), jnp.float32), *cast_shapes),
        grid=(t // ts,),
        in_specs=[pl.BlockSpec((ts, d), lambda i: (i, 0)),
                  _const_spec((d, n_qvo)), _const_spec((ML_QK_W + 4 * nh, d)),
                  _const_spec((2 * nh, d)),
                  _const_spec((2 * nh, 1)),
                  _const_spec((1, d)), _const_spec((d, d)),
                  _const_spec((1, d)), _const_spec((1, d)),
                  _const_spec((d, LANES)), _const_spec((d, LANES)), _const_spec((N_EXPERTS, 1)), *cast_in],
        out_specs=(pl.BlockSpec((ts, ROW_W), lambda i: (i, 0)),
                   pl.BlockSpec((SUBLANES, ts), lambda i: (0, i)),
                   _const_spec((CLASS_ROWS, LANES)), *cast_out),
        scratch_shapes=[pltpu.VMEM((ts, n_qvo), jnp.float32),
                        pltpu.VMEM((ts // ML_CHUNK, ML_QK_W, ML_CHUNK), jnp.float32),
                        pltpu.VMEM((ts // ML_CHUNK, 2 * nh, ML_CHUNK), jnp.float32),
                        pltpu.VMEM((ts, LANES), jnp.bfloat16),
                        pltpu.VMEM((ts // ML_CHUNK, LANES, ML_CHUNK), jnp.bfloat16),
                        pltpu.VMEM((ts, d), jnp.float32),
                        pltpu.VMEM((nh // 2, 2 * ML_QK_DIM, 2 * ML_V_DIM), jnp.float32),
                        pltpu.VMEM((nh, ML_CHUNK), jnp.float32)],
        compiler_params=pltpu.CompilerParams(dimension_semantics=("arbitrary",),
                                             vmem_limit_bytes=VMEM_LIMIT_BYTES),
        name="mlstm_mixer_route",
    )(x2d, w_qvo, wkt, wgt_hi, bg_row, norm_g[None, :], w_out.astype(jnp.bfloat16),
      ln_g, ln_b, rw_hi, rw_lo, rb, *cast_args)


def _expert_kernel(dest_ref, elo_ref, ehi_ref, nvalid_ref,
                   x_hbm, wg_lo_ref, wu_lo_ref, wd_lo_ref, wg_hi_ref, wu_hi_ref, wd_hi_ref,
                   ln_g_ref, ln_b_ref, out_hbm, xbuf0, xbuf1, obuf0, obuf1, tok_ref, gather_sem, scatter_sem):
    i = pl.program_id(0)
    nb = pl.num_programs(0)
    bm = obuf0.shape[0]
    d = D_MODEL

    @pl.when(i == 0)
    def _():
        def pad_block(b, carry):
            def pad_row(r, c):
                tok_ref[b * bm + r] = 0
                return c
            first = jnp.where(b < nb, nvalid_ref[jnp.minimum(b, nb - 1)], 0)
            is_gathered = (b == 0) | (nvalid_ref[jnp.maximum(b - 1, 0)] > 0)
            return lax.fori_loop(jnp.where(is_gathered, first, bm), bm, pad_row, carry)

        def place(t, c):
            tok_ref[dest_ref[t]] = t
            return c

        lax.fori_loop(0, nb + 1, pad_block, 0)
        lax.fori_loop(0, x_hbm.shape[0], place, 0, unroll=16)
    xbufs, obufs = (xbuf0, xbuf1), (obuf0, obuf1)
    nv = nvalid_ref[i]
    nv_prev = nvalid_ref[jnp.maximum(i - 1, 0)]

    def start_gather(j, s, rows=range(bm)):
        for r in rows:
            tok = tok_ref[j * bm + r]
            pltpu.make_async_copy(x_hbm.at[pl.ds(tok, 1)], xbufs[s].at[pl.ds(r, 1)], gather_sem.at[s]).start()

    def wait_gather(s):
        pltpu.make_async_copy(xbufs[s].at[pl.ds(0, bm)], xbufs[s].at[pl.ds(0, bm)], gather_sem.at[s]).wait()

    def scatter_copy(j, s, r):
        tok = tok_ref[j * bm + r]
        return pltpu.make_async_copy(obufs[s].at[pl.ds(r, 1)], out_hbm.at[pl.ds(tok, 1)], scatter_sem.at[s])

    def for_rows(n, fn):
        groups = n // ROW_UNROLL

        def group_body(g, c):
            for k in range(ROW_UNROLL):
                fn(g * ROW_UNROLL + k)
            return c

        def row_body(r, c):
            fn(r)
            return c

        lax.fori_loop(0, groups, group_body, 0)
        lax.fori_loop(groups * ROW_UNROLL, n, row_body, 0)

    def wait_scatter(s, n):
        @pl.when(n == bm)
        def _():
            pltpu.make_async_copy(obufs[s], obufs[s], scatter_sem.at[s]).wait()

        @pl.when(n < bm)
        def _():
            one_row = pltpu.make_async_copy(obufs[s].at[pl.ds(0, 1)], obufs[s].at[pl.ds(0, 1)], scatter_sem.at[s])
            for_rows(n, lambda r: one_row.wait())

    @pl.when(i == 0)
    def _():
        for buf in xbufs:
            buf[bm:, :] = jnp.zeros((buf.shape[0] - bm, buf.shape[1]), buf.dtype)
        start_gather(0, 0)

    def step(s):
        @pl.when((i == 0) | (nv_prev > 0))
        def _():
            wait_gather(s)

        @pl.when(nv > 0)
        def _():
            xb = xbufs[s][0:bm, :]
            x = xb[:, :d]
            g_lo = xb[:, d + META_G_LO:d + META_G_LO + 1]
            g_hi = xb[:, d + META_G_HI:d + META_G_HI + 1]
            x16 = x.astype(jnp.bfloat16)
            n_portions = 2 * GATE_CHUNKS

            def gate_proj(w_ref, k0):
                g = jnp.dot(x16, w_ref[0], preferred_element_type=jnp.float32)
                width = g.shape[1] // GATE_CHUNKS
                chunks = []
                for c in range(GATE_CHUNKS):
                    k = k0 + c
                    start_gather(i + 1, 1 - s, range(k * bm // n_portions, (k + 1) * bm // n_portions))
                    zero_row = xbufs[1 - s][bm:bm + 1, 0:LANES]
                    chunks.append(g[:, c * width:(c + 1) * width]
                                  + jnp.concatenate([zero_row] * (width // LANES), axis=1))
                return jnp.concatenate(chunks, axis=1)

            def ffn(k0, wg_ref, wu_ref, wd_ref):
                g = gate_proj(wg_ref, k0)
                u = jnp.dot(x16, wu_ref[0], preferred_element_type=jnp.float32)
                h = (g * jax.nn.sigmoid(g)) * u
                return jnp.dot(h.astype(jnp.bfloat16), wd_ref[0], preferred_element_type=jnp.float32)

            y = (g_lo * ffn(0, wg_lo_ref, wu_lo_ref, wd_lo_ref)
                 + g_hi * ffn(GATE_CHUNKS, wg_hi_ref, wu_hi_ref, wd_hi_ref))
            obufs[s][...] = _layer_norm(ALPHA * x + y, ln_g_ref[...], ln_b_ref[...])

            @pl.when(nv == bm)
            def _():
                for r in range(bm):
                    scatter_copy(i, s, r).start()

            @pl.when(nv < bm)
            def _():
                for_rows(nv, lambda r: scatter_copy(i, s, r).start())

        @pl.when(i > 0)
        def _():
            wait_scatter(1 - s, nv_prev)

        @pl.when(i == nb - 1)
        def _():
            wait_scatter(s, nv)

            @pl.when(nv > 0)
            def _():
                wait_gather(1 - s)

    for s in range(2):
        pl.when(i % 2 == s)(functools.partial(step, s))


def _moe_layer(xext, meta_t, counts, w_gate, w_up, w_down, ln_g, ln_b):
    t = xext.shape[0]
    d, f, bm = D_MODEL, D_EXPERT, EXPERT_BLOCK
    n_blocks = t // bm + N_CLASSES
    n_rows = n_blocks * bm

    cls = meta_t[META_CLASS].astype(jnp.int32)
    rank = meta_t[META_RANK].astype(jnp.int32)
    cnt = counts[:N_CLASSES, 0].astype(jnp.int32)
    cls_blocks = (cnt + bm - 1) // bm
    blk_end = jnp.cumsum(cls_blocks)
    blk_start = blk_end - cls_blocks
    total_blocks = blk_end[-1]
    row_start = jnp.sum(jnp.where(cls[None, :] == jnp.arange(N_CLASSES, dtype=jnp.int32)[:, None],
                                  blk_start[:, None] * bm, 0), axis=0)
    dest = (row_start + rank).astype(jnp.int32)
    blk = jnp.arange(n_blocks, dtype=jnp.int32)
    b = jnp.minimum(blk, total_blocks - 1)[None, :]
    in_cls = (b >= blk_start[:, None]) & (b < blk_end[:, None])
    pick = lambda table: jnp.sum(jnp.where(in_cls, table[:, None], 0), axis=0).astype(jnp.int32)
    nvalid = jnp.clip(pick(cnt) - (blk - pick(blk_start)) * bm, 0, bm)
    nvalid = jnp.where(blk < total_blocks, nvalid, 0).astype(jnp.int32)
    e_lo = pick(jnp.asarray(CLASS_E_LO))
    e_hi = pick(jnp.asarray(CLASS_E_HI))

    w_gate, w_up, w_down = (w_gate.reshape(N_EXPERTS, d, f), w_up.reshape(N_EXPERTS, d, f),
                            w_down.reshape(N_EXPERTS, f, d))
    w_lo = lambda shape: pl.BlockSpec((1,) + shape, lambda i, tok, elo, ehi, nv: (elo[i], 0, 0))
    w_hi = lambda shape: pl.BlockSpec((1,) + shape, lambda i, tok, elo, ehi, nv: (ehi[i], 0, 0))
    vec = pl.BlockSpec((1, d), lambda i, tok, elo, ehi, nv: (0, 0))
    grid_spec = pltpu.PrefetchScalarGridSpec(
        num_scalar_prefetch=4,
        grid=(n_blocks,),
        in_specs=[pl.BlockSpec(memory_space=pl.ANY),
                  w_lo((d, f)), w_lo((d, f)), w_lo((f, d)),
                  w_hi((d, f)), w_hi((d, f)), w_hi((f, d)),
                  vec, vec],
        out_specs=pl.BlockSpec(memory_space=pl.ANY),
        scratch_shapes=[pltpu.VMEM((bm + SUBLANES, ROW_W), jnp.float32), pltpu.VMEM((bm + SUBLANES, ROW_W), jnp.float32),
                        pltpu.VMEM((bm, d), jnp.float32), pltpu.VMEM((bm, d), jnp.float32),
                        pltpu.SMEM((n_rows + bm,), jnp.int32),
                        pltpu.SemaphoreType.DMA((2,)),
                        pltpu.SemaphoreType.DMA((2,))],
    )
    return pl.pallas_call(
        _expert_kernel,
        out_shape=jax.ShapeDtypeStruct((t, d), jnp.float32),
        grid_spec=grid_spec,
        compiler_params=pltpu.CompilerParams(dimension_semantics=("arbitrary",),
                                             vmem_limit_bytes=VMEM_LIMIT_BYTES),
        name="expert_pair_ffn",
    )(dest, e_lo, e_hi, nvalid, xext, w_gate, w_up, w_down, w_gate, w_up, w_down, ln_g, ln_b)


def kernel(x, conv_w_in, conv_w, conv_w_out, ml_w_in, ml_b_gate, ml_norm_g, ml_w_out, ln_mix_g, ln_mix_b,
           ln_ffn_g, ln_ffn_b, router_w, router_b, exp_w_gate, exp_w_up, exp_w_down):
    bsz, seq, d = x.shape
    assert d == D_MODEL and seq % MIX_TILE == 0 and (bsz * seq) % EXPERT_BLOCK == 0
    bf16 = jnp.bfloat16
    vec = lambda a: a.astype(jnp.float32)[None, :]
    (rw_hi, rw_lo), rb = _prep_router(router_w, router_b)
    x2d = x.reshape(bsz * seq, d).astype(jnp.float32)
    expert_weights = (exp_w_gate.astype(jnp.float32), exp_w_up.astype(jnp.float32),
                      exp_w_down.astype(jnp.float32))
    for i in range(DEPTH):
        j = i // 2
        if i % 2 == 0:
            xext, meta_t, counts, *w16 = _conv_mixer_layer(
                x2d, seq, conv_w_in[j].astype(bf16), _pad_rows(conv_w[j].astype(jnp.float32)),
                conv_w_out[j].astype(bf16), vec(ln_mix_g[i]), vec(ln_mix_b[i]), rw_hi, rw_lo, rb,
                i, expert_weights)
        else:
            xext, meta_t, counts, *w16 = _mlstm_mixer_layer(
                x2d, seq, ml_w_in[j], ml_b_gate[j], ml_norm_g[j].astype(jnp.float32), ml_w_out[j],
                vec(ln_mix_g[i]), vec(ln_mix_b[i]), rw_hi, rw_lo, rb, i, expert_weights)
        x2d = _moe_layer(xext, meta_t, counts, *w16, vec(ln_ffn_g[i]), vec(ln_ffn_b[i]))
    return x2d.reshape(bsz, seq, d).astype(x.dtype)
```

```python
import functools
import itertools

import numpy as np
import jax
import jax.numpy as jnp
from jax import lax
from jax.experimental import pallas as pl
from jax.experimental.pallas import tpu as pltpu

D_MODEL = 1024
DEPTH = 2
CONV_WIDTH = 3
ML_HEADS = 8
ML_QK_DIM = D_MODEL // (2 * ML_HEADS)
ML_V_DIM = D_MODEL // ML_HEADS
ML_QK_W = ML_HEADS * ML_QK_DIM
ML_V_W = ML_HEADS * ML_V_DIM
N_EXPERTS = 16
N_GROUPS = 4
EXPERTS_PER_GROUP = N_EXPERTS // N_GROUPS
D_EXPERT = 3 * D_MODEL // 2
ALPHA = (2 * DEPTH) ** 0.25
LN_EPS = 1e-5
HEAD_NORM_EPS = 1e-6

LANES = 128
SUBLANES = 8
V7X_VMEM_BYTES = 64 * 1024 * 1024
VMEM_LIMIT_BYTES = V7X_VMEM_BYTES * 7 // 8

PAIRS = tuple(itertools.combinations(range(EXPERTS_PER_GROUP), 2))
N_PAIRS = len(PAIRS)
N_CLASSES = N_GROUPS * N_PAIRS
CLASS_ROWS = -(-N_CLASSES // 16) * 16
CLASS_E_LO = np.array([g * EXPERTS_PER_GROUP + a for g in range(N_GROUPS) for a, _ in PAIRS], np.int32)
CLASS_E_HI = np.array([g * EXPERTS_PER_GROUP + b for g in range(N_GROUPS) for _, b in PAIRS], np.int32)

META_G_LO, META_G_HI, META_CLASS, META_RANK = 0, 1, 2, 3
ROW_W = D_MODEL + LANES

MIX_TILE = 512
MIX_SUBTILES = 2
EXPERT_BLOCK = 256
ROW_UNROLL = 8
GATE_CHUNKS = 6
ML_CHUNK = 128


def _layer_norm(z, g, b):
    mu = jnp.mean(z, axis=-1, keepdims=True)
    zc = z - mu
    var = jnp.mean(zc * zc, axis=-1, keepdims=True)
    return zc * lax.rsqrt(var + LN_EPS) * g + b


def _split_bf16(a):
    hi = a.astype(jnp.bfloat16)
    lo = (a - hi.astype(jnp.float32)).astype(jnp.bfloat16)
    return hi, lo


def _route_and_rank(x1, rw_hi, rw_lo, rb_col, count_ref):
    ts = x1.shape[0]
    f32, bf16 = jnp.float32, jnp.bfloat16
    n_g, n_m = N_GROUPS, EXPERTS_PER_GROUP

    x_hi, x_lo = _split_bf16(x1)
    hi_terms = jnp.dot(x_hi, jnp.concatenate([rw_hi, rw_lo], axis=1), preferred_element_type=f32)
    logits = (hi_terms[:, :LANES] + hi_terms[:, LANES:]
              + jnp.dot(x_lo, rw_hi, preferred_element_type=f32))
    lt = logits.T[:N_EXPERTS, :]
    ex = jnp.exp(lt - jnp.max(lt, axis=0, keepdims=True))
    probs = ex / jnp.sum(ex, axis=0, keepdims=True)
    sel = probs + rb_col
    member = lambda a, j: a[j * n_g:(j + 1) * n_g, :]

    in_top2 = []
    for j in range(n_m):
        beaten_by = jnp.zeros((n_g, ts), jnp.int32)
        for k in range(n_m):
            if k != j:
                wins = member(sel, k) > member(sel, j)
                if k < j:
                    wins = wins | (member(sel, k) == member(sel, j))
                beaten_by = beaten_by + wins.astype(jnp.int32)
        in_top2.append(beaten_by < 2)
    score = sum(jnp.where(in_top2[j], member(sel, j), 0.0) for j in range(n_m))
    group = lax.broadcasted_iota(jnp.int32, (n_g, ts), 0)
    best = jnp.max(score, axis=0, keepdims=True)
    g_best = jnp.min(jnp.where(score == best, group, n_g), axis=0, keepdims=True)
    in_best = group == g_best
    picked = [jnp.max(jnp.where(in_top2[j] & in_best, 1, 0), axis=0, keepdims=True) for j in range(n_m)]
    p_pick = [jnp.sum(jnp.where(in_top2[j] & in_best, member(probs, j), 0.0), axis=0, keepdims=True)
              for j in range(n_m)]
    a = functools.reduce(jnp.minimum, [jnp.where(picked[j] > 0, j, n_m) for j in range(n_m)])
    b = functools.reduce(jnp.maximum, [jnp.where(picked[j] > 0, j, -1) for j in range(n_m)])
    p_lo = sum(jnp.where(a == j, p_pick[j], 0.0) for j in range(n_m))
    p_hi = sum(jnp.where(b == j, p_pick[j], 0.0) for j in range(n_m))
    g_lo = p_lo / (p_lo + p_hi)
    g_hi = p_hi / (p_lo + p_hi)
    pair = ((a * (2 * n_m - 1 - a)) >> 1) + (b - a - 1)
    cls = g_best * N_PAIRS + pair

    n_rows = count_ref.shape[0]
    onehot = jnp.where(lax.broadcasted_iota(jnp.int32, (n_rows, ts), 0) == cls, 1.0, 0.0)
    earlier = jnp.where(lax.broadcasted_iota(jnp.int32, (ts, ts), 0)
                        < lax.broadcasted_iota(jnp.int32, (ts, ts), 1), 1.0, 0.0).astype(bf16)
    before = jnp.dot(onehot.astype(bf16), earlier, preferred_element_type=f32)
    running = count_ref[...]
    running_ts = jnp.concatenate([running] * (ts // LANES), axis=1)
    rank = jnp.sum(onehot * (before + running_ts), axis=0, keepdims=True)
    count_ref[...] = running + jnp.dot(onehot.astype(bf16), jnp.ones((ts, LANES), bf16),
                                       preferred_element_type=f32)
    return jnp.concatenate([g_lo, g_hi, cls.astype(f32), rank, jnp.zeros((SUBLANES - 4, ts), f32)], axis=0)


def _cast_slabs(w32_refs, w16_refs):
    for w32_ref, w16_ref in zip(w32_refs, w16_refs):
        w16_ref[...] = w32_ref[0].astype(w16_ref.dtype)


def _cast_slab_specs(layer, n_steps, weights):
    args, in_specs, out_shapes, out_specs = [], [], [], []
    for w in weights:
        n_layers, n_exp, rows, cols = w.shape
        slab, rem = divmod(n_exp * rows, n_steps)
        assert rem == 0 and slab % (2 * SUBLANES) == 0
        args.append(w.reshape(n_layers, n_exp * rows, cols))
        in_specs.append(pl.BlockSpec((1, slab, cols), lambda i: (layer, i, 0)))
        out_shapes.append(jax.ShapeDtypeStruct((n_exp * rows, cols), jnp.bfloat16))
        out_specs.append(pl.BlockSpec((slab, cols), lambda i: (i, 0)))
    return args, in_specs, out_shapes, out_specs


def _conv_mixer_kernel(x_ref, w_in_ref, w_conv_ref, w_out_ref, ln_g_ref, ln_b_ref,
                       rw_hi_ref, rw_lo_ref, rb_ref, wg32_ref, wu32_ref, wd32_ref,
                       out_ref, meta_t_ref, count_ref, wg16_ref, wu16_ref, wd16_ref, carry_ref, *, tiles_per_seq):
    i = pl.program_id(0)

    @pl.when(i == 0)
    def _():
        count_ref[...] = jnp.zeros_like(count_ref)

    @pl.when(i % tiles_per_seq == 0)
    def _():
        carry_ref[...] = jnp.zeros_like(carry_ref)

    f32 = jnp.float32
    d = D_MODEL
    ts = x_ref.shape[0]
    sub = ts // MIX_SUBTILES
    xs = [x_ref[k * sub:(k + 1) * sub, :] for k in range(MIX_SUBTILES)]
    projs = [jnp.dot(x.astype(jnp.bfloat16), w_in_ref[...], preferred_element_type=f32) for x in xs]
    row = lax.broadcasted_iota(jnp.int32, (sub, d), 0)
    wc = w_conv_ref[...]
    prev = carry_ref[...]
    gated = []
    for proj in projs:
        gate_b, gate_c, h = proj[:, :d], proj[:, d:2 * d], proj[:, 2 * d:]
        u = gate_c * h
        u1 = jnp.where(row == 0, prev[SUBLANES - 1:SUBLANES, :], pltpu.roll(u, 1, axis=0))
        u2 = jnp.where(row == 0, prev[SUBLANES - 2:SUBLANES - 1, :],
                       jnp.where(row == 1, prev[SUBLANES - 1:SUBLANES, :], pltpu.roll(u, 2, axis=0)))
        prev = u[sub - SUBLANES:, :]
        conv = wc[0:1, :] * u2 + wc[1:2, :] * u1 + wc[2:3, :] * u
        gated.append((gate_b * conv).astype(jnp.bfloat16))
    carry_ref[...] = prev
    mixes = [jnp.dot(v, w_out_ref[...], preferred_element_type=f32) for v in gated]
    x1 = jnp.concatenate([_layer_norm(ALPHA * x + mix, ln_g_ref[...], ln_b_ref[...])
                          for x, mix in zip(xs, mixes)], axis=0)
    meta = _route_and_rank(x1, rw_hi_ref[...], rw_lo_ref[...], rb_ref[...], count_ref)
    out_ref[:, :d] = x1
    out_ref[:, d:] = jnp.concatenate([meta, jnp.zeros((LANES - SUBLANES, meta.shape[1]), meta.dtype)], axis=0).T
    meta_t_ref[...] = meta
    _cast_slabs((wg32_ref, wu32_ref, wd32_ref), (wg16_ref, wu16_ref, wd16_ref))


def _const_spec(shape):
    return pl.BlockSpec(shape, lambda i: (0,) * len(shape))


def _pad_rows(a, rows=SUBLANES):
    return jnp.pad(a, ((0, rows - a.shape[0]), (0, 0)))


def _prep_router(router_w, router_b):
    order = np.array([g * EXPERTS_PER_GROUP + j for j in range(EXPERTS_PER_GROUP) for g in range(N_GROUPS)])
    rw = jnp.pad(router_w.astype(jnp.float32)[:, order], ((0, 0), (0, LANES - N_EXPERTS)))
    rb = router_b.astype(jnp.float32)[order][:, None]
    return _split_bf16(rw), rb


def _conv_mixer_layer(x2d, seq, w_in, w_conv, w_out, ln_g, ln_b, rw_hi, rw_lo, rb, layer, expert_weights):
    t, d = x2d.shape
    ts = MIX_TILE
    grid = (t // ts,)
    cast_args, cast_in, cast_shapes, cast_out = _cast_slab_specs(layer, grid[0], expert_weights)
    return pl.pallas_call(
        functools.partial(_conv_mixer_kernel, tiles_per_seq=seq // ts),
        out_shape=(jax.ShapeDtypeStruct((t, ROW_W), jnp.float32),
                   jax.ShapeDtypeStruct((SUBLANES, t), jnp.float32),
                   jax.ShapeDtypeStruct((CLASS_ROWS, LANES), jnp.float32), *cast_shapes),
        grid=grid,
        in_specs=[pl.BlockSpec((ts, d), lambda i: (i, 0)),
                  _const_spec((d, 3 * d)), _const_spec((SUBLANES, d)), _const_spec((d, d)),
                  _const_spec((1, d)), _const_spec((1, d)),
                  _const_spec((d, LANES)), _const_spec((d, LANES)), _const_spec((N_EXPERTS, 1)), *cast_in],
        out_specs=(pl.BlockSpec((ts, ROW_W), lambda i: (i, 0)),
                   pl.BlockSpec((SUBLANES, ts), lambda i: (0, i)),
                   _const_spec((CLASS_ROWS, LANES)), *cast_out),
        scratch_shapes=[pltpu.VMEM((SUBLANES, d), jnp.float32)],
        compiler_params=pltpu.CompilerParams(dimension_semantics=("arbitrary",),
                                             vmem_limit_bytes=VMEM_LIMIT_BYTES),
        name="conv_mixer_route",
    )(x2d, w_in, w_conv, w_out, ln_g, ln_b, rw_hi, rw_lo, rb, *cast_args)


def _log_sigmoid(z):
    return jnp.minimum(z, 0.0) - jnp.log1p(jnp.exp(-jnp.abs(z)))


def _split3_f32(a):
    f32, bf16 = jnp.float32, jnp.bfloat16
    hi = a.astype(bf16).astype(f32)
    r1 = a - hi
    mid = r1.astype(bf16).astype(f32)
    lo = (r1 - mid).astype(bf16).astype(f32)
    return hi, mid, lo


_COL_U, _COL_ONE, _COL_W, _COL_EMT, _COL_END = 0, 24, 48, 64, 80


def _mlstm_mixer_kernel(x_ref, w_qvo_ref, wkt_ref, wgt_hi_ref, bg_row_ref, norm_g_ref, w_out_ref,
                        ln_g_ref, ln_b_ref, rw_hi_ref, rw_lo_ref, rb_ref, wg32_ref, wu32_ref, wd32_ref,
                        out_ref, meta_t_ref, count_ref, wg16_ref, wu16_ref, wd16_ref,
                        proj_ref, kt_ref, grow_ref, colmat_ref, bexp_ref, h_ref, c_ref, m_ref, *, tiles_per_seq):
    i = pl.program_id(0)
    f32, bf16 = jnp.float32, jnp.bfloat16
    nh, dk, dv, d = ML_HEADS, ML_QK_DIM, ML_V_DIM, D_MODEL
    L = ML_CHUNK
    ts = x_ref.shape[0]
    n_chunks = ts // L
    o_base = ML_QK_W + ML_V_W

    @pl.when(i == 0)
    def _():
        count_ref[...] = jnp.zeros_like(count_ref)

    @pl.when(i % tiles_per_seq == 0)
    def _():
        c_ref[...] = jnp.zeros_like(c_ref)
        m_ref[...] = jnp.zeros_like(m_ref)

    x = x_ref[...]
    x_hi, x_lo = _split_bf16(x)
    nt = (((1,), (1,)), ((), ()))
    kg_t = lax.dot_general(wkt_ref[...], x_hi, nt, preferred_element_type=f32)
    k_t, hi_terms = kg_t[:ML_QK_W, :], kg_t[ML_QK_W:, :]
    g_row = (hi_terms[:2 * nh, :] + hi_terms[2 * nh:, :]
             + lax.dot_general(wgt_hi_ref[...], x_lo, nt, preferred_element_type=f32)) + bg_row_ref[...]
    r_i = lax.broadcasted_iota(jnp.int32, (ts, ts), 0)
    c_i = lax.broadcasted_iota(jnp.int32, (ts, ts), 1)
    tri_row = jnp.where(((r_i // L) == (c_i // L)) & (r_i <= c_i), 1.0, 0.0).astype(bf16)
    lf_hi, lf_lo = _split_bf16(_log_sigmoid(g_row[nh:, :]))
    cum = jnp.dot(jnp.concatenate([lf_hi, lf_lo], axis=0), tri_row, preferred_element_type=f32)
    i_row, b_row = g_row[:nh, :], cum[:nh, :] + cum[nh:, :]
    proj_ref[...] = jnp.dot(x_hi, w_qvo_ref[...], preferred_element_type=f32)

    g = i_row - b_row
    lane_in_chunk = lax.broadcasted_iota(jnp.int32, (nh, ts), 1) & (L - 1)
    cm = g
    shift = 1
    while shift < L:
        cm = jnp.maximum(cm, jnp.where(lane_in_chunk >= shift, pltpu.roll(cm, shift, axis=1), -jnp.inf))
        shift *= 2
    m_prev = m_ref[:, 0:1]
    u_parts, w_parts, gr_parts, gs_parts = [], [], [], []
    for c in range(n_chunks):
        sl = slice(c * L, (c + 1) * L)
        u_c = jnp.maximum(m_prev, cm[:, sl])
        b_last = b_row[:, (c + 1) * L - 1:(c + 1) * L]
        m_new = b_last + u_c[:, L - 1:L]
        u_parts.append(u_c)
        w_parts.append(jnp.exp(m_prev - u_c))
        gr_parts.append(jnp.exp(b_last - b_row[:, sl] + i_row[:, sl] - m_new))
        gs_parts.append(jnp.broadcast_to(jnp.exp(b_last + m_prev - m_new), (nh, L)))
        m_prev = m_new
    m_ref[...] = jnp.broadcast_to(m_prev, m_ref.shape)
    u = jnp.concatenate(u_parts, axis=1)
    w_inter = jnp.concatenate(w_parts, axis=1)
    emt = jnp.exp(-(b_row + u))
    w_hi = w_inter.astype(bf16).astype(f32)
    e_hi = emt.astype(bf16).astype(f32)
    col_rows = jnp.concatenate(
        [*_split3_f32(u), jnp.ones((_COL_W - _COL_ONE, ts), f32), w_hi, w_inter - w_hi, e_hi, emt - e_hi,
         jnp.zeros((LANES - _COL_END, ts), f32)], axis=0)
    colmat_ref[...] = col_rows.T.astype(bf16)
    b_exp = jnp.concatenate([jnp.full((_COL_ONE - _COL_U, ts), -1.0, f32), *_split3_f32(g),
                             jnp.zeros((LANES - _COL_W, ts), f32)], axis=0).astype(bf16)
    for c in range(n_chunks):
        sl = slice(c * L, (c + 1) * L)
        bexp_ref[c] = b_exp[:, sl]
        grow_ref[c] = jnp.concatenate([gr_parts[c], gs_parts[c]], axis=0)
        kt_ref[c] = k_t[:, sl]

    causal = lax.broadcasted_iota(jnp.int32, (L, L), 0) >= lax.broadcasted_iota(jnp.int32, (L, L), 1)
    ones_col = jnp.where(lax.broadcasted_iota(jnp.int32, (L, dv), 1) == 0, 1.0, 0.0).astype(bf16)
    lane_ll = lax.broadcasted_iota(jnp.int32, (L, LANES), 1)
    sel_r = lax.broadcasted_iota(jnp.int32, (LANES, 2 * LANES), 0)
    sel_c = lax.broadcasted_iota(jnp.int32, (LANES, 2 * LANES), 1)
    is_w_row = (sel_r >= _COL_W) & (sel_r < _COL_EMT)
    is_e_row = (sel_r >= _COL_EMT) & (sel_r < _COL_END)
    b_sel = jnp.where((is_w_row & (sel_c < LANES)) | (is_e_row & (sel_c >= LANES)), 1.0, 0.0).astype(bf16)
    r2_r = lax.broadcasted_iota(jnp.int32, (2 * dv, 2 * dv), 0)
    r2_c = lax.broadcasted_iota(jnp.int32, (2 * dv, 2 * dv), 1)
    rhs2 = jnp.where((r2_r < dv) & (r2_c < dv), 1.0 / dv,
                     jnp.where((r2_r == dv) & (r2_c >= dv), 1.0, 0.0)).astype(bf16)

    def chunk_body(c, carry):
        r0 = pl.multiple_of(c * L, L)
        gr = grow_ref[c]
        colmat = colmat_ref[pl.ds(r0, L), :]
        rhs1 = jnp.concatenate([bexp_ref[c], b_sel], axis=1)
        heads = range(nh)
        c_pairs = [c_ref[p] for p in range(nh // 2)]
        kt_pairs = [kt_ref[c, p * LANES:(p + 1) * LANES, :] for p in range(nh // 2)]

        ew, q_m, s_mat, v_ext = [], [], [], []
        for h in heads:
            p, hh = divmod(h, 2)
            lhs = jnp.where((lane_ll & (nh - 1)) == h, colmat, 0)
            ew.append(jnp.dot(lhs, rhs1, preferred_element_type=f32))
            q2 = proj_ref[pl.ds(r0, L), p * LANES:(p + 1) * LANES] * (dk ** -0.5)
            q_m.append(jnp.where((lane_ll >= hh * dk) & (lane_ll < (hh + 1) * dk), q2, 0.0))
            s_mat.append(jnp.dot(q_m[h].astype(bf16), kt_pairs[p].astype(bf16), preferred_element_type=f32))
            v = proj_ref[pl.ds(r0, L), ML_QK_W + h * dv:ML_QK_W + (h + 1) * dv].astype(bf16)
            v_ext.append(jnp.concatenate([v, ones_col], axis=1))
        tot = []
        for h in heads:
            p, hh = divmod(h, 2)
            a = jnp.exp(jnp.where(causal, ew[h][:, :L], -jnp.inf)) * s_mat[h]
            qw = q_m[h] * ew[h][:, L:L + LANES]
            lhs = jnp.concatenate([a.astype(bf16), qw.astype(bf16)], axis=1)
            rhs = jnp.concatenate([v_ext[h], c_pairs[p].astype(bf16)], axis=0)
            tot.append(jnp.dot(lhs, rhs, preferred_element_type=f32))
        for h in heads:
            num = tot[h][:, :dv]
            lhs2 = jnp.concatenate([(num * num).astype(bf16), tot[h][:, dv:].astype(bf16)], axis=1)
            r2 = jnp.dot(lhs2, rhs2, preferred_element_type=f32)
            inv = 1.0 / jnp.maximum(jnp.abs(r2[:, dv:]), ew[h][:, L + LANES:])
            hn = (num * inv) * lax.rsqrt(r2[:, :dv] * inv * inv + HEAD_NORM_EPS)
            o_pre = proj_ref[pl.ds(r0, L), o_base + h * dv:o_base + (h + 1) * dv]
            h_ref[pl.ds(r0, L), h * dv:(h + 1) * dv] = (
                jax.nn.sigmoid(o_pre) * hn * norm_g_ref[:, h * dv:(h + 1) * dv])
        for h in heads:
            p, hh = divmod(h, 2)
            rows = slice(hh * dk, (hh + 1) * dk)
            kg = (kt_pairs[p][rows, :] * gr[h:h + 1, :]).astype(bf16)
            gs = gr[nh + h:nh + h + 1, :]
            c_ref[p, rows, :] = (jnp.concatenate([gs, gs], axis=1) * c_pairs[p][rows, :]
                                 + jnp.dot(kg, v_ext[h], preferred_element_type=f32))
        return carry

    lax.fori_loop(0, n_chunks, chunk_body, 0, unroll=True)

    sub = ts // MIX_SUBTILES
    mixes = [jnp.dot(h_ref[k * sub:(k + 1) * sub, :].astype(bf16), w_out_ref[...], preferred_element_type=f32)
             for k in range(MIX_SUBTILES)]
    x1 = jnp.concatenate([_layer_norm(ALPHA * x[k * sub:(k + 1) * sub, :] + mixes[k], ln_g_ref[...], ln_b_ref[...])
                          for k in range(MIX_SUBTILES)], axis=0)
    meta = _route_and_rank(x1, rw_hi_ref[...], rw_lo_ref[...], rb_ref[...], count_ref)
    out_ref[:, :d] = x1
    out_ref[:, d:] = jnp.concatenate([meta, jnp.zeros((LANES - SUBLANES, meta.shape[1]), meta.dtype)], axis=0).T
    meta_t_ref[...] = meta
    _cast_slabs((wg32_ref, wu32_ref, wd32_ref), (wg16_ref, wu16_ref, wd16_ref))


def _mlstm_mixer_layer(x2d, seq, w_in, b_gate, norm_g, w_out, ln_g, ln_b, rw_hi, rw_lo, rb, layer,
                       expert_weights):
    t, d = x2d.shape
    ts = MIX_TILE
    nh = ML_HEADS
    cast_args, cast_in, cast_shapes, cast_out = _cast_slab_specs(layer, t // ts, expert_weights)
    assert ML_CHUNK == LANES and ts % ML_CHUNK == 0 and 2 * ML_QK_DIM == LANES and ML_V_DIM == LANES
    n_qkvo = 2 * ML_QK_W + 2 * ML_V_W
    n_qvo = ML_QK_W + 2 * ML_V_W
    w_qvo = jnp.concatenate([w_in[:, :ML_QK_W], w_in[:, 2 * ML_QK_W:n_qkvo]], axis=1).astype(jnp.bfloat16)
    w_g = w_in[:, n_qkvo:].astype(jnp.float32)
    wgt_hi, wgt_lo = _split_bf16(w_g.T)
    wkt = jnp.concatenate([w_in[:, ML_QK_W:2 * ML_QK_W].T.astype(jnp.bfloat16), wgt_hi, wgt_lo], axis=0)
    bg_row = b_gate.astype(jnp.float32)[:, None]
    return pl.pallas_call(
        functools.partial(_mlstm_mixer_kernel, tiles_per_seq=seq // ts),
        out_shape=(jax.ShapeDtypeStruct((t, ROW_W), jnp.float32),
                   jax.ShapeDtypeStruct((SUBLANES, t), jnp.float32),
                   jax.ShapeDtypeStruct((CLASS_ROWS, LANES), jnp.float32), *cast_shapes),
        grid=(t // ts,),
        in_specs=[pl.BlockSpec((ts, d), lambda i: (i, 0)),
                  _const_spec((d, n_qvo)), _const_spec((ML_QK_W + 4 * nh, d)),
                  _const_spec((2 * nh, d)),
                  _const_spec((2 * nh, 1)),
                  _const_spec((1, d)), _const_spec((d, d)),
                  _const_spec((1, d)), _const_spec((1, d)),
                  _const_spec((d, LANES)), _const_spec((d, LANES)), _const_spec((N_EXPERTS, 1)), *cast_in],
        out_specs=(pl.BlockSpec((ts, ROW_W), lambda i: (i, 0)),
                   pl.BlockSpec((SUBLANES, ts), lambda i: (0, i)),
                   _const_spec((CLASS_ROWS, LANES)), *cast_out),
        scratch_shapes=[pltpu.VMEM((ts, n_qvo), jnp.float32),
                        pltpu.VMEM((ts // ML_CHUNK, ML_QK_W, ML_CHUNK), jnp.float32),
                        pltpu.VMEM((ts // ML_CHUNK, 2 * nh, ML_CHUNK), jnp.float32),
                        pltpu.VMEM((ts, LANES), jnp.bfloat16),
                        pltpu.VMEM((ts // ML_CHUNK, LANES, ML_CHUNK), jnp.bfloat16),
                        pltpu.VMEM((ts, d), jnp.float32),
                        pltpu.VMEM((nh // 2, 2 * ML_QK_DIM, 2 * ML_V_DIM), jnp.float32),
                        pltpu.VMEM((nh, ML_CHUNK), jnp.float32)],
        compiler_params=pltpu.CompilerParams(dimension_semantics=("arbitrary",),
                                             vmem_limit_bytes=VMEM_LIMIT_BYTES),
        name="mlstm_mixer_route",
    )(x2d, w_qvo, wkt, wgt_hi, bg_row, norm_g[None, :], w_out.astype(jnp.bfloat16),
      ln_g, ln_b, rw_hi, rw_lo, rb, *cast_args)


def _for_rows(n, fn):
    groups = n // ROW_UNROLL

    def group_body(g, c):
        for k in range(ROW_UNROLL):
            fn(g * ROW_UNROLL + k)
        return c

    def row_body(r, c):
        fn(r)
        return c

    lax.fori_loop(0, groups, group_body, 0)
    lax.fori_loop(groups * ROW_UNROLL, n, row_body, 0)


def _expert_kernel(dest_ref, elo_ref, ehi_ref, nvalid_ref,
                   x_hbm, wg_lo_ref, wu_lo_ref, wd_lo_ref, wg_hi_ref, wu_hi_ref, wd_hi_ref,
                   ln_g_ref, ln_b_ref, out_hbm, xbuf0, xbuf1, obuf0, obuf1, tok_ref, gather_sem, scatter_sem):
    i = pl.program_id(0)
    nb = pl.num_programs(0)
    bm = obuf0.shape[0]
    d = D_MODEL

    @pl.when(i == 0)
    def _():
        def pad_block(b, carry):
            def pad_row(r):
                tok_ref[(b + 1) * bm - 1 - r] = 0

            first = jnp.where(b < nb, nvalid_ref[jnp.minimum(b, nb - 1)], 0)
            is_gathered = (b == 0) | (nvalid_ref[jnp.maximum(b - 1, 0)] > 0)
            _for_rows(jnp.where(is_gathered, bm - first, 0), pad_row)
            return carry

        def place(t, c):
            tok_ref[dest_ref[t]] = t
            return c

        lax.fori_loop(0, nb + 1, pad_block, 0)
        lax.fori_loop(0, x_hbm.shape[0], place, 0, unroll=16)
    xbufs, obufs = (xbuf0, xbuf1), (obuf0, obuf1)
    nv = nvalid_ref[i]
    nv_prev = nvalid_ref[jnp.maximum(i - 1, 0)]

    def start_gather(j, s, rows=range(bm)):
        for r in rows:
            tok = tok_ref[j * bm + r]
            pltpu.make_async_copy(x_hbm.at[pl.ds(tok, 1)], xbufs[s].at[pl.ds(r, 1)], gather_sem.at[s]).start()

    def wait_gather(s):
        pltpu.make_async_copy(xbufs[s].at[pl.ds(0, bm)], xbufs[s].at[pl.ds(0, bm)], gather_sem.at[s]).wait()

    def scatter_copy(j, s, r):
        tok = tok_ref[j * bm + r]
        return pltpu.make_async_copy(obufs[s].at[pl.ds(r, 1)], out_hbm.at[pl.ds(tok, 1)], scatter_sem.at[s])

    def wait_scatter(s, n):
        @pl.when(n == bm)
        def _():
            pltpu.make_async_copy(obufs[s], obufs[s], scatter_sem.at[s]).wait()

        @pl.when(n < bm)
        def _():
            one_row = pltpu.make_async_copy(obufs[s].at[pl.ds(0, 1)], obufs[s].at[pl.ds(0, 1)], scatter_sem.at[s])
            _for_rows(n, lambda r: one_row.wait())

    @pl.when(i == 0)
    def _():
        for buf in xbufs:
            buf[bm:, :] = jnp.zeros((buf.shape[0] - bm, buf.shape[1]), buf.dtype)
        start_gather(0, 0)

    def step(s):
        @pl.when((i == 0) | (nv_prev > 0))
        def _():
            wait_gather(s)

        @pl.when(nv > 0)
        def _():
            xb = xbufs[s][0:bm, :]
            x = xb[:, :d]
            g_lo = xb[:, d + META_G_LO:d + META_G_LO + 1]
            g_hi = xb[:, d + META_G_HI:d + META_G_HI + 1]
            x16 = x.astype(jnp.bfloat16)
            n_portions = 2 * GATE_CHUNKS

            def gate_proj(w_ref, k0):
                g = jnp.dot(x16, w_ref[0], preferred_element_type=jnp.float32)
                width = g.shape[1] // GATE_CHUNKS
                chunks = []
                for c in range(GATE_CHUNKS):
                    k = k0 + c
                    start_gather(i + 1, 1 - s, range(k * bm // n_portions, (k + 1) * bm // n_portions))
                    zero_row = xbufs[1 - s][bm:bm + 1, 0:LANES]
                    chunks.append(g[:, c * width:(c + 1) * width]
                                  + jnp.concatenate([zero_row] * (width // LANES), axis=1))
                return jnp.concatenate(chunks, axis=1)

            def ffn(k0, wg_ref, wu_ref, wd_ref):
                g = gate_proj(wg_ref, k0)
                u = jnp.dot(x16, wu_ref[0], preferred_element_type=jnp.float32)
                h = (g * jax.nn.sigmoid(g)) * u
                return jnp.dot(h.astype(jnp.bfloat16), wd_ref[0], preferred_element_type=jnp.float32)

            y = (g_lo * ffn(0, wg_lo_ref, wu_lo_ref, wd_lo_ref)
                 + g_hi * ffn(GATE_CHUNKS, wg_hi_ref, wu_hi_ref, wd_hi_ref))
            obufs[s][...] = _layer_norm(ALPHA * x + y, ln_g_ref[...], ln_b_ref[...])

            @pl.when(nv == bm)
            def _():
                for r in range(bm):
                    scatter_copy(i, s, r).start()

            @pl.when(nv < bm)
            def _():
                _for_rows(nv, lambda r: scatter_copy(i, s, r).start())

        @pl.when(i > 0)
        def _():
            wait_scatter(1 - s, nv_prev)

        @pl.when(i == nb - 1)
        def _():
            wait_scatter(s, nv)

            @pl.when(nv > 0)
            def _():
                wait_gather(1 - s)

    for s in range(2):
        pl.when(i % 2 == s)(functools.partial(step, s))


def _moe_layer(xext, meta_t, counts, w_gate, w_up, w_down, ln_g, ln_b):
    t = xext.shape[0]
    d, f, bm = D_MODEL, D_EXPERT, EXPERT_BLOCK
    n_blocks = t // bm + N_CLASSES
    n_rows = n_blocks * bm

    cls = meta_t[META_CLASS].astype(jnp.int32)
    rank = meta_t[META_RANK].astype(jnp.int32)
    cnt = counts[:N_CLASSES, 0].astype(jnp.int32)
    cls_blocks = (cnt + bm - 1) // bm
    blk_end = jnp.cumsum(cls_blocks)
    blk_start = blk_end - cls_blocks
    total_blocks = blk_end[-1]
    row_start = jnp.sum(jnp.where(cls[None, :] == jnp.arange(N_CLASSES, dtype=jnp.int32)[:, None],
                                  blk_start[:, None] * bm, 0), axis=0)
    dest = (row_start + rank).astype(jnp.int32)
    blk = jnp.arange(n_blocks, dtype=jnp.int32)
    b = jnp.minimum(blk, total_blocks - 1)[None, :]
    in_cls = (b >= blk_start[:, None]) & (b < blk_end[:, None])
    pick = lambda table: jnp.sum(jnp.where(in_cls, table[:, None], 0), axis=0).astype(jnp.int32)
    nvalid = jnp.clip(pick(cnt) - (blk - pick(blk_start)) * bm, 0, bm)
    nvalid = jnp.where(blk < total_blocks, nvalid, 0).astype(jnp.int32)
    e_lo = pick(jnp.asarray(CLASS_E_LO))
    e_hi = pick(jnp.asarray(CLASS_E_HI))

    w_gate, w_up, w_down = (w_gate.reshape(N_EXPERTS, d, f), w_up.reshape(N_EXPERTS, d, f),
                            w_down.reshape(N_EXPERTS, f, d))
    w_lo = lambda shape: pl.BlockSpec((1,) + shape, lambda i, tok, elo, ehi, nv: (elo[i], 0, 0))
    w_hi = lambda shape: pl.BlockSpec((1,) + shape, lambda i, tok, elo, ehi, nv: (ehi[i], 0, 0))
    vec = pl.BlockSpec((1, d), lambda i, tok, elo, ehi, nv: (0, 0))
    grid_spec = pltpu.PrefetchScalarGridSpec(
        num_scalar_prefetch=4,
        grid=(n_blocks,),
        in_specs=[pl.BlockSpec(memory_space=pl.ANY),
                  w_lo((d, f)), w_lo((d, f)), w_lo((f, d)),
                  w_hi((d, f)), w_hi((d, f)), w_hi((f, d)),
                  vec, vec],
        out_specs=pl.BlockSpec(memory_space=pl.ANY),
        scratch_shapes=[pltpu.VMEM((bm + SUBLANES, ROW_W), jnp.float32), pltpu.VMEM((bm + SUBLANES, ROW_W), jnp.float32),
                        pltpu.VMEM((bm, d), jnp.float32), pltpu.VMEM((bm, d), jnp.float32),
                        pltpu.SMEM((n_rows + bm,), jnp.int32),
                        pltpu.SemaphoreType.DMA((2,)),
                        pltpu.SemaphoreType.DMA((2,))],
    )
    return pl.pallas_call(
        _expert_kernel,
        out_shape=jax.ShapeDtypeStruct((t, d), jnp.float32),
        grid_spec=grid_spec,
        compiler_params=pltpu.CompilerParams(dimension_semantics=("arbitrary",),
                                             vmem_limit_bytes=VMEM_LIMIT_BYTES),
        name="expert_pair_ffn",
    )(dest, e_lo, e_hi, nvalid, xext, w_gate, w_up, w_down, w_gate, w_up, w_down, ln_g, ln_b)


def kernel(x, conv_w_in, conv_w, conv_w_out, ml_w_in, ml_b_gate, ml_norm_g, ml_w_out, ln_mix_g, ln_mix_b,
           ln_ffn_g, ln_ffn_b, router_w, router_b, exp_w_gate, exp_w_up, exp_w_down):
    bsz, seq, d = x.shape
    assert d == D_MODEL and seq % MIX_TILE == 0 and (bsz * seq) % EXPERT_BLOCK == 0
    bf16 = jnp.bfloat16
    vec = lambda a: a.astype(jnp.float32)[None, :]
    (rw_hi, rw_lo), rb = _prep_router(router_w, router_b)
    x2d = x.reshape(bsz * seq, d).astype(jnp.float32)
    expert_weights = (exp_w_gate.astype(jnp.float32), exp_w_up.astype(jnp.float32),
                      exp_w_down.astype(jnp.float32))
    for i in range(DEPTH):
        j = i // 2
        if i % 2 == 0:
            xext, meta_t, counts, *w16 = _conv_mixer_layer(
                x2d, seq, conv_w_in[j].astype(bf16), _pad_rows(conv_w[j].astype(jnp.float32)),
                conv_w_out[j].astype(bf16), vec(ln_mix_g[i]), vec(ln_mix_b[i]), rw_hi, rw_lo, rb,
                i, expert_weights)
        else:
            xext, meta_t, counts, *w16 = _mlstm_mixer_layer(
                x2d, seq, ml_w_in[j], ml_b_gate[j], ml_norm_g[j].astype(jnp.float32), ml_w_out[j],
                vec(ln_mix_g[i]), vec(ln_mix_b[i]), rw_hi, rw_lo, rb, i, expert_weights)
        x2d = _moe_layer(xext, meta_t, counts, *w16, vec(ln_ffn_g[i]), vec(ln_ffn_b[i]))
    return x2d.reshape(bsz, seq, d).astype(x.dtype)
```
